```python
import math
import jax
import jax.numpy as jnp
from jax import lax
import numpy as np

D_MODEL = 1024
BATCH = 4
SEQ = 8192
DEPTH = 4

GRID_W = 64
CTX_LEN = 256
N_MLSTM_HEADS = 4
MLSTM_HEAD_DIM = 64
MLSTM_WIDTH = N_MLSTM_HEADS * MLSTM_HEAD_DIM
MLSTM_CHUNK = 64
N_DIFF_HEADS = 4
DIFF_QK_DIM = 64
DIFF_V_DIM = 2 * DIFF_QK_DIM
DIFF_WIDTH = N_DIFF_HEADS * DIFF_V_DIM
Q_BLOCK = 128
ROPE_BASE = 10000.0
CONV_WIDTH = 256
CONV_KERNEL = 31
MIX_WIDTH = MLSTM_WIDTH + DIFF_WIDTH + CONV_WIDTH
IN_SPLITS = (MLSTM_WIDTH, MLSTM_WIDTH, MLSTM_WIDTH, MLSTM_WIDTH, 4 * N_MLSTM_HEADS,
             2 * N_DIFF_HEADS * DIFF_QK_DIM, 2 * N_DIFF_HEADS * DIFF_QK_DIM, DIFF_WIDTH,
             2 * CONV_WIDTH)
D_IN = sum(IN_SPLITS)
D_FF = 2816
N_EXPERTS = 8
TOP_K = 2
D_FF_EXPERT = 1408
EPS = 1e-6
M_INIT = -1e30

kernel_name = 'hymba_style_mlstm_diffattn_conformer_moe_dit'

F32 = jnp.float32


def rmsnorm(x, g):
    x32 = x.astype(F32)
    y = x32 * lax.rsqrt(jnp.mean(x32 * x32, axis=-1, keepdims=True) + EPS)
    return (y * g.astype(F32)).astype(x.dtype)


def layernorm_f32(x, g, b):
    x32 = x.astype(F32)
    mu = jnp.mean(x32, axis=-1, keepdims=True)
    var = jnp.mean(jnp.square(x32 - mu), axis=-1, keepdims=True)
    return (x32 - mu) * lax.rsqrt(var + EPS) * g.astype(F32) + b.astype(F32)


def modulate(h, shift, scale):
    return h * (1 + scale) + shift


def split_cols(p):
    return jnp.split(p, np.cumsum(IN_SPLITS)[:-1].tolist(), axis=-1)


def rope_tables(n_tokens):
    rows = n_tokens // GRID_W
    row = jnp.repeat(jnp.arange(rows, dtype=F32), GRID_W)
    col = jnp.tile(jnp.arange(GRID_W, dtype=F32), rows)
    per_axis = DIFF_QK_DIM // 2
    inv = ROPE_BASE ** (-jnp.arange(0, per_axis, 2, dtype=F32) / per_axis)
    ang = jnp.stack([row[:, None] * inv, col[:, None] * inv], axis=1)
    return jnp.cos(ang), jnp.sin(ang)


def apply_axial_rope(x, cos, sin):
    shp = x.shape
    xr = x.astype(F32).reshape(shp[:-1] + (2, 2, DIFF_QK_DIM // 4))
    x1, x2 = xr[..., 0, :], xr[..., 1, :]
    out = jnp.stack([x1 * cos - x2 * sin, x2 * cos + x1 * sin], axis=-2)
    return out.reshape(shp).astype(x.dtype)


def mlstm_scan(q, k, v, log_i, log_f, state):
    bsz, nh, t_len, dh = q.shape
    nc = t_len // MLSTM_CHUNK
    lower = jnp.tril(jnp.ones((MLSTM_CHUNK, MLSTM_CHUNK), dtype=bool))

    def to_chunks(a):
        return jnp.moveaxis(a.reshape(a.shape[:2] + (nc, MLSTM_CHUNK) + a.shape[3:]), 2, 0)

    def step(carry, inp):
        c_st, n_st, m_st = carry
        qc, kc, vc, lic, lfc = inp
        b = jnp.cumsum(lfc, axis=-1)
        dmat = b[..., :, None] - b[..., None, :] + lic[..., None, :]
        dmat = jnp.where(lower, dmat, -jnp.inf)
        inter = b + m_st[..., None]
        m_t = jnp.maximum(inter, jnp.max(dmat, axis=-1))
        w = jnp.exp(dmat - m_t[..., None]) * jnp.einsum('bhld,bhsd->bhls', qc, kc)
        e_inter = jnp.exp(inter - m_t)
        num = (e_inter[..., None] * jnp.einsum('bhvd,bhld->bhlv', c_st, qc)
               + jnp.einsum('bhls,bhsv->bhlv', w, vc))
        den = e_inter * jnp.einsum('bhd,bhld->bhl', n_st, qc) + jnp.sum(w, axis=-1)
        h = num / jnp.maximum(jnp.abs(den), jnp.exp(-m_t))[..., None]
        b_last = b[..., -1]
        g = b_last[..., None] - b + lic
        m_new = jnp.maximum(b_last + m_st, jnp.max(g, axis=-1))
        e_old = jnp.exp(b_last + m_st - m_new)
        e_g = jnp.exp(g - m_new[..., None])
        c_new = e_old[..., None, None] * c_st + jnp.einsum('bhs,bhsv,bhsd->bhvd', e_g, vc, kc)
        n_new = e_old[..., None] * n_st + jnp.einsum('bhs,bhsd->bhd', e_g, kc)
        return (c_new, n_new, m_new), h

    state, hs = lax.scan(step, state, (to_chunks(q), to_chunks(k), to_chunks(v),
                                       to_chunks(log_i), to_chunks(log_f)))
    return jnp.moveaxis(hs, 0, 2).reshape(bsz, nh, t_len, dh), state


def mlstm_mixer(lat, cx, b_gates, g_head, need_ctx):
    def prep(q, k, v, gates):
        bsz, n, _ = q.shape

        def heads(t):
            return t.astype(F32).reshape(bsz, n, N_MLSTM_HEADS, MLSTM_HEAD_DIM).transpose(0, 2, 1, 3)

        g = (gates.astype(F32) + b_gates.astype(F32)).transpose(0, 2, 1)
        i_f, f_f, i_b, f_b = jnp.split(g, 4, axis=1)
        return (heads(q), heads(k) * MLSTM_HEAD_DIM ** -0.5, heads(v),
                (i_f, jax.nn.log_sigmoid(f_f)), (i_b, jax.nn.log_sigmoid(f_b)))

    def flip(a):
        return jnp.flip(a, axis=2)

    def run(q, k, v, gates, state, reverse):
        li, lf = gates
        if reverse:
            h, st = mlstm_scan(flip(q), flip(k), flip(v), flip(li), flip(lf), state)
            return flip(h), st
        return mlstm_scan(q, k, v, li, lf, state)

    def finish(h, o_pre):
        bsz, nh, n, dh = h.shape
        mu = jnp.mean(h, axis=-1, keepdims=True)
        var = jnp.mean(jnp.square(h - mu), axis=-1, keepdims=True)
        hn = ((h - mu) * lax.rsqrt(var + EPS)).transpose(0, 2, 1, 3).reshape(bsz, n, nh * dh)
        return (jax.nn.sigmoid(o_pre.astype(F32)) * hn * g_head.astype(F32)).astype(o_pre.dtype)

    ql, kl, vl, fwd_l, bwd_l = prep(lat[0], lat[1], lat[2], lat[4])
    qc, kc, vc, fwd_c, bwd_c = prep(cx[0], cx[1], cx[2], cx[4])
    bsz = ql.shape[0]
    init = (jnp.zeros((bsz, N_MLSTM_HEADS, MLSTM_HEAD_DIM, MLSTM_HEAD_DIM), F32),
            jnp.zeros((bsz, N_MLSTM_HEADS, MLSTM_HEAD_DIM), F32),
            jnp.full((bsz, N_MLSTM_HEADS), M_INIT, F32))
    h_cf, st_f = run(qc, kc, vc, fwd_c, init, False)
    h_cb, st_b = run(qc, kc, vc, bwd_c, init, True)
    h_lf, _ = run(ql, kl, vl, fwd_l, st_f, False)
    h_lb, _ = run(ql, kl, vl, bwd_l, st_b, True)
    out_l = finish(h_lf + h_lb, lat[3])
    out_c = finish(h_cf + h_cb, cx[3]) if need_ctx else None
    return out_l, out_c


def diff_attention(q_lat, k_lat, v_lat, q_ctx, k_ctx, v_ctx, lam, lam_init, g_subln, need_ctx):
    bsz, s_len, _ = q_lat.shape

    def heads_qk(t):
        n = t.shape[1]
        return t.reshape(bsz, n, N_DIFF_HEADS, 2, DIFF_QK_DIM).transpose(0, 2, 3, 1, 4)

    def heads_v(t):
        n = t.shape[1]
        return t.reshape(bsz, n, N_DIFF_HEADS, DIFF_V_DIM).transpose(0, 2, 1, 3)

    cos, sin = rope_tables(s_len)
    ql = apply_axial_rope(heads_qk(q_lat), cos, sin)
    kl = apply_axial_rope(heads_qk(k_lat), cos, sin)
    kc = heads_qk(k_ctx)
    vc = heads_v(v_ctx)
    k_all = jnp.concatenate([kl, kc], axis=3)
    v_all = jnp.concatenate([heads_v(v_lat), vc], axis=2)
    scale = DIFF_QK_DIM ** -0.5

    def attend(q, k, v):
        s = jnp.einsum('bhcqd,bhckd->bhcqk', q, k, preferred_element_type=F32) * scale
        p = jax.nn.softmax(s, axis=-1)
        a = p[:, :, 0] - lam * p[:, :, 1]
        return jnp.einsum('bhqk,bhkv->bhqv', a, v.astype(F32))

    def finish(o):
        n = o.shape[2]
        o = rmsnorm(o, g_subln) * (1.0 - lam_init)
        return o.transpose(0, 2, 1, 3).reshape(bsz, n, DIFF_WIDTH).astype(q_lat.dtype)

    nb = s_len // Q_BLOCK
    qb = jnp.moveaxis(ql.reshape(bsz, N_DIFF_HEADS, 2, nb, Q_BLOCK, DIFF_QK_DIM), 3, 0)
    ob = lax.map(lambda qblk: attend(qblk, k_all, v_all), qb)
    o_lat = finish(jnp.moveaxis(ob, 0, 2).reshape(bsz, N_DIFF_HEADS, s_len, DIFF_V_DIM))
    o_ctx = finish(attend(heads_qk(q_ctx), kc, vc)) if need_ctx else None
    return o_lat, o_ctx


def conv_mixer(p, w_dw, b_dw, g_ln, b_ln):
    a, gt = jnp.split(p, 2, axis=-1)
    u = a * jax.nn.sigmoid(gt)
    y = lax.conv_general_dilated(u, w_dw[:, None, :], window_strides=(1,),
                                 padding=[(CONV_KERNEL // 2, CONV_KERNEL // 2)],
                                 dimension_numbers=('NWC', 'WIO', 'NWC'),
                                 feature_group_count=CONV_WIDTH) + b_dw
    return jax.nn.silu(layernorm_f32(y, g_ln, b_ln)).astype(p.dtype)


def swiglu(h, wg, wu, wd):
    return (jax.nn.silu(h @ wg) * (h @ wu)) @ wd


def moe_swiglu(h, w_r, b_r, wg, wu, wd):
    logits = (h @ w_r + b_r).astype(F32)
    probs = jax.nn.softmax(logits, axis=-1)
    top_v, top_i = lax.top_k(probs, TOP_K)
    top_v = top_v / jnp.sum(top_v, axis=-1, keepdims=True)
    combine = jnp.sum(jax.nn.one_hot(top_i, N_EXPERTS, dtype=F32) * top_v[..., None], axis=-2)
    combine = combine.astype(h.dtype)
    y = jnp.zeros_like(h)
    for e in range(N_EXPERTS):
        y = y + combine[..., e:e + 1] * swiglu(h, wg[e], wu[e], wd[e])
    return y


def setup_inputs(seed: int = 0) -> dict:
    key = jax.random.key(seed)
    ks = jax.random.split(key, 32)
    n_dense = (DEPTH + 1) // 2
    n_moe = DEPTH // 2

    def nrm(k, shape, scale):
        return jax.random.normal(k, shape, F32) * scale

    gate_offset = jnp.tile(jnp.repeat(jnp.array([0.0, 3.0], F32), N_MLSTM_HEADS), 2)
    return {
        'x': nrm(ks[0], (BATCH, SEQ, D_MODEL), 1.0),
        'c': nrm(ks[1], (BATCH, D_MODEL), 1.0),
        'ctx': nrm(ks[2], (BATCH, CTX_LEN, D_MODEL), 1.0),
        'c_ctx': nrm(ks[3], (D_MODEL,), 1.0),
        'w_mod': nrm(ks[4], (DEPTH, D_MODEL, 6 * D_MODEL), 0.5 * D_MODEL ** -0.5),
        'b_mod': nrm(ks[5], (DEPTH, 6 * D_MODEL), 0.02),
        'g_norm1': 1.0 + nrm(ks[6], (DEPTH, D_MODEL), 0.02),
        'w_in': nrm(ks[7], (DEPTH, D_MODEL, D_IN), D_MODEL ** -0.5),
        'b_gates': nrm(ks[8], (DEPTH, 4 * N_MLSTM_HEADS), 0.1) + gate_offset,
        'g_mlstm': 1.0 + nrm(ks[9], (DEPTH, MLSTM_WIDTH), 0.02),
        'lambda_q1': nrm(ks[10], (DEPTH, DIFF_QK_DIM), 0.1),
        'lambda_k1': nrm(ks[11], (DEPTH, DIFF_QK_DIM), 0.1),
        'lambda_q2': nrm(ks[12], (DEPTH, DIFF_QK_DIM), 0.1),
        'lambda_k2': nrm(ks[13], (DEPTH, DIFF_QK_DIM), 0.1),
        'g_subln': 1.0 + nrm(ks[14], (DEPTH, DIFF_V_DIM), 0.02),
        'w_dw': nrm(ks[15], (DEPTH, CONV_KERNEL, CONV_WIDTH), CONV_KERNEL ** -0.5),
        'b_dw': nrm(ks[16], (DEPTH, CONV_WIDTH), 0.02),
        'g_conv_ln': 1.0 + nrm(ks[17], (DEPTH, CONV_WIDTH), 0.02),
        'b_conv_ln': nrm(ks[18], (DEPTH, CONV_WIDTH), 0.02),
        'w_out': nrm(ks[19], (DEPTH, MIX_WIDTH, D_MODEL), MIX_WIDTH ** -0.5),
        'g_norm2': 1.0 + nrm(ks[20], (DEPTH, D_MODEL), 0.02),
        'w_ffn_gate': nrm(ks[21], (n_dense, D_MODEL, D_FF), D_MODEL ** -0.5),
        'w_ffn_up': nrm(ks[22], (n_dense, D_MODEL, D_FF), D_MODEL ** -0.5),
        'w_ffn_down': nrm(ks[23], (n_dense, D_FF, D_MODEL), D_FF ** -0.5),
        'w_router': nrm(ks[24], (n_moe, D_MODEL, N_EXPERTS), D_MODEL ** -0.5),
        'b_router': nrm(ks[25], (n_moe, N_EXPERTS), 0.01),
        'w_exp_gate': nrm(ks[26], (n_moe, N_EXPERTS, D_MODEL, D_FF_EXPERT), D_MODEL ** -0.5),
        'w_exp_up': nrm(ks[27], (n_moe, N_EXPERTS, D_MODEL, D_FF_EXPERT), D_MODEL ** -0.5),
        'w_exp_down': nrm(ks[28], (n_moe, N_EXPERTS, D_FF_EXPERT, D_MODEL), D_FF_EXPERT ** -0.5),
        'g_final': 1.0 + nrm(ks[29], (D_MODEL,), 0.02),
    }


def reference(x, c, ctx, c_ctx, w_mod, b_mod, g_norm1, w_in, b_gates, g_mlstm,
              lambda_q1, lambda_k1, lambda_q2, lambda_k2, g_subln, w_dw, b_dw,
              g_conv_ln, b_conv_ln, w_out, g_norm2, w_ffn_gate, w_ffn_up, w_ffn_down,
              w_router, b_router, w_exp_gate, w_exp_up, w_exp_down, g_final):
    cond_x = jax.nn.silu(c)
    cond_c = jax.nn.silu(c_ctx)
    h_ctx = ctx
    for l in range(DEPTH):
        last = l == DEPTH - 1
        mod_x = jnp.split((cond_x @ w_mod[l] + b_mod[l])[:, None, :], 6, axis=-1)
        mod_c = jnp.split((cond_c @ w_mod[l] + b_mod[l])[None, None, :], 6, axis=-1)

        hx = modulate(rmsnorm(x, g_norm1[l]), mod_x[0], mod_x[1])
        hc = modulate(rmsnorm(h_ctx, g_norm1[l]), mod_c[0], mod_c[1])
        px = split_cols(hx @ w_in[l])
        pc = split_cols(hc @ w_in[l])

        m_x, m_c = mlstm_mixer(px[0:5], pc[0:5], b_gates[l], g_mlstm[l], not last)

        lam_init = 0.8 - 0.6 * math.exp(-0.3 * l)
        lam = (jnp.exp(jnp.dot(lambda_q1[l].astype(F32), lambda_k1[l].astype(F32)))
               - jnp.exp(jnp.dot(lambda_q2[l].astype(F32), lambda_k2[l].astype(F32))) + lam_init)
        d_x, d_c = diff_attention(px[5], px[6], px[7], pc[5], pc[6], pc[7],
                                  lam, lam_init, g_subln[l], not last)

        c_x = conv_mixer(px[8], w_dw[l], b_dw[l], g_conv_ln[l], b_conv_ln[l])
        x = x + mod_x[2] * (jnp.concatenate([m_x, d_x, c_x], axis=-1) @ w_out[l])
        if not last:
            c_c = conv_mixer(pc[8], w_dw[l], b_dw[l], g_conv_ln[l], b_conv_ln[l])
            h_ctx = h_ctx + mod_c[2] * (jnp.concatenate([m_c, d_c, c_c], axis=-1) @ w_out[l])

        j = l // 2
        if l % 2 == 0:
            def ffn(h):
                return swiglu(h, w_ffn_gate[j], w_ffn_up[j], w_ffn_down[j])
        else:
            def ffn(h):
                return moe_swiglu(h, w_router[j], b_router[j], w_exp_gate[j], w_exp_up[j], w_exp_down[j])
        x = x + mod_x[5] * ffn(modulate(rmsnorm(x, g_norm2[l]), mod_x[3], mod_x[4]))
        if not last:
            h_ctx = h_ctx + mod_c[5] * ffn(modulate(rmsnorm(h_ctx, g_norm2[l]), mod_c[3], mod_c[4]))
    return rmsnorm(x, g_final)
```

```python
import functools
import math

import jax
import jax.numpy as jnp
import numpy as np
from jax import lax
from jax.experimental import pallas as pl
from jax.experimental.pallas import tpu as pltpu

F32 = jnp.float32
BF16 = jnp.bfloat16
HIGHEST = lax.Precision.HIGHEST

D_MODEL = 1024
DEPTH = 4
GRID_W = 64
CTX_LEN = 256
N_MLSTM_HEADS = 4
MLSTM_HEAD_DIM = 64
MLSTM_WIDTH = N_MLSTM_HEADS * MLSTM_HEAD_DIM
N_DIFF_HEADS = 4
DIFF_QK_DIM = 64
DIFF_V_DIM = 2 * DIFF_QK_DIM
DIFF_WIDTH = N_DIFF_HEADS * DIFF_V_DIM
ROPE_BASE = 10000.0
CONV_WIDTH = 256
CONV_KERNEL = 31
IN_SPLITS = (MLSTM_WIDTH, MLSTM_WIDTH, MLSTM_WIDTH, MLSTM_WIDTH, 4 * N_MLSTM_HEADS,
             2 * N_DIFF_HEADS * DIFF_QK_DIM, 2 * N_DIFF_HEADS * DIFF_QK_DIM, DIFF_WIDTH,
             2 * CONV_WIDTH)
D_FF = 2816
N_EXPERTS = 8
D_FF_EXPERT = 1408
EPS = 1e-6
M_INIT = -1e30
NEG_BIG = -1e30
LOG2E = 1.4426950408889634

LANE = 128
V7X_VMEM_LIMIT = 56 * 1024 * 1024
TOK_TILE = 768
MLSTM_CHUNK = 256
CONV_TILE = 256
CONV_HALO = 16
Q_TILE = 512
MOD_ROWS = 8
VAUG = 16


def _cparams(sem):
    return pltpu.CompilerParams(dimension_semantics=sem, vmem_limit_bytes=V7X_VMEM_LIMIT)


def _sigmoid(v):
    return 1.0 / (1.0 + jnp.exp(-v))


def _log_sigmoid(v):
    return jnp.minimum(v, 0.0) - jnp.log(1.0 + jnp.exp(-jnp.abs(v)))


def _mod_rows(mod_ref, b, ctx_row, k):
    lat = mod_ref[0, pl.ds(b, 1), k * D_MODEL:(k + 1) * D_MODEL]
    ctx = mod_ref[0, ctx_row:ctx_row + 1, k * D_MODEL:(k + 1) * D_MODEL]
    return lat, ctx


def _is_ctx_rows(j, tm, seq):
    rows = j * tm + lax.broadcasted_iota(jnp.int32, (tm, 1), 0)
    return rows >= seq


def _rms_mod(x, g, mod_ref, b, ctx_row, is_ctx, k_shift, k_scale):
    y = x * lax.rsqrt(jnp.mean(x * x, axis=-1, keepdims=True) + EPS) * g
    sh_l, sh_c = _mod_rows(mod_ref, b, ctx_row, k_shift)
    sc_l, sc_c = _mod_rows(mod_ref, b, ctx_row, k_scale)
    shift = jnp.where(is_ctx, sh_c, sh_l)
    scale = jnp.where(is_ctx, sc_c, sc_l)
    return y * (1.0 + scale) + shift


def _mod_kernel(cond_ref, w_ref, b_ref, o_ref):
    c = cond_ref[...]
    s = c * _sigmoid(c)
    o_ref[0] = jnp.dot(s, w_ref[0], preferred_element_type=F32, precision=HIGHEST) + b_ref[0]


def _mod_table(cond, w_mod, b_mod):
    depth = w_mod.shape[0]
    n = w_mod.shape[2] // D_MODEL
    return pl.pallas_call(
        _mod_kernel,
        grid=(depth, n),
        in_specs=[pl.BlockSpec((MOD_ROWS, D_MODEL), lambda l, c: (0, 0)),
                  pl.BlockSpec((1, D_MODEL, D_MODEL), lambda l, c: (l, 0, c)),
                  pl.BlockSpec((1, 1, D_MODEL), lambda l, c: (l, 0, c))],
        out_specs=pl.BlockSpec((1, MOD_ROWS, D_MODEL), lambda l, c: (l, 0, c)),
        out_shape=jax.ShapeDtypeStruct((depth, MOD_ROWS, n * D_MODEL), F32),
        compiler_params=_cparams(("parallel", "parallel")),
        name="mod_table",
    )(cond, w_mod, b_mod.reshape(depth, 1, n * D_MODEL))


def _inproj_kernel(x_ref, mod_ref, g1_ref, ropeT_ref, ropeR_ref,
                   wTm_ref, wTg_ref, bgT_ref, wkm_ref, wg_ref, bg_ref,
                   wTaq_ref, wak_ref, wTav_ref, wcv_ref,
                   qmT_o, km_o, vmT_o, omT_o, gT_o, g_o, qaT_o, ka_o, vaT_o, u_o,
                   *, seq, tm, ctx_row):
    b = pl.program_id(0)
    j = pl.program_id(1)
    is_ctx = _is_ctx_rows(j, tm, seq)
    h = _rms_mod(x_ref[0], g1_ref[...], mod_ref, b, ctx_row, is_ctx, 0, 1)
    hb = h.astype(BF16)
    hT = h.T.astype(BF16)

    mT = jnp.dot(wTm_ref[...], hT, preferred_element_type=F32)
    w = MLSTM_WIDTH
    qmT_o[0] = mT[0:w].astype(BF16)
    vmT_o[0] = mT[w:2 * w].astype(BF16)
    omT_o[0] = mT[2 * w:3 * w].astype(BF16)
    gT = jnp.dot(wTg_ref[...], hT, preferred_element_type=F32) + bgT_ref[...]
    rowi = lax.broadcasted_iota(jnp.int32, gT.shape, 0)
    gT_o[0] = jnp.where((rowi % 8) >= 4, _log_sigmoid(gT), gT)
    km_o[0] = jnp.dot(hb, wkm_ref[...], preferred_element_type=F32).astype(BF16)
    g = jnp.dot(hb, wg_ref[...], preferred_element_type=F32) + bg_ref[...]
    lanei = lax.broadcasted_iota(jnp.int32, g.shape, 1) % LANE
    g_o[0] = jnp.where((lanei >= 4) & (lanei < 8), _log_sigmoid(g), g)

    qT = jnp.dot(wTaq_ref[...], hT, preferred_element_type=F32)
    for grp in range(2 * N_DIFF_HEADS * 2):
        ax = grp % 2
        cos = ropeT_ref[ax * 16:(ax + 1) * 16, :]
        sin = ropeT_ref[32 + ax * 16:32 + (ax + 1) * 16, :]
        x1 = qT[grp * 32:grp * 32 + 16]
        x2 = qT[grp * 32 + 16:grp * 32 + 32]
        qaT_o[0, grp * 32:grp * 32 + 16, :] = (x1 * cos - x2 * sin).astype(BF16)
        qaT_o[0, grp * 32 + 16:grp * 32 + 32, :] = (x2 * cos + x1 * sin).astype(BF16)
    kk = jnp.dot(hb, wak_ref[...], preferred_element_type=F32)
    cosr = ropeR_ref[:, 0:LANE]
    sinr = ropeR_ref[:, LANE:2 * LANE]
    nk = 2 * N_DIFF_HEADS * DIFF_QK_DIM
    for sl in range(nk // LANE):
        k0 = kk[:, sl * LANE:(sl + 1) * LANE]
        k1 = kk[:, nk + sl * LANE:nk + (sl + 1) * LANE]
        ka_o[0, :, sl * LANE:(sl + 1) * LANE] = (k0 * cosr + k1 * sinr).astype(BF16)
    vaT_o[0, 0] = jnp.dot(wTav_ref[...], hT, preferred_element_type=F32).astype(BF16)
    cv = jnp.dot(hb, wcv_ref[...], preferred_element_type=F32)
    u_o[0] = (cv[:, :CONV_WIDTH] * _sigmoid(cv[:, CONV_WIDTH:])).astype(BF16)


def _const_spec(shape):
    nd = len(shape)
    return pl.BlockSpec(shape, lambda *_: (0,) * nd)


def _inproj(xa, mod, layer, g1, ropeT, ropeR, wts, *, seq):
    bsz, t_all, _ = xa.shape
    tm = TOK_TILE
    nt = t_all // tm
    kern = functools.partial(_inproj_kernel, seq=seq, tm=tm, ctx_row=bsz)
    w = MLSTM_WIDTH
    out_shapes = (
        jax.ShapeDtypeStruct((bsz, w, t_all), BF16),
        jax.ShapeDtypeStruct((bsz, t_all, w), BF16),
        jax.ShapeDtypeStruct((bsz, w, t_all), BF16),
        jax.ShapeDtypeStruct((bsz, w, t_all), BF16),
        jax.ShapeDtypeStruct((bsz, 16, t_all), F32),
        jax.ShapeDtypeStruct((bsz, t_all, 2 * LANE), F32),
        jax.ShapeDtypeStruct((bsz, DIFF_WIDTH, t_all), BF16),
        jax.ShapeDtypeStruct((bsz, t_all, DIFF_WIDTH), BF16),
        jax.ShapeDtypeStruct((bsz, nt, DIFF_WIDTH, tm), BF16),
        jax.ShapeDtypeStruct((bsz, t_all, CONV_WIDTH), BF16),
    )
    fm = lambda rows: pl.BlockSpec((1, rows, tm), lambda b, j: (b, 0, j))
    tk = lambda cols: pl.BlockSpec((1, tm, cols), lambda b, j: (b, j, 0))
    out_specs = (fm(w), tk(w), fm(w), fm(w), fm(16), tk(2 * LANE), fm(DIFF_WIDTH), tk(DIFF_WIDTH),
                 pl.BlockSpec((1, 1, DIFF_WIDTH, tm), lambda b, j: (b, j, 0, 0)), tk(CONV_WIDTH))
    in_specs = [
        pl.BlockSpec((1, tm, D_MODEL), lambda b, j: (b, j, 0)),
        pl.BlockSpec((1, MOD_ROWS, 6 * D_MODEL), lambda b, j: (layer, 0, 0)),
        _const_spec((1, D_MODEL)),
        pl.BlockSpec((64, tm), lambda b, j: (0, j)),
        pl.BlockSpec((tm, 2 * LANE), lambda b, j: (j, 0)),
    ] + [_const_spec(a.shape) for a in wts]
    return pl.pallas_call(
        kern, grid=(bsz, nt), in_specs=in_specs, out_specs=out_specs, out_shape=out_shapes,
        compiler_params=_cparams(("parallel", "parallel")), name="inproj",
    )(xa, mod, g1, ropeT, ropeR, *wts)


def _mlstm_kernel(qT_ref, k_ref, vT_ref, gT_ref, g_ref, hT_o, c_scr, m_scr, *, chunk):
    d = pl.program_id(1)
    step = pl.program_id(2)
    L = chunk
    hd = MLSTM_HEAD_DIM

    @pl.when(step == 0)
    def _():
        c_scr[...] = jnp.zeros(c_scr.shape, F32)
        m_scr[...] = jnp.full(m_scr.shape, M_INIT, F32)

    si = lax.broadcasted_iota(jnp.int32, (L, L), 0)
    ti = lax.broadcasted_iota(jnp.int32, (L, L), 1)
    sgn = 1 - 2 * d
    causal = (si - ti) * sgn <= 0
    tri_row = jnp.where(causal, 1.0, 0.0).astype(F32)
    tri_col = jnp.where((ti - si) * sgn <= 0, 1.0, 0.0).astype(F32)

    gT = gT_ref[0]
    gc = g_ref[0]
    b_rows = jnp.dot(gT, tri_row, preferred_element_type=F32, precision=HIGHEST)
    b_cols = jnp.dot(tri_col, gc, preferred_element_type=F32, precision=HIGHEST)
    totals = jnp.sum(gT, axis=1, keepdims=True)

    ones_rows = jnp.ones((VAUG, L), BF16)
    zero64 = jnp.zeros((hd, L), BF16)
    for hh in range(N_MLSTM_HEADS):
        pair, half = hh // 2, hh % 2
        q_pair = qT_ref[0, pair * 2 * hd:(pair + 1) * 2 * hd, :]
        q_h = q_pair[half * hd:(half + 1) * hd]
        q_msk = jnp.concatenate([q_h, zero64] if half == 0 else [zero64, q_h], axis=0)
        k_pair = k_ref[0, :, pair * 2 * hd:(pair + 1) * 2 * hd]
        v_aug = jnp.concatenate([vT_ref[0, hh * hd:(hh + 1) * hd, :], ones_rows], axis=0)

        li_row = gT[hh:hh + 1]
        b_row = b_rows[4 + hh:5 + hh]
        total = totals[4 + hh:5 + hh]
        a_col = gc[:, hh:hh + 1] - b_cols[:, 4 + hh:5 + hh]
        m_st = m_scr[hh, 0:1, :]
        c_st = c_scr[hh]

        dmat = jnp.where(causal, b_row + a_col, -jnp.inf)
        inter = b_row + m_st
        m_t = jnp.maximum(inter, jnp.max(dmat, axis=0, keepdims=True))
        sT = jnp.dot(k_pair, q_msk, preferred_element_type=F32)
        wT = jnp.exp(dmat - m_t) * sT
        e_inter = jnp.exp(inter - m_t)
        intra = jnp.dot(v_aug, wT.astype(BF16), preferred_element_type=F32)
        cq = jnp.dot(c_st.astype(BF16), q_msk, preferred_element_type=F32)
        num = e_inter * cq[0:hd] + intra[0:hd]
        den = e_inter * cq[hd:hd + 1] + jnp.sum(wT, axis=0, keepdims=True)
        hT_o[0, 0, hh * hd:(hh + 1) * hd, :] = num / jnp.maximum(jnp.abs(den), jnp.exp(-m_t))

        g_row = total - b_row + li_row
        m_prev = m_st[:, 0:1]
        m_new = jnp.maximum(total + m_prev, jnp.max(g_row, axis=1, keepdims=True))
        e_old = jnp.exp(total + m_prev - m_new)
        e_g = jnp.exp(g_row - m_new)
        upd = jnp.dot((v_aug.astype(F32) * e_g).astype(BF16), k_pair, preferred_element_type=F32)
        c_scr[hh] = e_old * c_st + upd
        m_scr[hh] = jnp.broadcast_to(m_new, m_scr.shape[1:])


def _mlstm(qmT, km, vmT, gT, g, *, seq):
    bsz, w, t_all = qmT.shape
    L = MLSTM_CHUNK
    nlat = seq // L
    nch = t_all // L
    nctx = nch - nlat

    def chunk_of(d, i):
        fwd = jnp.where(i < nctx, nlat + i, i - nctx)
        bwd = nch - 1 - i
        return jnp.where(d == 0, fwd, bwd)

    kern = functools.partial(_mlstm_kernel, chunk=L)
    return pl.pallas_call(
        kern, grid=(bsz, 2, nch),
        in_specs=[pl.BlockSpec((1, w, L), lambda b, d, i: (b, 0, chunk_of(d, i))),
                  pl.BlockSpec((1, L, w), lambda b, d, i: (b, chunk_of(d, i), 0)),
                  pl.BlockSpec((1, w, L), lambda b, d, i: (b, 0, chunk_of(d, i))),
                  pl.BlockSpec((1, 8, L), lambda b, d, i: (b, d, chunk_of(d, i))),
                  pl.BlockSpec((1, L, LANE), lambda b, d, i: (b, chunk_of(d, i), d))],
        out_specs=pl.BlockSpec((1, 1, w, L), lambda b, d, i: (d, b, 0, chunk_of(d, i))),
        out_shape=jax.ShapeDtypeStruct((2, bsz, w, t_all), F32),
        scratch_shapes=[pltpu.VMEM((N_MLSTM_HEADS, MLSTM_HEAD_DIM + VAUG, LANE), F32),
                        pltpu.VMEM((N_MLSTM_HEADS, 8, L), F32)],
        compiler_params=_cparams(("parallel", "parallel", "arbitrary")), name="mlstm_scan",
    )(qmT, km, vmT, gT, g)


def _attn_kernel(lam_ref, gs_ref, qT_ref, k_ref, vT_ref, o_ref, m_scr, l_scr, acc_scr,
                 *, nch, tk, lam_init):
    tq = qT_ref.shape[2]
    qT = qT_ref[0]
    z = jnp.zeros((DIFF_QK_DIM, tq), BF16)
    rhs = jnp.concatenate([jnp.concatenate([qT[:DIFF_QK_DIM], z], axis=0),
                           jnp.concatenate([z, qT[DIFF_QK_DIM:]], axis=0)], axis=1)
    m_scr[...] = jnp.full(m_scr.shape, NEG_BIG, F32)
    l_scr[...] = jnp.zeros(l_scr.shape, F32)
    acc_scr[...] = jnp.zeros(acc_scr.shape, F32)

    def body(c, carry):
        k = k_ref[0, pl.ds(pl.multiple_of(c * tk, tk), tk), :]
        sT = jnp.dot(k, rhs, preferred_element_type=F32)
        m_old = m_scr[...]
        m_new = jnp.maximum(m_old, jnp.max(sT, axis=0, keepdims=True))
        p = jnp.exp2(sT - m_new)
        alpha = jnp.exp2(m_old - m_new)
        l_scr[...] = alpha * l_scr[...] + jnp.sum(p, axis=0, keepdims=True)
        acc_scr[...] = alpha * acc_scr[...] + jnp.dot(vT_ref[0, c], p.astype(BF16),
                                                      preferred_element_type=F32)
        m_scr[...] = m_new
        return carry

    lax.fori_loop(0, nch, body, 0)

    lv = lam_ref[...]
    lam = (jnp.exp(jnp.sum(lv[0:1] * lv[1:2], axis=1, keepdims=True))
           - jnp.exp(jnp.sum(lv[2:3] * lv[3:4], axis=1, keepdims=True)) + lam_init)
    acc = acc_scr[...]
    l = l_scr[...]
    oT = acc[:, :tq] / l[:, :tq] - lam * (acc[:, tq:] / l[:, tq:])
    oT = oT * lax.rsqrt(jnp.mean(oT * oT, axis=0, keepdims=True) + EPS) * gs_ref[...] * (1.0 - lam_init)
    o_ref[0] = oT.T.astype(BF16)


def _attention(lamv, gs_col, qaT, ka, vaT, out_rows, *, q_tile, q_blk0, n_q, k_rows, k_blk0, v_chunks,
               v_chunk0, v_cols, v_blk0, lam_init):
    bsz = qaT.shape[0]
    kern = functools.partial(_attn_kernel, nch=v_chunks, tk=v_cols, lam_init=lam_init)
    return pl.pallas_call(
        kern, grid=(bsz, N_DIFF_HEADS, n_q),
        in_specs=[_const_spec(lamv.shape), _const_spec((DIFF_V_DIM, 1)),
                  pl.BlockSpec((1, DIFF_V_DIM, q_tile), lambda b, h, i: (b, h, q_blk0 + i)),
                  pl.BlockSpec((1, k_rows, DIFF_V_DIM), lambda b, h, i: (b, k_blk0, h)),
                  pl.BlockSpec((1, v_chunks, DIFF_V_DIM, v_cols), lambda b, h, i: (b, v_chunk0, h, v_blk0))],
        out_specs=pl.BlockSpec((1, q_tile, DIFF_V_DIM), lambda b, h, i: (b, i, h)),
        out_shape=jax.ShapeDtypeStruct((bsz, out_rows, DIFF_WIDTH), BF16),
        scratch_shapes=[pltpu.VMEM((1, 2 * q_tile), F32), pltpu.VMEM((1, 2 * q_tile), F32),
                        pltpu.VMEM((DIFF_V_DIM, 2 * q_tile), F32)],
        compiler_params=_cparams(("parallel", "parallel", "arbitrary")), name="diff_attn",
    )(lamv, gs_col, qaT, ka, vaT)


def _conv_kernel(l_ref, c_ref, r_ref, w_ref, b_ref, g_ref, bb_ref, o_ref, buf, *, seq, tc, nt):
    j = pl.program_id(1)
    start = j * tc
    lvalid = jnp.logical_and(j > 0, start != seq)
    rvalid = jnp.logical_and(j < nt - 1, start + tc != seq)
    hl = CONV_HALO
    buf[0:hl, :] = jnp.where(lvalid, l_ref[0].astype(F32), 0.0)
    buf[hl:hl + tc, :] = c_ref[0].astype(F32)
    buf[hl + tc:2 * hl + tc, :] = jnp.where(rvalid, r_ref[0].astype(F32), 0.0)
    pad = CONV_KERNEL // 2
    sub = 64
    for r0 in range(0, tc, sub):
        acc = jnp.zeros((sub, CONV_WIDTH), F32)
        for tap in range(CONV_KERNEL):
            acc = acc + w_ref[tap:tap + 1, :] * buf[pl.ds(hl - pad + r0 + tap, sub), :]
        y = acc + b_ref[...]
        mu = jnp.mean(y, axis=-1, keepdims=True)
        var = jnp.mean(jnp.square(y - mu), axis=-1, keepdims=True)
        z = (y - mu) * lax.rsqrt(var + EPS) * g_ref[...] + bb_ref[...]
        o_ref[0, r0:r0 + sub, :] = (z * _sigmoid(z)).astype(BF16)


def _conv(u, w_dw, b_dw, g_ln, b_ln, *, seq):
    bsz, t_all, cw = u.shape
    tc = CONV_TILE
    nt = t_all // tc
    r = tc // CONV_HALO
    nhalo = t_all // CONV_HALO
    kern = functools.partial(_conv_kernel, seq=seq, tc=tc, nt=nt)
    row = lambda a: a.reshape(1, cw)
    return pl.pallas_call(
        kern, grid=(bsz, nt),
        in_specs=[pl.BlockSpec((1, CONV_HALO, cw), lambda b, j: (b, jnp.maximum(j * r - 1, 0), 0)),
                  pl.BlockSpec((1, tc, cw), lambda b, j: (b, j, 0)),
                  pl.BlockSpec((1, CONV_HALO, cw), lambda b, j: (b, jnp.minimum((j + 1) * r, nhalo - 1), 0)),
                  _const_spec((CONV_KERNEL, cw)), _const_spec((1, cw)), _const_spec((1, cw)),
                  _const_spec((1, cw))],
        out_specs=pl.BlockSpec((1, tc, cw), lambda b, j: (b, j, 0)),
        out_shape=jax.ShapeDtypeStruct((bsz, t_all, cw), BF16),
        scratch_shapes=[pltpu.VMEM((tc + 2 * CONV_HALO, cw), F32)],
        compiler_params=_cparams(("parallel", "parallel")), name="conv_mixer",
    )(u, u, u, w_dw, row(b_dw), row(g_ln), row(b_ln))


def _mixout_kernel(x_ref, mod_ref, hT_ref, omT_ref, gm_ref, d_ref, c_ref, wm_ref, wd_ref, wc_ref, o_ref,
                   *, seq, tm, ctx_row):
    b = pl.program_id(0)
    j = pl.program_id(1)
    is_ctx = _is_ctx_rows(j, tm, seq)
    hT = hT_ref[0, 0] + hT_ref[1, 0]
    h4 = hT.reshape(N_MLSTM_HEADS, MLSTM_HEAD_DIM, tm)
    mu = jnp.mean(h4, axis=1, keepdims=True)
    var = jnp.mean(jnp.square(h4 - mu), axis=1, keepdims=True)
    hn = ((h4 - mu) * lax.rsqrt(var + EPS)).reshape(MLSTM_WIDTH, tm)
    mT = _sigmoid(omT_ref[0].astype(F32)) * hn * gm_ref[...]
    m = mT.T.astype(BF16)
    y = (jnp.dot(m, wm_ref[...], preferred_element_type=F32)
         + jnp.dot(d_ref[0], wd_ref[...], preferred_element_type=F32)
         + jnp.dot(c_ref[0], wc_ref[...], preferred_element_type=F32))
    g_l, g_c = _mod_rows(mod_ref, b, ctx_row, 2)
    o_ref[0] = x_ref[0] + jnp.where(is_ctx, g_c, g_l) * y


def _mixout(xa, mod, layer, hT, omT, gm_col, d, cx, wm, wd, wc, *, seq):
    bsz, t_all, _ = xa.shape
    tm = TOK_TILE
    nt = t_all // tm
    kern = functools.partial(_mixout_kernel, seq=seq, tm=tm, ctx_row=bsz)
    w = MLSTM_WIDTH
    return pl.pallas_call(
        kern, grid=(bsz, nt),
        in_specs=[pl.BlockSpec((1, tm, D_MODEL), lambda b, j: (b, j, 0)),
                  pl.BlockSpec((1, MOD_ROWS, 6 * D_MODEL), lambda b, j: (layer, 0, 0)),
                  pl.BlockSpec((2, 1, w, tm), lambda b, j: (0, b, 0, j)),
                  pl.BlockSpec((1, w, tm), lambda b, j: (b, 0, j)),
                  _const_spec((w, 1)),
                  pl.BlockSpec((1, tm, DIFF_WIDTH), lambda b, j: (b, j, 0)),
                  pl.BlockSpec((1, tm, CONV_WIDTH), lambda b, j: (b, j, 0)),
                  _const_spec(wm.shape), _const_spec(wd.shape), _const_spec(wc.shape)],
        out_specs=pl.BlockSpec((1, tm, D_MODEL), lambda b, j: (b, j, 0)),
        out_shape=jax.ShapeDtypeStruct(xa.shape, F32),
        compiler_params=_cparams(("parallel", "parallel")), name="mix_out",
    )(xa, mod, hT, omT, gm_col, d, cx, wm, wd, wc)


def _swiglu_partial(hb, wg, wu, wd):
    a = jnp.dot(hb, wg, preferred_element_type=F32)
    u = jnp.dot(hb, wu, preferred_element_type=F32)
    t = (a * _sigmoid(a) * u).astype(BF16)
    return jnp.dot(t, wd, preferred_element_type=F32)


def _ffn_kernel(x_ref, mod_ref, g2_ref, wg_ref, wu_ref, wd_ref, o_ref, hb_scr, acc_scr,
                *, seq, tm, ctx_row, nf):
    b = pl.program_id(0)
    j = pl.program_id(1)
    f = pl.program_id(2)
    is_ctx = _is_ctx_rows(j, tm, seq)

    @pl.when(f == 0)
    def _():
        h = _rms_mod(x_ref[0], g2_ref[...], mod_ref, b, ctx_row, is_ctx, 3, 4)
        hb_scr[...] = h.astype(BF16)
        acc_scr[...] = jnp.zeros(acc_scr.shape, F32)

    acc_scr[...] += _swiglu_partial(hb_scr[...], wg_ref[0], wu_ref[0], wd_ref[0])

    @pl.when(f == nf - 1)
    def _():
        g_l, g_c = _mod_rows(mod_ref, b, ctx_row, 5)
        o_ref[0] = x_ref[0] + jnp.where(is_ctx, g_c, g_l) * acc_scr[...]


def _ffn(xa, mod, layer, g2, wg, wu, wd, *, seq):
    bsz, t_all, _ = xa.shape
    tm = TOK_TILE
    nt = t_all // tm
    nf, _, tf = wg.shape
    kern = functools.partial(_ffn_kernel, seq=seq, tm=tm, ctx_row=bsz, nf=nf)
    return pl.pallas_call(
        kern, grid=(bsz, nt, nf),
        in_specs=[pl.BlockSpec((1, tm, D_MODEL), lambda b, j, f: (b, j, 0)),
                  pl.BlockSpec((1, MOD_ROWS, 6 * D_MODEL), lambda b, j, f: (layer, 0, 0)),
                  _const_spec((1, D_MODEL)),
                  pl.BlockSpec((1, D_MODEL, tf), lambda b, j, f: (f, 0, 0)),
                  pl.BlockSpec((1, D_MODEL, tf), lambda b, j, f: (f, 0, 0)),
                  pl.BlockSpec((1, tf, D_MODEL), lambda b, j, f: (f, 0, 0))],
        out_specs=pl.BlockSpec((1, tm, D_MODEL), lambda b, j, f: (b, j, 0)),
        out_shape=jax.ShapeDtypeStruct(xa.shape, F32),
        scratch_shapes=[pltpu.VMEM((tm, D_MODEL), BF16), pltpu.VMEM((tm, D_MODEL), F32)],
        compiler_params=_cparams(("parallel", "parallel", "arbitrary")), name="ffn_swiglu",
    )(xa, mod, g2, wg, wu, wd)


def _moe_kernel(x_ref, mod_ref, g2_ref, wr_ref, br_ref, wg_ref, wu_ref, wd_ref, o_ref,
                hb_scr, acc_scr, comb_scr, *, seq, tm, ctx_row, ne):
    b = pl.program_id(0)
    j = pl.program_id(1)
    e = pl.program_id(2)
    is_ctx = _is_ctx_rows(j, tm, seq)
    lane = lax.broadcasted_iota(jnp.int32, (tm, LANE), 1)

    @pl.when(e == 0)
    def _():
        h = _rms_mod(x_ref[0], g2_ref[...], mod_ref, b, ctx_row, is_ctx, 3, 4)
        hb_scr[...] = h.astype(BF16)
        acc_scr[...] = jnp.zeros(acc_scr.shape, F32)
        logits = jnp.dot(h, wr_ref[...], preferred_element_type=F32, precision=HIGHEST) + br_ref[...]
        logits = jnp.where(lane < ne, logits, -jnp.inf)
        ex = jnp.exp(logits - jnp.max(logits, axis=-1, keepdims=True))
        probs = ex / jnp.sum(ex, axis=-1, keepdims=True)
        v1 = jnp.max(probs, axis=-1, keepdims=True)
        i1 = jnp.min(jnp.where(probs == v1, lane, LANE), axis=-1, keepdims=True)
        rest = jnp.where(lane == i1, -1.0, probs)
        v2 = jnp.max(rest, axis=-1, keepdims=True)
        i2 = jnp.min(jnp.where(rest == v2, lane, LANE), axis=-1, keepdims=True)
        tot = v1 + v2
        comb_scr[...] = jnp.where(lane == i1, v1 / tot, 0.0) + jnp.where(lane == i2, v2 / tot, 0.0)

    cw = jnp.sum(jnp.where(lane == e, comb_scr[...], 0.0), axis=-1, keepdims=True)
    acc_scr[...] += cw * _swiglu_partial(hb_scr[...], wg_ref[0], wu_ref[0], wd_ref[0])

    @pl.when(e == ne - 1)
    def _():
        g_l, g_c = _mod_rows(mod_ref, b, ctx_row, 5)
        o_ref[0] = x_ref[0] + jnp.where(is_ctx, g_c, g_l) * acc_scr[...]


def _moe(xa, mod, layer, g2, wr, br, wg, wu, wd, *, seq):
    bsz, t_all, _ = xa.shape
    tm = TOK_TILE
    nt = t_all // tm
    ne, _, fe = wg.shape
    kern = functools.partial(_moe_kernel, seq=seq, tm=tm, ctx_row=bsz, ne=ne)
    return pl.pallas_call(
        kern, grid=(bsz, nt, ne),
        in_specs=[pl.BlockSpec((1, tm, D_MODEL), lambda b, j, e: (b, j, 0)),
                  pl.BlockSpec((1, MOD_ROWS, 6 * D_MODEL), lambda b, j, e: (layer, 0, 0)),
                  _const_spec((1, D_MODEL)), _const_spec(wr.shape), _const_spec(br.shape),
                  pl.BlockSpec((1, D_MODEL, fe), lambda b, j, e: (e, 0, 0)),
                  pl.BlockSpec((1, D_MODEL, fe), lambda b, j, e: (e, 0, 0)),
                  pl.BlockSpec((1, fe, D_MODEL), lambda b, j, e: (e, 0, 0))],
        out_specs=pl.BlockSpec((1, tm, D_MODEL), lambda b, j, e: (b, j, 0)),
        out_shape=jax.ShapeDtypeStruct(xa.shape, F32),
        scratch_shapes=[pltpu.VMEM((tm, D_MODEL), BF16), pltpu.VMEM((tm, D_MODEL), F32),
                        pltpu.VMEM((tm, LANE), F32)],
        compiler_params=_cparams(("parallel", "parallel", "arbitrary")), name="moe_swiglu",
    )(xa, mod, g2, wr, br, wg, wu, wd)


def _final_kernel(x_ref, g_ref, o_ref):
    x = x_ref[0]
    o_ref[0] = x * lax.rsqrt(jnp.mean(x * x, axis=-1, keepdims=True) + EPS) * g_ref[...]


def _final_norm(xa, g, *, seq):
    bsz = xa.shape[0]
    tf = 512
    return pl.pallas_call(
        _final_kernel, grid=(bsz, seq // tf),
        in_specs=[pl.BlockSpec((1, tf, D_MODEL), lambda b, j: (b, j, 0)), _const_spec((1, D_MODEL))],
        out_specs=pl.BlockSpec((1, tf, D_MODEL), lambda b, j: (b, j, 0)),
        out_shape=jax.ShapeDtypeStruct((bsz, seq, D_MODEL), F32),
        compiler_params=_cparams(("parallel", "parallel")), name="final_norm",
    )(xa, g.reshape(1, D_MODEL))


def _rope_tables(seq, t_all):
    pos = np.arange(seq)
    per_axis = DIFF_QK_DIM // 2
    inv = ROPE_BASE ** (-np.arange(0, per_axis, 2, dtype=np.float32) / per_axis)
    inv = jnp.asarray(inv, F32)
    rowp = jnp.asarray(pos // GRID_W, F32)
    colp = jnp.asarray(pos % GRID_W, F32)
    ang = jnp.stack([rowp[:, None] * inv, colp[:, None] * inv], axis=1)
    cos = jnp.concatenate([jnp.cos(ang), jnp.ones((t_all - seq, 2, 16), F32)], axis=0)
    sin = jnp.concatenate([jnp.sin(ang), jnp.zeros((t_all - seq, 2, 16), F32)], axis=0)
    qscale = (DIFF_QK_DIM ** -0.5) * LOG2E
    ropeT = jnp.concatenate([cos.reshape(t_all, 32).T, sin.reshape(t_all, 32).T], axis=0) * qscale
    cos64 = jnp.concatenate([cos[:, 0], cos[:, 0], cos[:, 1], cos[:, 1]], axis=-1)
    sin64 = jnp.concatenate([-sin[:, 0], sin[:, 0], -sin[:, 1], sin[:, 1]], axis=-1)
    ropeR = jnp.concatenate([cos64, cos64, sin64, sin64], axis=-1)
    return ropeT, ropeR


def _swap_perm():
    idx = np.arange(2 * N_DIFF_HEADS * DIFF_QK_DIM)
    return np.where((idx % 32) < 16, idx + 16, idx - 16)


def _prep_inproj_weights(w_in_l, b_gates_l):
    offs = np.cumsum((0,) + IN_SPLITS)
    col = lambda i: w_in_l[:, offs[i]:offs[i + 1]]
    mq, mk, mv, mo, gt, aq, ak, av, cv = (col(i) for i in range(9))
    nh = N_MLSTM_HEADS
    wTm = jnp.concatenate([mq, mv, mo], axis=1).T.astype(BF16)
    wTg = gt.T.astype(BF16)
    bgT = b_gates_l.reshape(4 * nh, 1).astype(F32)
    wkm = (mk * (MLSTM_HEAD_DIM ** -0.5)).astype(BF16)
    zpad = jnp.zeros((D_MODEL, LANE - 2 * nh), F32)
    wg = jnp.concatenate([gt[:, 0:2 * nh], zpad, gt[:, 2 * nh:4 * nh], zpad], axis=1).astype(BF16)
    bpad = jnp.zeros((LANE - 2 * nh,), F32)
    bg = jnp.concatenate([b_gates_l[0:2 * nh], bpad, b_gates_l[2 * nh:4 * nh], bpad]).reshape(1, 2 * LANE)
    wTaq = aq.T.astype(BF16)
    wak = jnp.concatenate([ak, ak[:, _swap_perm()]], axis=1).astype(BF16)
    wTav = av.T.astype(BF16)
    wcv = cv.astype(BF16)
    return (wTm, wTg, bgT, wkm, wg, bg, wTaq, wak, wTav, wcv)


def kernel(x, c, ctx, c_ctx, w_mod, b_mod, g_norm1, w_in, b_gates, g_mlstm, lambda_q1, lambda_k1,
           lambda_q2, lambda_k2, g_subln, w_dw, b_dw, g_conv_ln, b_conv_ln, w_out, g_norm2,
           w_ffn_gate, w_ffn_up, w_ffn_down, w_router, b_router, w_exp_gate, w_exp_up, w_exp_down,
           g_final):
    bsz, seq, _ = x.shape
    nctx = ctx.shape[1]
    t_all = seq + nctx
    depth = w_mod.shape[0]
    assert nctx == CTX_LEN == MLSTM_CHUNK and bsz + 1 <= MOD_ROWS
    assert t_all % TOK_TILE == 0 and seq % Q_TILE == 0 and seq % CONV_TILE == 0 and seq % GRID_W == 0

    xa = jnp.concatenate([x, ctx], axis=1)
    cond = jnp.concatenate([c, c_ctx[None, :], jnp.zeros((MOD_ROWS - bsz - 1, D_MODEL), F32)], axis=0)
    mod = _mod_table(cond, w_mod, b_mod)
    ropeT, ropeR = _rope_tables(seq, t_all)
    nkc = t_all // TOK_TILE
    lat_rows = seq

    for l in range(depth):
        wts = _prep_inproj_weights(w_in[l], b_gates[l])
        (qmT, km, vmT, omT, gT, g, qaT, ka, vaT, u) = _inproj(
            xa, mod, l, g_norm1[l].reshape(1, D_MODEL), ropeT, ropeR, wts, seq=seq)

        hT = _mlstm(qmT, km, vmT, gT, g, seq=seq)

        lam_init = 0.8 - 0.6 * math.exp(-0.3 * l)
        lamv = jnp.zeros((8, LANE), F32).at[0:4, 0:DIFF_QK_DIM].set(
            jnp.stack([lambda_q1[l], lambda_k1[l], lambda_q2[l], lambda_k2[l]]).astype(F32))
        gs_col = g_subln[l].reshape(DIFF_V_DIM, 1).astype(F32)
        d_lat = _attention(lamv, gs_col, qaT, ka, vaT, lat_rows, q_tile=Q_TILE, q_blk0=0,
                           n_q=seq // Q_TILE, k_rows=t_all, k_blk0=0, v_chunks=nkc, v_chunk0=0,
                           v_cols=TOK_TILE, v_blk0=0, lam_init=lam_init)
        d_ctx = _attention(lamv, gs_col, qaT, ka, vaT, nctx, q_tile=nctx, q_blk0=seq // nctx, n_q=1,
                           k_rows=nctx, k_blk0=seq // nctx, v_chunks=1, v_chunk0=nkc - 1,
                           v_cols=nctx, v_blk0=TOK_TILE // nctx - 1, lam_init=lam_init)
        d = jnp.concatenate([d_lat, d_ctx], axis=1)

        cx = _conv(u, w_dw[l], b_dw[l], g_conv_ln[l], b_conv_ln[l], seq=seq)

        wo = w_out[l].astype(BF16)
        xa = _mixout(xa, mod, l, hT, omT, g_mlstm[l].reshape(MLSTM_WIDTH, 1).astype(F32), d, cx,
                     wo[0:MLSTM_WIDTH], wo[MLSTM_WIDTH:MLSTM_WIDTH + DIFF_WIDTH],
                     wo[MLSTM_WIDTH + DIFF_WIDTH:], seq=seq)

        jj = l // 2
        g2 = g_norm2[l].reshape(1, D_MODEL)
        if l % 2 == 0:
            nf = 2
            tf = D_FF // nf
            split_cols = lambda w_: w_.astype(BF16).reshape(D_MODEL, nf, tf).transpose(1, 0, 2)
            xa = _ffn(xa, mod, l, g2, split_cols(w_ffn_gate[jj]), split_cols(w_ffn_up[jj]),
                      w_ffn_down[jj].astype(BF16).reshape(nf, tf, D_MODEL), seq=seq)
        else:
            wr = jnp.concatenate([w_router[jj], jnp.zeros((D_MODEL, LANE - N_EXPERTS), F32)], axis=1)
            br = jnp.concatenate([b_router[jj], jnp.zeros((LANE - N_EXPERTS,), F32)]).reshape(1, LANE)
            xa = _moe(xa, mod, l, g2, wr, br, w_exp_gate[jj].astype(BF16), w_exp_up[jj].astype(BF16),
                      w_exp_down[jj].astype(BF16), seq=seq)

    return _final_norm(xa, g_final, seq=seq)
```

```python
import functools
import math

import jax
import jax.numpy as jnp
import numpy as np
from jax import lax
from jax.experimental import pallas as pl
from jax.experimental.pallas import tpu as pltpu

F32 = jnp.float32
BF16 = jnp.bfloat16
HIGHEST = lax.Precision.HIGHEST

D_MODEL = 1024
DEPTH = 4
GRID_W = 64
CTX_LEN = 256
N_MLSTM_HEADS = 4
MLSTM_HEAD_DIM = 64
MLSTM_WIDTH = N_MLSTM_HEADS * MLSTM_HEAD_DIM
N_DIFF_HEADS = 4
DIFF_QK_DIM = 64
DIFF_V_DIM = 2 * DIFF_QK_DIM
DIFF_WIDTH = N_DIFF_HEADS * DIFF_V_DIM
ROPE_BASE = 10000.0
CONV_WIDTH = 256
CONV_KERNEL = 31
IN_SPLITS = (MLSTM_WIDTH, MLSTM_WIDTH, MLSTM_WIDTH, MLSTM_WIDTH, 4 * N_MLSTM_HEADS,
             2 * N_DIFF_HEADS * DIFF_QK_DIM, 2 * N_DIFF_HEADS * DIFF_QK_DIM, DIFF_WIDTH,
             2 * CONV_WIDTH)
D_FF = 2816
N_EXPERTS = 8
D_FF_EXPERT = 1408
EPS = 1e-6
M_INIT = -1e30
NEG_BIG = -1e30
LOG2E = 1.4426950408889634

LANE = 128
V7X_VMEM_LIMIT = 56 * 1024 * 1024
TOK_TILE = 768
MLSTM_CHUNK = 256
CONV_TILE = 256
CONV_HALO = 16
Q_TILE = 512
MOD_ROWS = 8
VAUG = 16
MOE_SUB = 256


def _cparams(sem):
    return pltpu.CompilerParams(dimension_semantics=sem, vmem_limit_bytes=V7X_VMEM_LIMIT)


def _sigmoid(v):
    return 1.0 / (1.0 + jnp.exp(-v))


def _log_sigmoid(v):
    return jnp.minimum(v, 0.0) - jnp.log(1.0 + jnp.exp(-jnp.abs(v)))


def _mod_rows(mod_ref, b, ctx_row, k):
    lat = mod_ref[0, pl.ds(b, 1), k * D_MODEL:(k + 1) * D_MODEL]
    ctx = mod_ref[0, ctx_row:ctx_row + 1, k * D_MODEL:(k + 1) * D_MODEL]
    return lat, ctx


def _is_ctx_rows(j, tm, seq):
    rows = j * tm + lax.broadcasted_iota(jnp.int32, (tm, 1), 0)
    return rows >= seq


def _rms_mod(x, g, mod_ref, b, ctx_row, is_ctx, k_shift, k_scale):
    y = x * lax.rsqrt(jnp.mean(x * x, axis=-1, keepdims=True) + EPS) * g
    sh_l, sh_c = _mod_rows(mod_ref, b, ctx_row, k_shift)
    sc_l, sc_c = _mod_rows(mod_ref, b, ctx_row, k_scale)
    shift = jnp.where(is_ctx, sh_c, sh_l)
    scale = jnp.where(is_ctx, sc_c, sc_l)
    return y * (1.0 + scale) + shift


def _mod_kernel(cond_ref, w_ref, b_ref, o_ref):
    c = cond_ref[...]
    s = c * _sigmoid(c)
    o_ref[0] = jnp.dot(s, w_ref[0], preferred_element_type=F32, precision=HIGHEST) + b_ref[0]


def _mod_table(cond, w_mod, b_mod):
    depth = w_mod.shape[0]
    n = w_mod.shape[2] // D_MODEL
    return pl.pallas_call(
        _mod_kernel,
        grid=(depth, n),
        in_specs=[pl.BlockSpec((MOD_ROWS, D_MODEL), lambda l, c: (0, 0)),
                  pl.BlockSpec((1, D_MODEL, D_MODEL), lambda l, c: (l, 0, c)),
                  pl.BlockSpec((1, 1, D_MODEL), lambda l, c: (l, 0, c))],
        out_specs=pl.BlockSpec((1, MOD_ROWS, D_MODEL), lambda l, c: (l, 0, c)),
        out_shape=jax.ShapeDtypeStruct((depth, MOD_ROWS, n * D_MODEL), F32),
        compiler_params=_cparams(("parallel", "parallel")),
        name="mod_table",
    )(cond, w_mod, b_mod.reshape(depth, 1, n * D_MODEL))


def _inproj_kernel(x_ref, mod_ref, g1_ref, ropeT_ref, ropeR_ref,
                   wTm_ref, wTg_ref, bgT_ref, wkm_ref, wg_ref, bg_ref,
                   wTaq_ref, wak_ref, wTav_ref, wcv_ref,
                   qmT_o, km_o, vmT_o, omT_o, gT_o, g_o, qaT_o, ka_o, vaT_o, u_o,
                   *, seq, tm, ctx_row):
    b = pl.program_id(0)
    j = pl.program_id(1)
    is_ctx = _is_ctx_rows(j, tm, seq)
    h = _rms_mod(x_ref[0], g1_ref[...], mod_ref, b, ctx_row, is_ctx, 0, 1)
    hb = h.astype(BF16)
    hT = h.T.astype(BF16)

    mT = jnp.dot(wTm_ref[...], hT, preferred_element_type=F32)
    w = MLSTM_WIDTH
    qmT_o[0] = mT[0:w].astype(BF16)
    vmT_o[0] = mT[w:2 * w].astype(BF16)
    omT_o[0] = mT[2 * w:3 * w].astype(BF16)
    gT = jnp.dot(wTg_ref[...], hT, preferred_element_type=F32) + bgT_ref[...]
    rowi = lax.broadcasted_iota(jnp.int32, gT.shape, 0)
    gT_o[0] = jnp.where((rowi % 8) >= 4, _log_sigmoid(gT), gT)
    km_o[0] = jnp.dot(hb, wkm_ref[...], preferred_element_type=F32).astype(BF16)
    g = jnp.dot(hb, wg_ref[...], preferred_element_type=F32) + bg_ref[...]
    lanei = lax.broadcasted_iota(jnp.int32, g.shape, 1) % LANE
    g_o[0] = jnp.where((lanei >= 4) & (lanei < 8), _log_sigmoid(g), g)

    qT = jnp.dot(wTaq_ref[...], hT, preferred_element_type=F32)
    for grp in range(2 * N_DIFF_HEADS * 2):
        ax = grp % 2
        cos = ropeT_ref[ax * 16:(ax + 1) * 16, :]
        sin = ropeT_ref[32 + ax * 16:32 + (ax + 1) * 16, :]
        x1 = qT[grp * 32:grp * 32 + 16]
        x2 = qT[grp * 32 + 16:grp * 32 + 32]
        qaT_o[0, grp * 32:grp * 32 + 16, :] = (x1 * cos - x2 * sin).astype(BF16)
        qaT_o[0, grp * 32 + 16:grp * 32 + 32, :] = (x2 * cos + x1 * sin).astype(BF16)
    kk = jnp.dot(hb, wak_ref[...], preferred_element_type=F32)
    cosr = ropeR_ref[:, 0:LANE]
    sinr = ropeR_ref[:, LANE:2 * LANE]
    nk = 2 * N_DIFF_HEADS * DIFF_QK_DIM
    for sl in range(nk // LANE):
        k0 = kk[:, sl * LANE:(sl + 1) * LANE]
        k1 = kk[:, nk + sl * LANE:nk + (sl + 1) * LANE]
        ka_o[0, :, sl * LANE:(sl + 1) * LANE] = (k0 * cosr + k1 * sinr).astype(BF16)
    vaT_o[0, 0] = jnp.dot(wTav_ref[...], hT, preferred_element_type=F32).astype(BF16)
    cv = jnp.dot(hb, wcv_ref[...], preferred_element_type=F32)
    u_o[0] = (cv[:, :CONV_WIDTH] * _sigmoid(cv[:, CONV_WIDTH:])).astype(BF16)


def _const_spec(shape):
    nd = len(shape)
    return pl.BlockSpec(shape, lambda *_: (0,) * nd)


def _inproj(xa, mod, layer, g1, ropeT, ropeR, wts, *, seq):
    bsz, t_all, _ = xa.shape
    tm = TOK_TILE
    nt = t_all // tm
    kern = functools.partial(_inproj_kernel, seq=seq, tm=tm, ctx_row=bsz)
    w = MLSTM_WIDTH
    out_shapes = (
        jax.ShapeDtypeStruct((bsz, w, t_all), BF16),
        jax.ShapeDtypeStruct((bsz, t_all, w), BF16),
        jax.ShapeDtypeStruct((bsz, w, t_all), BF16),
        jax.ShapeDtypeStruct((bsz, w, t_all), BF16),
        jax.ShapeDtypeStruct((bsz, 16, t_all), F32),
        jax.ShapeDtypeStruct((bsz, t_all, 2 * LANE), F32),
        jax.ShapeDtypeStruct((bsz, DIFF_WIDTH, t_all), BF16),
        jax.ShapeDtypeStruct((bsz, t_all, DIFF_WIDTH), BF16),
        jax.ShapeDtypeStruct((bsz, nt, DIFF_WIDTH, tm), BF16),
        jax.ShapeDtypeStruct((bsz, t_all, CONV_WIDTH), BF16),
    )
    fm = lambda rows: pl.BlockSpec((1, rows, tm), lambda b, j: (b, 0, j))
    tk = lambda cols: pl.BlockSpec((1, tm, cols), lambda b, j: (b, j, 0))
    out_specs = (fm(w), tk(w), fm(w), fm(w), fm(16), tk(2 * LANE), fm(DIFF_WIDTH), tk(DIFF_WIDTH),
                 pl.BlockSpec((1, 1, DIFF_WIDTH, tm), lambda b, j: (b, j, 0, 0)), tk(CONV_WIDTH))
    in_specs = [
        pl.BlockSpec((1, tm, D_MODEL), lambda b, j: (b, j, 0)),
        pl.BlockSpec((1, MOD_ROWS, 6 * D_MODEL), lambda b, j: (layer, 0, 0)),
        _const_spec((1, D_MODEL)),
        pl.BlockSpec((64, tm), lambda b, j: (0, j)),
        pl.BlockSpec((tm, 2 * LANE), lambda b, j: (j, 0)),
    ] + [_const_spec(a.shape) for a in wts]
    return pl.pallas_call(
        kern, grid=(bsz, nt), in_specs=in_specs, out_specs=out_specs, out_shape=out_shapes,
        compiler_params=_cparams(("parallel", "parallel")), name="inproj",
    )(xa, mod, g1, ropeT, ropeR, *wts)


def _mlstm_kernel(qT_ref, k_ref, vT_ref, gT_ref, g_ref, hT_o, c_scr, m_scr, *, chunk):
    d = pl.program_id(1)
    step = pl.program_id(2)
    L = chunk
    hd = MLSTM_HEAD_DIM

    @pl.when(step == 0)
    def _():
        c_scr[...] = jnp.zeros(c_scr.shape, F32)
        m_scr[...] = jnp.full(m_scr.shape, M_INIT, F32)

    si = lax.broadcasted_iota(jnp.int32, (L, L), 0)
    ti = lax.broadcasted_iota(jnp.int32, (L, L), 1)
    sgn = 1 - 2 * d
    causal = (si - ti) * sgn <= 0
    tri_row = jnp.where(causal, 1.0, 0.0).astype(F32)
    tri_col = jnp.where((ti - si) * sgn <= 0, 1.0, 0.0).astype(F32)

    gT = gT_ref[0]
    gc = g_ref[0]
    b_rows = jnp.dot(gT, tri_row, preferred_element_type=F32, precision=HIGHEST)
    b_cols = jnp.dot(tri_col, gc, preferred_element_type=F32, precision=HIGHEST)
    totals = jnp.sum(gT, axis=1, keepdims=True)

    ones_rows = jnp.ones((VAUG, L), BF16)
    zero64 = jnp.zeros((hd, L), BF16)
    for hh in range(N_MLSTM_HEADS):
        pair, half = hh // 2, hh % 2
        q_pair = qT_ref[0, pair * 2 * hd:(pair + 1) * 2 * hd, :]
        q_h = q_pair[half * hd:(half + 1) * hd]
        q_msk = jnp.concatenate([q_h, zero64] if half == 0 else [zero64, q_h], axis=0)
        k_pair = k_ref[0, :, pair * 2 * hd:(pair + 1) * 2 * hd]
        v_aug = jnp.concatenate([vT_ref[0, hh * hd:(hh + 1) * hd, :], ones_rows], axis=0)

        li_row = gT[hh:hh + 1]
        b_row = b_rows[4 + hh:5 + hh]
        total = totals[4 + hh:5 + hh]
        a_col = gc[:, hh:hh + 1] - b_cols[:, 4 + hh:5 + hh]
        m_st = m_scr[hh, 0:1, :]
        c_st = c_scr[hh]

        dmat = jnp.where(causal, b_row + a_col, -jnp.inf)
        inter = b_row + m_st
        m_t = jnp.maximum(inter, jnp.max(dmat, axis=0, keepdims=True))
        sT = jnp.dot(k_pair, q_msk, preferred_element_type=F32)
        wT = jnp.exp(dmat - m_t) * sT
        e_inter = jnp.exp(inter - m_t)
        intra = jnp.dot(v_aug, wT.astype(BF16), preferred_element_type=F32)
        cq = jnp.dot(c_st.astype(BF16), q_msk, preferred_element_type=F32)
        num = e_inter * cq[0:hd] + intra[0:hd]
        den = e_inter * cq[hd:hd + 1] + jnp.sum(wT, axis=0, keepdims=True)
        hT_o[0, 0, hh * hd:(hh + 1) * hd, :] = num / jnp.maximum(jnp.abs(den), jnp.exp(-m_t))

        g_row = total - b_row + li_row
        m_prev = m_st[:, 0:1]
        m_new = jnp.maximum(total + m_prev, jnp.max(g_row, axis=1, keepdims=True))
        e_old = jnp.exp(total + m_prev - m_new)
        e_g = jnp.exp(g_row - m_new)
        upd = jnp.dot((v_aug.astype(F32) * e_g).astype(BF16), k_pair, preferred_element_type=F32)
        c_scr[hh] = e_old * c_st + upd
        m_scr[hh] = jnp.broadcast_to(m_new, m_scr.shape[1:])


def _mlstm(qmT, km, vmT, gT, g, *, seq):
    bsz, w, t_all = qmT.shape
    L = MLSTM_CHUNK
    nlat = seq // L
    nch = t_all // L
    nctx = nch - nlat

    def chunk_of(d, i):
        fwd = jnp.where(i < nctx, nlat + i, i - nctx)
        bwd = nch - 1 - i
        return jnp.where(d == 0, fwd, bwd)

    kern = functools.partial(_mlstm_kernel, chunk=L)
    return pl.pallas_call(
        kern, grid=(bsz, 2, nch),
        in_specs=[pl.BlockSpec((1, w, L), lambda b, d, i: (b, 0, chunk_of(d, i))),
                  pl.BlockSpec((1, L, w), lambda b, d, i: (b, chunk_of(d, i), 0)),
                  pl.BlockSpec((1, w, L), lambda b, d, i: (b, 0, chunk_of(d, i))),
                  pl.BlockSpec((1, 8, L), lambda b, d, i: (b, d, chunk_of(d, i))),
                  pl.BlockSpec((1, L, LANE), lambda b, d, i: (b, chunk_of(d, i), d))],
        out_specs=pl.BlockSpec((1, 1, w, L), lambda b, d, i: (d, b, 0, chunk_of(d, i))),
        out_shape=jax.ShapeDtypeStruct((2, bsz, w, t_all), F32),
        scratch_shapes=[pltpu.VMEM((N_MLSTM_HEADS, MLSTM_HEAD_DIM + VAUG, LANE), F32),
                        pltpu.VMEM((N_MLSTM_HEADS, 8, L), F32)],
        compiler_params=_cparams(("parallel", "parallel", "arbitrary")), name="mlstm_scan",
    )(qmT, km, vmT, gT, g)


def _attn_kernel(lam_ref, gs_ref, qT_ref, k_ref, vT_ref, o_ref, *, nch, tk, lam_init):
    tq = qT_ref.shape[2]
    qT = qT_ref[0]
    z = jnp.zeros((DIFF_QK_DIM, tq), BF16)
    rhs = jnp.concatenate([jnp.concatenate([qT[:DIFF_QK_DIM], z], axis=0),
                           jnp.concatenate([z, qT[DIFF_QK_DIM:]], axis=0)], axis=1)
    def scores(c):
        return jnp.dot(k_ref[0, c * tk:(c + 1) * tk, :], rhs, preferred_element_type=F32)

    m = jnp.full((1, 2 * tq), NEG_BIG, F32)
    l = jnp.zeros((1, 2 * tq), F32)
    acc = jnp.zeros((DIFF_V_DIM, 2 * tq), F32)
    sT = scores(0)
    for c in range(nch):
        s_next = scores(c + 1) if c + 1 < nch else None
        m_new = jnp.maximum(m, jnp.max(sT, axis=0, keepdims=True))
        p = jnp.exp2(sT - m_new)
        alpha = jnp.exp2(m - m_new)
        l = alpha * l + jnp.sum(p, axis=0, keepdims=True)
        acc = alpha * acc + jnp.dot(vT_ref[0, c], p.astype(BF16), preferred_element_type=F32)
        m = m_new
        sT = s_next

    lv = lam_ref[...]
    lam = (jnp.exp(jnp.sum(lv[0:1] * lv[1:2], axis=1, keepdims=True))
           - jnp.exp(jnp.sum(lv[2:3] * lv[3:4], axis=1, keepdims=True)) + lam_init)
    oT = acc[:, :tq] / l[:, :tq] - lam * (acc[:, tq:] / l[:, tq:])
    oT = oT * lax.rsqrt(jnp.mean(oT * oT, axis=0, keepdims=True) + EPS) * gs_ref[...] * (1.0 - lam_init)
    o_ref[0] = oT.T.astype(BF16)


def _attention(lamv, gs_col, qaT, ka, vaT, out_rows, *, q_tile, q_blk0, n_q, k_rows, k_blk0, v_chunks,
               v_chunk0, v_cols, v_blk0, lam_init):
    bsz = qaT.shape[0]
    kern = functools.partial(_attn_kernel, nch=v_chunks, tk=v_cols, lam_init=lam_init)
    return pl.pallas_call(
        kern, grid=(bsz, N_DIFF_HEADS, n_q),
        in_specs=[_const_spec(lamv.shape), _const_spec((DIFF_V_DIM, 1)),
                  pl.BlockSpec((1, DIFF_V_DIM, q_tile), lambda b, h, i: (b, h, q_blk0 + i)),
                  pl.BlockSpec((1, k_rows, DIFF_V_DIM), lambda b, h, i: (b, k_blk0, h)),
                  pl.BlockSpec((1, v_chunks, DIFF_V_DIM, v_cols), lambda b, h, i: (b, v_chunk0, h, v_blk0))],
        out_specs=pl.BlockSpec((1, q_tile, DIFF_V_DIM), lambda b, h, i: (b, i, h)),
        out_shape=jax.ShapeDtypeStruct((bsz, out_rows, DIFF_WIDTH), BF16),
        compiler_params=_cparams(("parallel", "parallel", "arbitrary")), name="diff_attn",
    )(lamv, gs_col, qaT, ka, vaT)


def _conv_kernel(l_ref, c_ref, r_ref, w_ref, b_ref, g_ref, bb_ref, o_ref, buf, *, seq, tc, nt):
    j = pl.program_id(1)
    start = j * tc
    lvalid = jnp.logical_and(j > 0, start != seq)
    rvalid = jnp.logical_and(j < nt - 1, start + tc != seq)
    hl = CONV_HALO
    buf[0:hl, :] = jnp.where(lvalid, l_ref[0].astype(F32), 0.0)
    buf[hl:hl + tc, :] = c_ref[0].astype(F32)
    buf[hl + tc:2 * hl + tc, :] = jnp.where(rvalid, r_ref[0].astype(F32), 0.0)
    pad = CONV_KERNEL // 2
    sub = 64
    for r0 in range(0, tc, sub):
        acc = jnp.zeros((sub, CONV_WIDTH), F32)
        for tap in range(CONV_KERNEL):
            acc = acc + w_ref[tap:tap + 1, :] * buf[pl.ds(hl - pad + r0 + tap, sub), :]
        y = acc + b_ref[...]
        mu = jnp.mean(y, axis=-1, keepdims=True)
        var = jnp.mean(jnp.square(y - mu), axis=-1, keepdims=True)
        z = (y - mu) * lax.rsqrt(var + EPS) * g_ref[...] + bb_ref[...]
        o_ref[0, r0:r0 + sub, :] = (z * _sigmoid(z)).astype(BF16)


def _conv(u, w_dw, b_dw, g_ln, b_ln, *, seq):
    bsz, t_all, cw = u.shape
    tc = CONV_TILE
    nt = t_all // tc
    r = tc // CONV_HALO
    nhalo = t_all // CONV_HALO
    kern = functools.partial(_conv_kernel, seq=seq, tc=tc, nt=nt)
    row = lambda a: a.reshape(1, cw)
    return pl.pallas_call(
        kern, grid=(bsz, nt),
        in_specs=[pl.BlockSpec((1, CONV_HALO, cw), lambda b, j: (b, jnp.maximum(j * r - 1, 0), 0)),
                  pl.BlockSpec((1, tc, cw), lambda b, j: (b, j, 0)),
                  pl.BlockSpec((1, CONV_HALO, cw), lambda b, j: (b, jnp.minimum((j + 1) * r, nhalo - 1), 0)),
                  _const_spec((CONV_KERNEL, cw)), _const_spec((1, cw)), _const_spec((1, cw)),
                  _const_spec((1, cw))],
        out_specs=pl.BlockSpec((1, tc, cw), lambda b, j: (b, j, 0)),
        out_shape=jax.ShapeDtypeStruct((bsz, t_all, cw), BF16),
        scratch_shapes=[pltpu.VMEM((tc + 2 * CONV_HALO, cw), F32)],
        compiler_params=_cparams(("parallel", "parallel")), name="conv_mixer",
    )(u, u, u, w_dw, row(b_dw), row(g_ln), row(b_ln))


def _mixout_kernel(x_ref, mod_ref, hT_ref, omT_ref, gm_ref, d_ref, c_ref, wm_ref, wd_ref, wc_ref, o_ref,
                   *, seq, tm, ctx_row):
    b = pl.program_id(0)
    j = pl.program_id(1)
    is_ctx = _is_ctx_rows(j, tm, seq)
    hT = hT_ref[0, 0] + hT_ref[1, 0]
    h4 = hT.reshape(N_MLSTM_HEADS, MLSTM_HEAD_DIM, tm)
    mu = jnp.mean(h4, axis=1, keepdims=True)
    var = jnp.mean(jnp.square(h4 - mu), axis=1, keepdims=True)
    hn = ((h4 - mu) * lax.rsqrt(var + EPS)).reshape(MLSTM_WIDTH, tm)
    mT = _sigmoid(omT_ref[0].astype(F32)) * hn * gm_ref[...]
    m = mT.T.astype(BF16)
    y = (jnp.dot(m, wm_ref[...], preferred_element_type=F32)
         + jnp.dot(d_ref[0], wd_ref[...], preferred_element_type=F32)
         + jnp.dot(c_ref[0], wc_ref[...], preferred_element_type=F32))
    g_l, g_c = _mod_rows(mod_ref, b, ctx_row, 2)
    o_ref[0] = x_ref[0] + jnp.where(is_ctx, g_c, g_l) * y


def _mixout(xa, mod, layer, hT, omT, gm_col, d, cx, wm, wd, wc, *, seq):
    bsz, t_all, _ = xa.shape
    tm = TOK_TILE
    nt = t_all // tm
    kern = functools.partial(_mixout_kernel, seq=seq, tm=tm, ctx_row=bsz)
    w = MLSTM_WIDTH
    return pl.pallas_call(
        kern, grid=(bsz, nt),
        in_specs=[pl.BlockSpec((1, tm, D_MODEL), lambda b, j: (b, j, 0)),
                  pl.BlockSpec((1, MOD_ROWS, 6 * D_MODEL), lambda b, j: (layer, 0, 0)),
                  pl.BlockSpec((2, 1, w, tm), lambda b, j: (0, b, 0, j)),
                  pl.BlockSpec((1, w, tm), lambda b, j: (b, 0, j)),
                  _const_spec((w, 1)),
                  pl.BlockSpec((1, tm, DIFF_WIDTH), lambda b, j: (b, j, 0)),
                  pl.BlockSpec((1, tm, CONV_WIDTH), lambda b, j: (b, j, 0)),
                  _const_spec(wm.shape), _const_spec(wd.shape), _const_spec(wc.shape)],
        out_specs=pl.BlockSpec((1, tm, D_MODEL), lambda b, j: (b, j, 0)),
        out_shape=jax.ShapeDtypeStruct(xa.shape, F32),
        compiler_params=_cparams(("parallel", "parallel")), name="mix_out",
    )(xa, mod, hT, omT, gm_col, d, cx, wm, wd, wc)


def _swiglu_partial(hb, wg, wu, wd):
    a = jnp.dot(hb, wg, preferred_element_type=F32)
    u = jnp.dot(hb, wu, preferred_element_type=F32)
    t = (a * _sigmoid(a) * u).astype(BF16)
    return jnp.dot(t, wd, preferred_element_type=F32)


def _ffn_kernel(x_ref, mod_ref, g2_ref, wg_ref, wu_ref, wd_ref, o_ref, hb_scr, acc_scr,
                *, seq, tm, ctx_row, nf):
    b = pl.program_id(0)
    j = pl.program_id(1)
    f = pl.program_id(2)
    is_ctx = _is_ctx_rows(j, tm, seq)

    @pl.when(f == 0)
    def _():
        h = _rms_mod(x_ref[0], g2_ref[...], mod_ref, b, ctx_row, is_ctx, 3, 4)
        hb_scr[...] = h.astype(BF16)
        acc_scr[...] = jnp.zeros(acc_scr.shape, F32)

    acc_scr[...] += _swiglu_partial(hb_scr[...], wg_ref[0], wu_ref[0], wd_ref[0])

    @pl.when(f == nf - 1)
    def _():
        g_l, g_c = _mod_rows(mod_ref, b, ctx_row, 5)
        o_ref[0] = x_ref[0] + jnp.where(is_ctx, g_c, g_l) * acc_scr[...]


def _ffn(xa, mod, layer, g2, wg, wu, wd, *, seq):
    bsz, t_all, _ = xa.shape
    tm = TOK_TILE
    nt = t_all // tm
    nf, _, tf = wg.shape
    kern = functools.partial(_ffn_kernel, seq=seq, tm=tm, ctx_row=bsz, nf=nf)
    return pl.pallas_call(
        kern, grid=(bsz, nt, nf),
        in_specs=[pl.BlockSpec((1, tm, D_MODEL), lambda b, j, f: (b, j, 0)),
                  pl.BlockSpec((1, MOD_ROWS, 6 * D_MODEL), lambda b, j, f: (layer, 0, 0)),
                  _const_spec((1, D_MODEL)),
                  pl.BlockSpec((1, D_MODEL, tf), lambda b, j, f: (f, 0, 0)),
                  pl.BlockSpec((1, D_MODEL, tf), lambda b, j, f: (f, 0, 0)),
                  pl.BlockSpec((1, tf, D_MODEL), lambda b, j, f: (f, 0, 0))],
        out_specs=pl.BlockSpec((1, tm, D_MODEL), lambda b, j, f: (b, j, 0)),
        out_shape=jax.ShapeDtypeStruct(xa.shape, F32),
        scratch_shapes=[pltpu.VMEM((tm, D_MODEL), BF16), pltpu.VMEM((tm, D_MODEL), F32)],
        compiler_params=_cparams(("parallel", "parallel", "arbitrary")), name="ffn_swiglu",
    )(xa, mod, g2, wg, wu, wd)


def _moe_kernel(x_ref, mod_ref, g2_ref, wr_ref, br_ref, wg_ref, wu_ref, wd_ref, o_ref,
                hb_scr, acc_scr, comb_scr, slot_scr, slotT_scr, *, seq, tm, ctx_row, ne):
    b = pl.program_id(0)
    j = pl.program_id(1)
    e = pl.program_id(2)
    is_ctx = _is_ctx_rows(j, tm, seq)
    lane = lax.broadcasted_iota(jnp.int32, (tm, LANE), 1)

    @pl.when(e == 0)
    def _():
        h = _rms_mod(x_ref[0], g2_ref[...], mod_ref, b, ctx_row, is_ctx, 3, 4)
        hb_scr[...] = h.astype(BF16)
        acc_scr[...] = jnp.zeros(acc_scr.shape, F32)
        logits = jnp.dot(h, wr_ref[...], preferred_element_type=F32, precision=HIGHEST) + br_ref[...]
        logits = jnp.where(lane < ne, logits, -jnp.inf)
        ex = jnp.exp(logits - jnp.max(logits, axis=-1, keepdims=True))
        probs = ex / jnp.sum(ex, axis=-1, keepdims=True)
        v1 = jnp.max(probs, axis=-1, keepdims=True)
        i1 = jnp.min(jnp.where(probs == v1, lane, LANE), axis=-1, keepdims=True)
        rest = jnp.where(lane == i1, -1.0, probs)
        v2 = jnp.max(rest, axis=-1, keepdims=True)
        i2 = jnp.min(jnp.where(rest == v2, lane, LANE), axis=-1, keepdims=True)
        tot = v1 + v2
        comb_scr[...] = jnp.where(lane == i1, v1 / tot, 0.0) + jnp.where(lane == i2, v2 / tot, 0.0)
        sel = jnp.logical_or(lane == i1, lane == i2)
        ri = lax.broadcasted_iota(jnp.int32, (tm, tm), 0)
        ci = lax.broadcasted_iota(jnp.int32, (tm, tm), 1)
        before = jnp.where(ci < ri, 1.0, 0.0).astype(BF16)
        rank = jnp.dot(before, jnp.where(sel, 1.0, 0.0).astype(BF16), preferred_element_type=F32)
        slot = jnp.where(sel, rank, -1.0)
        slot_scr[...] = slot
        slotT_scr[...] = slot.T

    onlane = lane == e
    cw = jnp.sum(jnp.where(onlane, comb_scr[...], 0.0), axis=-1, keepdims=True)
    slot_c = jnp.max(jnp.where(onlane, slot_scr[...], -1.0), axis=-1, keepdims=True)
    slot_r = slotT_scr[pl.ds(e, 1), :]
    count = (jnp.max(slot_r) + 1.0).astype(jnp.int32)
    sb = MOE_SUB
    row_i = lax.broadcasted_iota(jnp.int32, (sb, 1), 0).astype(F32)
    col_i = lax.broadcasted_iota(jnp.int32, (1, sb), 1).astype(F32)

    def sub_block(i, carry):
        base = (i * sb).astype(F32)
        gather = jnp.where(slot_r == base + row_i, 1.0, 0.0).astype(BF16)
        xs = jnp.dot(gather, hb_scr[...], preferred_element_type=F32).astype(BF16)
        y = _swiglu_partial(xs, wg_ref[0], wu_ref[0], wd_ref[0])
        scatter = jnp.where(slot_c == base + col_i, 1.0, 0.0).astype(BF16)
        acc_scr[...] += cw * jnp.dot(scatter, y.astype(BF16), preferred_element_type=F32)
        return carry

    lax.fori_loop(0, (count + sb - 1) // sb, sub_block, 0)

    @pl.when(e == ne - 1)
    def _():
        g_l, g_c = _mod_rows(mod_ref, b, ctx_row, 5)
        o_ref[0] = x_ref[0] + jnp.where(is_ctx, g_c, g_l) * acc_scr[...]


def _moe(xa, mod, layer, g2, wr, br, wg, wu, wd, *, seq):
    bsz, t_all, _ = xa.shape
    tm = TOK_TILE
    nt = t_all // tm
    ne, _, fe = wg.shape
    kern = functools.partial(_moe_kernel, seq=seq, tm=tm, ctx_row=bsz, ne=ne)
    return pl.pallas_call(
        kern, grid=(bsz, nt, ne),
        in_specs=[pl.BlockSpec((1, tm, D_MODEL), lambda b, j, e: (b, j, 0)),
                  pl.BlockSpec((1, MOD_ROWS, 6 * D_MODEL), lambda b, j, e: (layer, 0, 0)),
                  _const_spec((1, D_MODEL)), _const_spec(wr.shape), _const_spec(br.shape),
                  pl.BlockSpec((1, D_MODEL, fe), lambda b, j, e: (e, 0, 0)),
                  pl.BlockSpec((1, D_MODEL, fe), lambda b, j, e: (e, 0, 0)),
                  pl.BlockSpec((1, fe, D_MODEL), lambda b, j, e: (e, 0, 0))],
        out_specs=pl.BlockSpec((1, tm, D_MODEL), lambda b, j, e: (b, j, 0)),
        out_shape=jax.ShapeDtypeStruct(xa.shape, F32),
        scratch_shapes=[pltpu.VMEM((tm, D_MODEL), BF16), pltpu.VMEM((tm, D_MODEL), F32),
                        pltpu.VMEM((tm, LANE), F32), pltpu.VMEM((tm, LANE), F32),
                        pltpu.VMEM((LANE, tm), F32)],
        compiler_params=_cparams(("parallel", "parallel", "arbitrary")), name="moe_swiglu",
    )(xa, mod, g2, wr, br, wg, wu, wd)


def _final_kernel(x_ref, g_ref, o_ref):
    x = x_ref[0]
    o_ref[0] = x * lax.rsqrt(jnp.mean(x * x, axis=-1, keepdims=True) + EPS) * g_ref[...]


def _final_norm(xa, g, *, seq):
    bsz = xa.shape[0]
    tf = 512
    return pl.pallas_call(
        _final_kernel, grid=(bsz, seq // tf),
        in_specs=[pl.BlockSpec((1, tf, D_MODEL), lambda b, j: (b, j, 0)), _const_spec((1, D_MODEL))],
        out_specs=pl.BlockSpec((1, tf, D_MODEL), lambda b, j: (b, j, 0)),
        out_shape=jax.ShapeDtypeStruct((bsz, seq, D_MODEL), F32),
        compiler_params=_cparams(("parallel", "parallel")), name="final_norm",
    )(xa, g.reshape(1, D_MODEL))


def _rope_tables(seq, t_all):
    pos = np.arange(seq)
    per_axis = DIFF_QK_DIM // 2
    inv = ROPE_BASE ** (-np.arange(0, per_axis, 2, dtype=np.float32) / per_axis)
    inv = jnp.asarray(inv, F32)
    rowp = jnp.asarray(pos // GRID_W, F32)
    colp = jnp.asarray(pos % GRID_W, F32)
    ang = jnp.stack([rowp[:, None] * inv, colp[:, None] * inv], axis=1)
    cos = jnp.concatenate([jnp.cos(ang), jnp.ones((t_all - seq, 2, 16), F32)], axis=0)
    sin = jnp.concatenate([jnp.sin(ang), jnp.zeros((t_all - seq, 2, 16), F32)], axis=0)
    qscale = (DIFF_QK_DIM ** -0.5) * LOG2E
    ropeT = jnp.concatenate([cos.reshape(t_all, 32).T, sin.reshape(t_all, 32).T], axis=0) * qscale
    cos64 = jnp.concatenate([cos[:, 0], cos[:, 0], cos[:, 1], cos[:, 1]], axis=-1)
    sin64 = jnp.concatenate([-sin[:, 0], sin[:, 0], -sin[:, 1], sin[:, 1]], axis=-1)
    ropeR = jnp.concatenate([cos64, cos64, sin64, sin64], axis=-1)
    return ropeT, ropeR


def _swap_perm():
    idx = np.arange(2 * N_DIFF_HEADS * DIFF_QK_DIM)
    return np.where((idx % 32) < 16, idx + 16, idx - 16)


def _prep_inproj_weights(w_in_l, b_gates_l):
    offs = np.cumsum((0,) + IN_SPLITS)
    col = lambda i: w_in_l[:, offs[i]:offs[i + 1]]
    mq, mk, mv, mo, gt, aq, ak, av, cv = (col(i) for i in range(9))
    nh = N_MLSTM_HEADS
    wTm = jnp.concatenate([mq, mv, mo], axis=1).T.astype(BF16)
    wTg = gt.T.astype(BF16)
    bgT = b_gates_l.reshape(4 * nh, 1).astype(F32)
    wkm = (mk * (MLSTM_HEAD_DIM ** -0.5)).astype(BF16)
    zpad = jnp.zeros((D_MODEL, LANE - 2 * nh), F32)
    wg = jnp.concatenate([gt[:, 0:2 * nh], zpad, gt[:, 2 * nh:4 * nh], zpad], axis=1).astype(BF16)
    bpad = jnp.zeros((LANE - 2 * nh,), F32)
    bg = jnp.concatenate([b_gates_l[0:2 * nh], bpad, b_gates_l[2 * nh:4 * nh], bpad]).reshape(1, 2 * LANE)
    wTaq = aq.T.astype(BF16)
    wak = jnp.concatenate([ak, ak[:, _swap_perm()]], axis=1).astype(BF16)
    wTav = av.T.astype(BF16)
    wcv = cv.astype(BF16)
    return (wTm, wTg, bgT, wkm, wg, bg, wTaq, wak, wTav, wcv)


def kernel(x, c, ctx, c_ctx, w_mod, b_mod, g_norm1, w_in, b_gates, g_mlstm, lambda_q1, lambda_k1,
           lambda_q2, lambda_k2, g_subln, w_dw, b_dw, g_conv_ln, b_conv_ln, w_out, g_norm2,
           w_ffn_gate, w_ffn_up, w_ffn_down, w_router, b_router, w_exp_gate, w_exp_up, w_exp_down,
           g_final):
    bsz, seq, _ = x.shape
    nctx = ctx.shape[1]
    t_all = seq + nctx
    depth = w_mod.shape[0]
    assert nctx == CTX_LEN == MLSTM_CHUNK and bsz + 1 <= MOD_ROWS
    assert t_all % TOK_TILE == 0 and seq % Q_TILE == 0 and seq % CONV_TILE == 0 and seq % GRID_W == 0

    xa = jnp.concatenate([x, ctx], axis=1)
    cond = jnp.concatenate([c, c_ctx[None, :], jnp.zeros((MOD_ROWS - bsz - 1, D_MODEL), F32)], axis=0)
    mod = _mod_table(cond, w_mod, b_mod)
    ropeT, ropeR = _rope_tables(seq, t_all)
    nkc = t_all // TOK_TILE
    lat_rows = seq

    for l in range(depth):
        wts = _prep_inproj_weights(w_in[l], b_gates[l])
        (qmT, km, vmT, omT, gT, g, qaT, ka, vaT, u) = _inproj(
            xa, mod, l, g_norm1[l].reshape(1, D_MODEL), ropeT, ropeR, wts, seq=seq)

        hT = _mlstm(qmT, km, vmT, gT, g, seq=seq)

        lam_init = 0.8 - 0.6 * math.exp(-0.3 * l)
        lamv = jnp.zeros((8, LANE), F32).at[0:4, 0:DIFF_QK_DIM].set(
            jnp.stack([lambda_q1[l], lambda_k1[l], lambda_q2[l], lambda_k2[l]]).astype(F32))
        gs_col = g_subln[l].reshape(DIFF_V_DIM, 1).astype(F32)
        d_lat = _attention(lamv, gs_col, qaT, ka, vaT, lat_rows, q_tile=Q_TILE, q_blk0=0,
                           n_q=seq // Q_TILE, k_rows=t_all, k_blk0=0, v_chunks=nkc, v_chunk0=0,
                           v_cols=TOK_TILE, v_blk0=0, lam_init=lam_init)
        d_ctx = _attention(lamv, gs_col, qaT, ka, vaT, nctx, q_tile=nctx, q_blk0=seq // nctx, n_q=1,
                           k_rows=nctx, k_blk0=seq // nctx, v_chunks=1, v_chunk0=nkc - 1,
                           v_cols=nctx, v_blk0=TOK_TILE // nctx - 1, lam_init=lam_init)
        d = jnp.concatenate([d_lat, d_ctx], axis=1)

        cx = _conv(u, w_dw[l], b_dw[l], g_conv_ln[l], b_conv_ln[l], seq=seq)

        wo = w_out[l].astype(BF16)
        xa = _mixout(xa, mod, l, hT, omT, g_mlstm[l].reshape(MLSTM_WIDTH, 1).astype(F32), d, cx,
                     wo[0:MLSTM_WIDTH], wo[MLSTM_WIDTH:MLSTM_WIDTH + DIFF_WIDTH],
                     wo[MLSTM_WIDTH + DIFF_WIDTH:], seq=seq)

        jj = l // 2
        g2 = g_norm2[l].reshape(1, D_MODEL)
        if l % 2 == 0:
            nf = 2
            tf = D_FF // nf
            split_cols = lambda w_: w_.astype(BF16).reshape(D_MODEL, nf, tf).transpose(1, 0, 2)
            xa = _ffn(xa, mod, l, g2, split_cols(w_ffn_gate[jj]), split_cols(w_ffn_up[jj]),
                      w_ffn_down[jj].astype(BF16).reshape(nf, tf, D_MODEL), seq=seq)
        else:
            wr = jnp.concatenate([w_router[jj], jnp.zeros((D_MODEL, LANE - N_EXPERTS), F32)], axis=1)
            br = jnp.concatenate([b_router[jj], jnp.zeros((LANE - N_EXPERTS,), F32)]).reshape(1, LANE)
            xa = _moe(xa, mod, l, g2, wr, br, w_exp_gate[jj].astype(BF16), w_exp_up[jj].astype(BF16),
                      w_exp_down[jj].astype(BF16), seq=seq)

    return _final_norm(xa, g_final, seq=seq)
```

```python
import functools
import math

import jax
import jax.numpy as jnp
import numpy as np
from jax import lax
from jax.experimental import pallas as pl
from jax.experimental.pallas import tpu as pltpu

F32 = jnp.float32
BF16 = jnp.bfloat16
HIGHEST = lax.Precision.HIGHEST

D_MODEL = 1024
DEPTH = 4
GRID_W = 64
CTX_LEN = 256
N_MLSTM_HEADS = 4
MLSTM_HEAD_DIM = 64
MLSTM_WIDTH = N_MLSTM_HEADS * MLSTM_HEAD_DIM
N_DIFF_HEADS = 4
DIFF_QK_DIM = 64
DIFF_V_DIM = 2 * DIFF_QK_DIM
DIFF_WIDTH = N_DIFF_HEADS * DIFF_V_DIM
ROPE_BASE = 10000.0
CONV_WIDTH = 256
CONV_KERNEL = 31
IN_SPLITS = (MLSTM_WIDTH, MLSTM_WIDTH, MLSTM_WIDTH, MLSTM_WIDTH, 4 * N_MLSTM_HEADS,
             2 * N_DIFF_HEADS * DIFF_QK_DIM, 2 * N_DIFF_HEADS * DIFF_QK_DIM, DIFF_WIDTH,
             2 * CONV_WIDTH)
D_FF = 2816
N_EXPERTS = 8
D_FF_EXPERT = 1408
EPS = 1e-6
M_INIT = -1e30
NEG_BIG = -1e30
LOG2E = 1.4426950408889634

LANE = 128
V7X_VMEM_LIMIT = 56 * 1024 * 1024
TOK_TILE = 768
MLSTM_CHUNK = 256
CONV_TILE = 256
CONV_HALO = 16
Q_TILE = 512
MOD_ROWS = 8
VAUG = 16
MOE_SUB = 224
ATT_KEYS = 256
ATT_COLS = 256
ATT_AHEAD = 4


def _cparams(sem):
    return pltpu.CompilerParams(dimension_semantics=sem, vmem_limit_bytes=V7X_VMEM_LIMIT)


def _sigmoid(v):
    return 1.0 / (1.0 + jnp.exp(-v))


def _log_sigmoid(v):
    return jnp.minimum(v, 0.0) - jnp.log(1.0 + jnp.exp(-jnp.abs(v)))


def _mod_rows(mod_ref, b, ctx_row, k):
    lat = mod_ref[0, pl.ds(b, 1), k * D_MODEL:(k + 1) * D_MODEL]
    ctx = mod_ref[0, ctx_row:ctx_row + 1, k * D_MODEL:(k + 1) * D_MODEL]
    return lat, ctx


def _is_ctx_rows(j, tm, seq):
    rows = j * tm + lax.broadcasted_iota(jnp.int32, (tm, 1), 0)
    return rows >= seq


def _rms_mod(x, g, mod_ref, b, ctx_row, is_ctx, k_shift, k_scale):
    y = x * lax.rsqrt(jnp.mean(x * x, axis=-1, keepdims=True) + EPS) * g
    sh_l, sh_c = _mod_rows(mod_ref, b, ctx_row, k_shift)
    sc_l, sc_c = _mod_rows(mod_ref, b, ctx_row, k_scale)
    shift = jnp.where(is_ctx, sh_c, sh_l)
    scale = jnp.where(is_ctx, sc_c, sc_l)
    return y * (1.0 + scale) + shift


def _mod_kernel(cond_ref, w_ref, b_ref, o_ref):
    c = cond_ref[...]
    s = c * _sigmoid(c)
    o_ref[0] = jnp.dot(s, w_ref[0], preferred_element_type=F32, precision=HIGHEST) + b_ref[0]


def _mod_table(cond, w_mod, b_mod):
    depth = w_mod.shape[0]
    n = w_mod.shape[2] // D_MODEL
    return pl.pallas_call(
        _mod_kernel,
        grid=(depth, n),
        in_specs=[pl.BlockSpec((MOD_ROWS, D_MODEL), lambda l, c: (0, 0)),
                  pl.BlockSpec((1, D_MODEL, D_MODEL), lambda l, c: (l, 0, c)),
                  pl.BlockSpec((1, 1, D_MODEL), lambda l, c: (l, 0, c))],
        out_specs=pl.BlockSpec((1, MOD_ROWS, D_MODEL), lambda l, c: (l, 0, c)),
        out_shape=jax.ShapeDtypeStruct((depth, MOD_ROWS, n * D_MODEL), F32),
        compiler_params=_cparams(("parallel", "parallel")),
        name="mod_table",
    )(cond, w_mod, b_mod.reshape(depth, 1, n * D_MODEL))


def _inproj_kernel(x_ref, mod_ref, g1_ref, ropeT_ref, ropeR_ref,
                   wTm_ref, wTg_ref, bgT_ref, wkm_ref, wg_ref, bg_ref,
                   wTaq_ref, wak_ref, wTav_ref, wcv_ref,
                   qmT_o, km_o, vmT_o, omT_o, gT_o, g_o, qaT_o, ka_o, vaT_o, u_o,
                   *, seq, tm, ctx_row):
    b = pl.program_id(0)
    j = pl.program_id(1)
    is_ctx = _is_ctx_rows(j, tm, seq)
    h = _rms_mod(x_ref[0], g1_ref[...], mod_ref, b, ctx_row, is_ctx, 0, 1)
    hb = h.astype(BF16)
    hT = h.T.astype(BF16)

    mT = jnp.dot(wTm_ref[...], hT, preferred_element_type=F32)
    w = MLSTM_WIDTH
    qmT_o[0] = mT[0:w].astype(BF16)
    vmT_o[0] = mT[w:2 * w].astype(BF16)
    omT_o[0] = mT[2 * w:3 * w].astype(BF16)
    gT = jnp.dot(wTg_ref[...], hT, preferred_element_type=F32) + bgT_ref[...]
    rowi = lax.broadcasted_iota(jnp.int32, gT.shape, 0)
    gT_o[0] = jnp.where((rowi % 8) >= 4, _log_sigmoid(gT), gT)
    km_o[0] = jnp.dot(hb, wkm_ref[...], preferred_element_type=F32).astype(BF16)
    g = jnp.dot(hb, wg_ref[...], preferred_element_type=F32) + bg_ref[...]
    lanei = lax.broadcasted_iota(jnp.int32, g.shape, 1) % LANE
    g_o[0] = jnp.where((lanei >= 4) & (lanei < 8), _log_sigmoid(g), g)

    qT = jnp.dot(wTaq_ref[...], hT, preferred_element_type=F32)
    for grp in range(2 * N_DIFF_HEADS * 2):
        ax = grp % 2
        cos = ropeT_ref[ax * 16:(ax + 1) * 16, :]
        sin = ropeT_ref[32 + ax * 16:32 + (ax + 1) * 16, :]
        x1 = qT[grp * 32:grp * 32 + 16]
        x2 = qT[grp * 32 + 16:grp * 32 + 32]
        qaT_o[0, grp * 32:grp * 32 + 16, :] = (x1 * cos - x2 * sin).astype(BF16)
        qaT_o[0, grp * 32 + 16:grp * 32 + 32, :] = (x2 * cos + x1 * sin).astype(BF16)
    kk = jnp.dot(hb, wak_ref[...], preferred_element_type=F32)
    cosr = ropeR_ref[:, 0:LANE]
    sinr = ropeR_ref[:, LANE:2 * LANE]
    nk = 2 * N_DIFF_HEADS * DIFF_QK_DIM
    for sl in range(nk // LANE):
        k0 = kk[:, sl * LANE:(sl + 1) * LANE]
        k1 = kk[:, nk + sl * LANE:nk + (sl + 1) * LANE]
        ka_o[0, :, sl * LANE:(sl + 1) * LANE] = (k0 * cosr + k1 * sinr).astype(BF16)
    vT = jnp.dot(wTav_ref[...], hT, preferred_element_type=F32)
    for hh in range(N_DIFF_HEADS):
        vaT_o[0, 0, hh, 0:DIFF_V_DIM, :] = vT[hh * DIFF_V_DIM:(hh + 1) * DIFF_V_DIM].astype(BF16)
        vaT_o[0, 0, hh, DIFF_V_DIM:DIFF_V_DIM + VAUG, :] = jnp.ones((VAUG, tm), BF16)
    cv = jnp.dot(hb, wcv_ref[...], preferred_element_type=F32)
    u_o[0] = (cv[:, :CONV_WIDTH] * _sigmoid(cv[:, CONV_WIDTH:])).astype(BF16)


def _const_spec(shape):
    nd = len(shape)
    return pl.BlockSpec(shape, lambda *_: (0,) * nd)


def _inproj(xa, mod, layer, g1, ropeT, ropeR, wts, *, seq):
    bsz, t_all, _ = xa.shape
    tm = TOK_TILE
    nt = t_all // tm
    kern = functools.partial(_inproj_kernel, seq=seq, tm=tm, ctx_row=bsz)
    w = MLSTM_WIDTH
    out_shapes = (
        jax.ShapeDtypeStruct((bsz, w, t_all), BF16),
        jax.ShapeDtypeStruct((bsz, t_all, w), BF16),
        jax.ShapeDtypeStruct((bsz, w, t_all), BF16),
        jax.ShapeDtypeStruct((bsz, w, t_all), BF16),
        jax.ShapeDtypeStruct((bsz, 16, t_all), F32),
        jax.ShapeDtypeStruct((bsz, t_all, 2 * LANE), F32),
        jax.ShapeDtypeStruct((bsz, DIFF_WIDTH, t_all), BF16),
        jax.ShapeDtypeStruct((bsz, t_all, DIFF_WIDTH), BF16),
        jax.ShapeDtypeStruct((bsz, nt, N_DIFF_HEADS, DIFF_V_DIM + VAUG, tm), BF16),
        jax.ShapeDtypeStruct((bsz, t_all, CONV_WIDTH), BF16),
    )
    fm = lambda rows: pl.BlockSpec((1, rows, tm), lambda b, j: (b, 0, j))
    tk = lambda cols: pl.BlockSpec((1, tm, cols), lambda b, j: (b, j, 0))
    out_specs = (fm(w), tk(w), fm(w), fm(w), fm(16), tk(2 * LANE), fm(DIFF_WIDTH), tk(DIFF_WIDTH),
                 pl.BlockSpec((1, 1, N_DIFF_HEADS, DIFF_V_DIM + VAUG, tm), lambda b, j: (b, j, 0, 0, 0)),
                 tk(CONV_WIDTH))
    in_specs = [
        pl.BlockSpec((1, tm, D_MODEL), lambda b, j: (b, j, 0)),
        pl.BlockSpec((1, MOD_ROWS, 6 * D_MODEL), lambda b, j: (layer, 0, 0)),
        _const_spec((1, D_MODEL)),
        pl.BlockSpec((64, tm), lambda b, j: (0, j)),
        pl.BlockSpec((tm, 2 * LANE), lambda b, j: (j, 0)),
    ] + [_const_spec(a.shape) for a in wts]
    return pl.pallas_call(
        kern, grid=(bsz, nt), in_specs=in_specs, out_specs=out_specs, out_shape=out_shapes,
        compiler_params=_cparams(("parallel", "parallel")), name="inproj",
    )(xa, mod, g1, ropeT, ropeR, *wts)


def _mlstm_kernel(qf_ref, kf_ref, vf_ref, gTf_ref, gf_ref, qb_ref, kb_ref, vb_ref, gTb_ref, gb_ref,
                  hf_o, hb_o, c_scr, m_scr, *, chunk):
    step = pl.program_id(1)
    L = chunk
    hd = MLSTM_HEAD_DIM
    nh = N_MLSTM_HEADS

    @pl.when(step == 0)
    def _():
        c_scr[...] = jnp.zeros(c_scr.shape, F32)
        m_scr[...] = jnp.full(m_scr.shape, M_INIT, F32)

    si = lax.broadcasted_iota(jnp.int32, (L, L), 0)
    ti = lax.broadcasted_iota(jnp.int32, (L, L), 1)
    ones_rows = jnp.ones((VAUG, L), BF16)
    zero64 = jnp.zeros((hd, L), BF16)

    chains = []
    for d, (qT_ref, k_ref, vT_ref, gT_ref, g_ref, h_o) in enumerate(
            ((qf_ref, kf_ref, vf_ref, gTf_ref, gf_ref, hf_o), (qb_ref, kb_ref, vb_ref, gTb_ref, gb_ref, hb_o))):
        causal = (si <= ti) if d == 0 else (si >= ti)
        tri = jnp.where(causal, 1.0, 0.0).astype(F32)
        gT = gT_ref[0]
        gc = g_ref[0]
        b_rows = jnp.dot(gT, tri, preferred_element_type=F32, precision=HIGHEST)
        b_cols = jnp.dot(tri.T, gc, preferred_element_type=F32, precision=HIGHEST)
        totals = jnp.sum(gT, axis=1, keepdims=True)
        for hh in range(nh):
            pair, half = hh // 2, hh % 2
            q_h = qT_ref[0, hh * hd:(hh + 1) * hd, :]
            q_msk = jnp.concatenate([q_h, zero64] if half == 0 else [zero64, q_h], axis=0)
            k_pair = k_ref[0, :, pair * 2 * hd:(pair + 1) * 2 * hd]
            v_aug = jnp.concatenate([vT_ref[0, hh * hd:(hh + 1) * hd, :], ones_rows], axis=0)
            idx = d * nh + hh
            c_st = c_scr[idx]
            sT = jnp.dot(k_pair, q_msk, preferred_element_type=F32)
            cq = jnp.dot(c_st.astype(BF16), q_msk, preferred_element_type=F32)
            chains.append(dict(idx=idx, hh=hh, h_o=h_o, causal=causal, k_pair=k_pair, v_aug=v_aug,
                               c_st=c_st, sT=sT, cq=cq, li_row=gT[hh:hh + 1], b_row=b_rows[4 + hh:5 + hh],
                               total=totals[4 + hh:5 + hh],
                               a_col=gc[:, hh:hh + 1] - b_cols[:, 4 + hh:5 + hh]))

    for ch in chains:
        hh, b_row = ch["hh"], ch["b_row"]
        m_st = m_scr[ch["idx"], 0:1, :]
        dmat = jnp.where(ch["causal"], b_row + ch["a_col"], -jnp.inf)
        inter = b_row + m_st
        m_t = jnp.maximum(inter, jnp.max(dmat, axis=0, keepdims=True))
        wT = jnp.exp(dmat - m_t) * ch["sT"]
        e_inter = jnp.exp(inter - m_t)
        intra = jnp.dot(ch["v_aug"], wT.astype(BF16), preferred_element_type=F32)
        cq = ch["cq"]
        num = e_inter * cq[0:hd] + intra[0:hd]
        den = e_inter * cq[hd:hd + 1] + jnp.sum(wT, axis=0, keepdims=True)
        ch["h_o"][0, hh * hd:(hh + 1) * hd, :] = num / jnp.maximum(jnp.abs(den), jnp.exp(-m_t))

        total = ch["total"]
        g_row = total - b_row + ch["li_row"]
        m_prev = m_st[:, 0:1]
        m_new = jnp.maximum(total + m_prev, jnp.max(g_row, axis=1, keepdims=True))
        e_old = jnp.exp(total + m_prev - m_new)
        e_g = jnp.exp(g_row - m_new)
        upd = jnp.dot((ch["v_aug"].astype(F32) * e_g).astype(BF16), ch["k_pair"],
                      preferred_element_type=F32)
        c_scr[ch["idx"]] = e_old * ch["c_st"] + upd
        m_scr[ch["idx"]] = jnp.broadcast_to(m_new, m_scr.shape[1:])


def _mlstm(qmT, km, vmT, gT, g, *, seq):
    bsz, w, t_all = qmT.shape
    L = MLSTM_CHUNK
    nlat = seq // L
    nch = t_all // L
    nctx = nch - nlat
    fwd = lambda i: jnp.where(i < nctx, nlat + i, i - nctx)
    bwd = lambda i: nch - 1 - i

    def specs(chunk_of, d):
        return [pl.BlockSpec((1, w, L), lambda b, i: (b, 0, chunk_of(i))),
                pl.BlockSpec((1, L, w), lambda b, i: (b, chunk_of(i), 0)),
                pl.BlockSpec((1, w, L), lambda b, i: (b, 0, chunk_of(i))),
                pl.BlockSpec((1, 8, L), lambda b, i: (b, d, chunk_of(i))),
                pl.BlockSpec((1, L, LANE), lambda b, i: (b, chunk_of(i), d))]

    kern = functools.partial(_mlstm_kernel, chunk=L)
    out = jax.ShapeDtypeStruct((bsz, w, t_all), F32)
    return pl.pallas_call(
        kern, grid=(bsz, nch),
        in_specs=specs(fwd, 0) + specs(bwd, 1),
        out_specs=(pl.BlockSpec((1, w, L), lambda b, i: (b, 0, fwd(i))),
                   pl.BlockSpec((1, w, L), lambda b, i: (b, 0, bwd(i)))),
        out_shape=(out, out),
        scratch_shapes=[pltpu.VMEM((2 * N_MLSTM_HEADS, MLSTM_HEAD_DIM + VAUG, LANE), F32),
                        pltpu.VMEM((2 * N_MLSTM_HEADS, 8, L), F32)],
        compiler_params=_cparams(("parallel", "arbitrary")), name="mlstm_scan",
    )(qmT, km, vmT, gT, g, qmT, km, vmT, gT, g)


def _attn_kernel(lam_ref, gs_ref, qT_ref, k_ref, vT_ref, o_ref, *, nch, tk, lam_init):
    tq = qT_ref.shape[2]
    qT = qT_ref[0]
    z = jnp.zeros((DIFF_QK_DIM, tq), BF16)
    rhs = jnp.concatenate([jnp.concatenate([qT[:DIFF_QK_DIM], z], axis=0),
                           jnp.concatenate([z, qT[DIFF_QK_DIM:]], axis=0)], axis=1)
    ncb = 2 * tq // ATT_COLS
    ms = [jnp.full((1, ATT_COLS), NEG_BIG, F32) for _ in range(ncb)]
    accs = [jnp.zeros((DIFF_V_DIM + VAUG, ATT_COLS), F32) for _ in range(ncb)]
    sub = tk // ATT_KEYS
    units = [(c, cb) for c in range(nch * sub) for cb in range(ncb)]

    def scores(c, cb):
        return jnp.dot(k_ref[0, c * ATT_KEYS:(c + 1) * ATT_KEYS, :],
                       rhs[:, cb * ATT_COLS:(cb + 1) * ATT_COLS], preferred_element_type=F32)

    pending = [scores(*u) for u in units[:ATT_AHEAD]]
    for i, (c, cb) in enumerate(units):
        sT = pending.pop(0)
        if i + ATT_AHEAD < len(units):
            pending.append(scores(*units[i + ATT_AHEAD]))
        vT = vT_ref[0, c // sub, 0, :, (c % sub) * ATT_KEYS:(c % sub + 1) * ATT_KEYS]
        m_new = jnp.maximum(ms[cb], jnp.max(sT, axis=0, keepdims=True))
        p = jnp.exp2((sT - m_new).astype(BF16))
        alpha = jnp.exp2(ms[cb] - m_new)
        accs[cb] = alpha * accs[cb] + jnp.dot(vT, p, preferred_element_type=F32)
        ms[cb] = m_new
    acc = jnp.concatenate(accs, axis=1)
    l = acc[DIFF_V_DIM:DIFF_V_DIM + 1]
    acc = acc[0:DIFF_V_DIM]

    lv = lam_ref[...]
    lam = (jnp.exp(jnp.sum(lv[0:1] * lv[1:2], axis=1, keepdims=True))
           - jnp.exp(jnp.sum(lv[2:3] * lv[3:4], axis=1, keepdims=True)) + lam_init)
    oT = acc[:, :tq] / l[:, :tq] - lam * (acc[:, tq:] / l[:, tq:])
    oT = oT * lax.rsqrt(jnp.mean(oT * oT, axis=0, keepdims=True) + EPS) * gs_ref[...] * (1.0 - lam_init)
    o_ref[0] = oT.T.astype(BF16)


def _attention(lamv, gs_col, qaT, ka, vaT, out_rows, *, q_tile, q_blk0, n_q, k_rows, k_blk0, v_chunks,
               v_chunk0, v_cols, v_blk0, lam_init):
    bsz = qaT.shape[0]
    kern = functools.partial(_attn_kernel, nch=v_chunks, tk=v_cols, lam_init=lam_init)
    return pl.pallas_call(
        kern, grid=(bsz, N_DIFF_HEADS, n_q),
        in_specs=[_const_spec(lamv.shape), _const_spec((DIFF_V_DIM, 1)),
                  pl.BlockSpec((1, DIFF_V_DIM, q_tile), lambda b, h, i: (b, h, q_blk0 + i)),
                  pl.BlockSpec((1, k_rows, DIFF_V_DIM), lambda b, h, i: (b, k_blk0, h)),
                  pl.BlockSpec((1, v_chunks, 1, DIFF_V_DIM + VAUG, v_cols),
                               lambda b, h, i: (b, v_chunk0, h, 0, v_blk0))],
        out_specs=pl.BlockSpec((1, q_tile, DIFF_V_DIM), lambda b, h, i: (b, i, h)),
        out_shape=jax.ShapeDtypeStruct((bsz, out_rows, DIFF_WIDTH), BF16),
        compiler_params=_cparams(("parallel", "parallel", "arbitrary")), name="diff_attn",
    )(lamv, gs_col, qaT, ka, vaT)


def _conv_kernel(l_ref, c_ref, r_ref, w_ref, b_ref, g_ref, bb_ref, o_ref, buf, *, seq, tc, nt):
    j = pl.program_id(1)
    start = j * tc
    lvalid = jnp.logical_and(j > 0, start != seq)
    rvalid = jnp.logical_and(j < nt - 1, start + tc != seq)
    hl = CONV_HALO
    buf[0:hl, :] = jnp.where(lvalid, l_ref[0].astype(F32), 0.0)
    buf[hl:hl + tc, :] = c_ref[0].astype(F32)
    buf[hl + tc:2 * hl + tc, :] = jnp.where(rvalid, r_ref[0].astype(F32), 0.0)
    pad = CONV_KERNEL // 2
    sub = 64
    for r0 in range(0, tc, sub):
        acc = jnp.zeros((sub, CONV_WIDTH), F32)
        for tap in range(CONV_KERNEL):
            acc = acc + w_ref[tap:tap + 1, :] * buf[pl.ds(hl - pad + r0 + tap, sub), :]
        y = acc + b_ref[...]
        mu = jnp.mean(y, axis=-1, keepdims=True)
        var = jnp.mean(jnp.square(y - mu), axis=-1, keepdims=True)
        z = (y - mu) * lax.rsqrt(var + EPS) * g_ref[...] + bb_ref[...]
        o_ref[0, r0:r0 + sub, :] = (z * _sigmoid(z)).astype(BF16)


def _conv(u, w_dw, b_dw, g_ln, b_ln, *, seq):
    bsz, t_all, cw = u.shape
    tc = CONV_TILE
    nt = t_all // tc
    r = tc // CONV_HALO
    nhalo = t_all // CONV_HALO
    kern = functools.partial(_conv_kernel, seq=seq, tc=tc, nt=nt)
    row = lambda a: a.reshape(1, cw)
    return pl.pallas_call(
        kern, grid=(bsz, nt),
        in_specs=[pl.BlockSpec((1, CONV_HALO, cw), lambda b, j: (b, jnp.maximum(j * r - 1, 0), 0)),
                  pl.BlockSpec((1, tc, cw), lambda b, j: (b, j, 0)),
                  pl.BlockSpec((1, CONV_HALO, cw), lambda b, j: (b, jnp.minimum((j + 1) * r, nhalo - 1), 0)),
                  _const_spec((CONV_KERNEL, cw)), _const_spec((1, cw)), _const_spec((1, cw)),
                  _const_spec((1, cw))],
        out_specs=pl.BlockSpec((1, tc, cw), lambda b, j: (b, j, 0)),
        out_shape=jax.ShapeDtypeStruct((bsz, t_all, cw), BF16),
        scratch_shapes=[pltpu.VMEM((tc + 2 * CONV_HALO, cw), F32)],
        compiler_params=_cparams(("parallel", "parallel")), name="conv_mixer",
    )(u, u, u, w_dw, row(b_dw), row(g_ln), row(b_ln))


def _mixout_kernel(x_ref, mod_ref, hf_ref, hb_ref, omT_ref, gm_ref, d_ref, c_ref, wm_ref, wd_ref, wc_ref, o_ref,
                   *, seq, tm, ctx_row):
    b = pl.program_id(0)
    j = pl.program_id(1)
    is_ctx = _is_ctx_rows(j, tm, seq)
    hT = hf_ref[0] + hb_ref[0]
    h4 = hT.reshape(N_MLSTM_HEADS, MLSTM_HEAD_DIM, tm)
    mu = jnp.mean(h4, axis=1, keepdims=True)
    var = jnp.mean(jnp.square(h4 - mu), axis=1, keepdims=True)
    hn = ((h4 - mu) * lax.rsqrt(var + EPS)).reshape(MLSTM_WIDTH, tm)
    mT = _sigmoid(omT_ref[0].astype(F32)) * hn * gm_ref[...]
    m = mT.T.astype(BF16)
    y = (jnp.dot(m, wm_ref[...], preferred_element_type=F32)
         + jnp.dot(d_ref[0], wd_ref[...], preferred_element_type=F32)
         + jnp.dot(c_ref[0], wc_ref[...], preferred_element_type=F32))
    g_l, g_c = _mod_rows(mod_ref, b, ctx_row, 2)
    o_ref[0] = x_ref[0] + jnp.where(is_ctx, g_c, g_l) * y


def _mixout(xa, mod, layer, hTf, hTb, omT, gm_col, d, cx, wm, wd, wc, *, seq):
    bsz, t_all, _ = xa.shape
    tm = TOK_TILE
    nt = t_all // tm
    kern = functools.partial(_mixout_kernel, seq=seq, tm=tm, ctx_row=bsz)
    w = MLSTM_WIDTH
    return pl.pallas_call(
        kern, grid=(bsz, nt),
        in_specs=[pl.BlockSpec((1, tm, D_MODEL), lambda b, j: (b, j, 0)),
                  pl.BlockSpec((1, MOD_ROWS, 6 * D_MODEL), lambda b, j: (layer, 0, 0)),
                  pl.BlockSpec((1, w, tm), lambda b, j: (b, 0, j)),
                  pl.BlockSpec((1, w, tm), lambda b, j: (b, 0, j)),
                  pl.BlockSpec((1, w, tm), lambda b, j: (b, 0, j)),
                  _const_spec((w, 1)),
                  pl.BlockSpec((1, tm, DIFF_WIDTH), lambda b, j: (b, j, 0)),
                  pl.BlockSpec((1, tm, CONV_WIDTH), lambda b, j: (b, j, 0)),
                  _const_spec(wm.shape), _const_spec(wd.shape), _const_spec(wc.shape)],
        out_specs=pl.BlockSpec((1, tm, D_MODEL), lambda b, j: (b, j, 0)),
        out_shape=jax.ShapeDtypeStruct(xa.shape, F32),
        compiler_params=_cparams(("parallel", "parallel")), name="mix_out",
    )(xa, mod, hTf, hTb, omT, gm_col, d, cx, wm, wd, wc)


def _swiglu_partial(hb, wg, wu, wd):
    a = jnp.dot(hb, wg, preferred_element_type=F32)
    u = jnp.dot(hb, wu, preferred_element_type=F32)
    t = (a * _sigmoid(a) * u).astype(BF16)
    return jnp.dot(t, wd, preferred_element_type=F32)


def _ffn_kernel(x_ref, mod_ref, g2_ref, wg_ref, wu_ref, wd_ref, o_ref, hb_scr, acc_scr,
                *, seq, tm, ctx_row, nf):
    b = pl.program_id(0)
    j = pl.program_id(1)
    f = pl.program_id(2)
    is_ctx = _is_ctx_rows(j, tm, seq)

    @pl.when(f == 0)
    def _():
        h = _rms_mod(x_ref[0], g2_ref[...], mod_ref, b, ctx_row, is_ctx, 3, 4)
        hb_scr[...] = h.astype(BF16)
        acc_scr[...] = jnp.zeros(acc_scr.shape, F32)

    acc_scr[...] += _swiglu_partial(hb_scr[...], wg_ref[0], wu_ref[0], wd_ref[0])

    @pl.when(f == nf - 1)
    def _():
        g_l, g_c = _mod_rows(mod_ref, b, ctx_row, 5)
        o_ref[0] = x_ref[0] + jnp.where(is_ctx, g_c, g_l) * acc_scr[...]


def _ffn(xa, mod, layer, g2, wg, wu, wd, *, seq):
    bsz, t_all, _ = xa.shape
    tm = TOK_TILE
    nt = t_all // tm
    nf, _, tf = wg.shape
    kern = functools.partial(_ffn_kernel, seq=seq, tm=tm, ctx_row=bsz, nf=nf)
    return pl.pallas_call(
        kern, grid=(bsz, nt, nf),
        in_specs=[pl.BlockSpec((1, tm, D_MODEL), lambda b, j, f: (b, j, 0)),
                  pl.BlockSpec((1, MOD_ROWS, 6 * D_MODEL), lambda b, j, f: (layer, 0, 0)),
                  _const_spec((1, D_MODEL)),
                  pl.BlockSpec((1, D_MODEL, tf), lambda b, j, f: (f, 0, 0)),
                  pl.BlockSpec((1, D_MODEL, tf), lambda b, j, f: (f, 0, 0)),
                  pl.BlockSpec((1, tf, D_MODEL), lambda b, j, f: (f, 0, 0))],
        out_specs=pl.BlockSpec((1, tm, D_MODEL), lambda b, j, f: (b, j, 0)),
        out_shape=jax.ShapeDtypeStruct(xa.shape, F32),
        scratch_shapes=[pltpu.VMEM((tm, D_MODEL), BF16), pltpu.VMEM((tm, D_MODEL), F32)],
        compiler_params=_cparams(("parallel", "parallel", "arbitrary")), name="ffn_swiglu",
    )(xa, mod, g2, wg, wu, wd)


def _moe_kernel(x_ref, mod_ref, g2_ref, wr_ref, br_ref, wg_ref, wu_ref, wd_ref, o_ref,
                hb_scr, acc_scr, comb_scr, slot_scr, slotT_scr, *, seq, tm, ctx_row, ne):
    b = pl.program_id(0)
    j = pl.program_id(1)
    e = pl.program_id(2)
    is_ctx = _is_ctx_rows(j, tm, seq)
    lane = lax.broadcasted_iota(jnp.int32, (tm, LANE), 1)

    @pl.when(e == 0)
    def _():
        h = _rms_mod(x_ref[0], g2_ref[...], mod_ref, b, ctx_row, is_ctx, 3, 4)
        hb_scr[...] = h.astype(BF16)
        acc_scr[...] = jnp.zeros(acc_scr.shape, F32)
        logits = jnp.dot(h, wr_ref[...], preferred_element_type=F32, precision=HIGHEST) + br_ref[...]
        logits = jnp.where(lane < ne, logits, -jnp.inf)
        ex = jnp.exp(logits - jnp.max(logits, axis=-1, keepdims=True))
        probs = ex / jnp.sum(ex, axis=-1, keepdims=True)
        v1 = jnp.max(probs, axis=-1, keepdims=True)
        i1 = jnp.min(jnp.where(probs == v1, lane, LANE), axis=-1, keepdims=True)
        rest = jnp.where(lane == i1, -1.0, probs)
        v2 = jnp.max(rest, axis=-1, keepdims=True)
        i2 = jnp.min(jnp.where(rest == v2, lane, LANE), axis=-1, keepdims=True)
        tot = v1 + v2
        comb_scr[...] = jnp.where(lane == i1, v1 / tot, 0.0) + jnp.where(lane == i2, v2 / tot, 0.0)
        sel = jnp.logical_or(lane == i1, lane == i2)
        ri = lax.broadcasted_iota(jnp.int32, (tm, tm), 0)
        ci = lax.broadcasted_iota(jnp.int32, (tm, tm), 1)
        before = jnp.where(ci < ri, 1.0, 0.0).astype(BF16)
        rank = jnp.dot(before, jnp.where(sel, 1.0, 0.0).astype(BF16), preferred_element_type=F32)
        slot = jnp.where(sel, rank, -1.0)
        slot_scr[...] = slot
        slotT_scr[...] = slot.T

    onlane = lane == e
    cw = jnp.sum(jnp.where(onlane, comb_scr[...], 0.0), axis=-1, keepdims=True)
    slot_c = jnp.max(jnp.where(onlane, slot_scr[...], -1.0), axis=-1, keepdims=True)
    slot_r = slotT_scr[pl.ds(e, 1), :]
    count = (jnp.max(slot_r) + 1.0).astype(jnp.int32)
    sb = MOE_SUB
    row_i = lax.broadcasted_iota(jnp.int32, (sb, 1), 0).astype(F32)
    col_i = lax.broadcasted_iota(jnp.int32, (1, sb), 1).astype(F32)

    def sub_block(i, carry):
        base = (i * sb).astype(F32)
        gather = jnp.where(slot_r == base + row_i, 1.0, 0.0).astype(BF16)
        xs = jnp.dot(gather, hb_scr[...], preferred_element_type=F32).astype(BF16)
        y = _swiglu_partial(xs, wg_ref[0], wu_ref[0], wd_ref[0])
        scatter = jnp.where(slot_c == base + col_i, 1.0, 0.0).astype(BF16)
        acc_scr[...] += cw * jnp.dot(scatter, y.astype(BF16), preferred_element_type=F32)
        return carry

    lax.fori_loop(0, (count + sb - 1) // sb, sub_block, 0)

    @pl.when(e == ne - 1)
    def _():
        g_l, g_c = _mod_rows(mod_ref, b, ctx_row, 5)
        o_ref[0] = x_ref[0] + jnp.where(is_ctx, g_c, g_l) * acc_scr[...]


def _moe(xa, mod, layer, g2, wr, br, wg, wu, wd, *, seq):
    bsz, t_all, _ = xa.shape
    tm = TOK_TILE
    nt = t_all // tm
    ne, _, fe = wg.shape
    kern = functools.partial(_moe_kernel, seq=seq, tm=tm, ctx_row=bsz, ne=ne)
    return pl.pallas_call(
        kern, grid=(bsz, nt, ne),
        in_specs=[pl.BlockSpec((1, tm, D_MODEL), lambda b, j, e: (b, j, 0)),
                  pl.BlockSpec((1, MOD_ROWS, 6 * D_MODEL), lambda b, j, e: (layer, 0, 0)),
                  _const_spec((1, D_MODEL)), _const_spec(wr.shape), _const_spec(br.shape),
                  pl.BlockSpec((1, D_MODEL, fe), lambda b, j, e: (e, 0, 0)),
                  pl.BlockSpec((1, D_MODEL, fe), lambda b, j, e: (e, 0, 0)),
                  pl.BlockSpec((1, fe, D_MODEL), lambda b, j, e: (e, 0, 0))],
        out_specs=pl.BlockSpec((1, tm, D_MODEL), lambda b, j, e: (b, j, 0)),
        out_shape=jax.ShapeDtypeStruct(xa.shape, F32),
        scratch_shapes=[pltpu.VMEM((tm, D_MODEL), BF16), pltpu.VMEM((tm, D_MODEL), F32),
                        pltpu.VMEM((tm, LANE), F32), pltpu.VMEM((tm, LANE), F32),
                        pltpu.VMEM((LANE, tm), F32)],
        compiler_params=_cparams(("parallel", "parallel", "arbitrary")), name="moe_swiglu",
    )(xa, mod, g2, wr, br, wg, wu, wd)


def _final_kernel(x_ref, g_ref, o_ref):
    x = x_ref[0]
    o_ref[0] = x * lax.rsqrt(jnp.mean(x * x, axis=-1, keepdims=True) + EPS) * g_ref[...]


def _final_norm(xa, g, *, seq):
    bsz = xa.shape[0]
    tf = 512
    return pl.pallas_call(
        _final_kernel, grid=(bsz, seq // tf),
        in_specs=[pl.BlockSpec((1, tf, D_MODEL), lambda b, j: (b, j, 0)), _const_spec((1, D_MODEL))],
        out_specs=pl.BlockSpec((1, tf, D_MODEL), lambda b, j: (b, j, 0)),
        out_shape=jax.ShapeDtypeStruct((bsz, seq, D_MODEL), F32),
        compiler_params=_cparams(("parallel", "parallel")), name="final_norm",
    )(xa, g.reshape(1, D_MODEL))


def _rope_tables(seq, t_all):
    pos = np.arange(seq)
    per_axis = DIFF_QK_DIM // 2
    inv = ROPE_BASE ** (-np.arange(0, per_axis, 2, dtype=np.float32) / per_axis)
    inv = jnp.asarray(inv, F32)
    rowp = jnp.asarray(pos // GRID_W, F32)
    colp = jnp.asarray(pos % GRID_W, F32)
    ang = jnp.stack([rowp[:, None] * inv, colp[:, None] * inv], axis=1)
    cos = jnp.concatenate([jnp.cos(ang), jnp.ones((t_all - seq, 2, 16), F32)], axis=0)
    sin = jnp.concatenate([jnp.sin(ang), jnp.zeros((t_all - seq, 2, 16), F32)], axis=0)
    qscale = (DIFF_QK_DIM ** -0.5) * LOG2E
    ropeT = jnp.concatenate([cos.reshape(t_all, 32).T, sin.reshape(t_all, 32).T], axis=0) * qscale
    cos64 = jnp.concatenate([cos[:, 0], cos[:, 0], cos[:, 1], cos[:, 1]], axis=-1)
    sin64 = jnp.concatenate([-sin[:, 0], sin[:, 0], -sin[:, 1], sin[:, 1]], axis=-1)
    ropeR = jnp.concatenate([cos64, cos64, sin64, sin64], axis=-1)
    return ropeT, ropeR


def _swap_perm():
    idx = np.arange(2 * N_DIFF_HEADS * DIFF_QK_DIM)
    return np.where((idx % 32) < 16, idx + 16, idx - 16)


def _prep_inproj_weights(w_in_l, b_gates_l):
    offs = np.cumsum((0,) + IN_SPLITS)
    col = lambda i: w_in_l[:, offs[i]:offs[i + 1]]
    mq, mk, mv, mo, gt, aq, ak, av, cv = (col(i) for i in range(9))
    nh = N_MLSTM_HEADS
    wTm = jnp.concatenate([mq, mv, mo], axis=1).T.astype(BF16)
    wTg = gt.T.astype(BF16)
    bgT = b_gates_l.reshape(4 * nh, 1).astype(F32)
    wkm = (mk * (MLSTM_HEAD_DIM ** -0.5)).astype(BF16)
    zpad = jnp.zeros((D_MODEL, LANE - 2 * nh), F32)
    wg = jnp.concatenate([gt[:, 0:2 * nh], zpad, gt[:, 2 * nh:4 * nh], zpad], axis=1).astype(BF16)
    bpad = jnp.zeros((LANE - 2 * nh,), F32)
    bg = jnp.concatenate([b_gates_l[0:2 * nh], bpad, b_gates_l[2 * nh:4 * nh], bpad]).reshape(1, 2 * LANE)
    wTaq = aq.T.astype(BF16)
    wak = jnp.concatenate([ak, ak[:, _swap_perm()]], axis=1).astype(BF16)
    wTav = av.T.astype(BF16)
    wcv = cv.astype(BF16)
    return (wTm, wTg, bgT, wkm, wg, bg, wTaq, wak, wTav, wcv)


def kernel(x, c, ctx, c_ctx, w_mod, b_mod, g_norm1, w_in, b_gates, g_mlstm, lambda_q1, lambda_k1,
           lambda_q2, lambda_k2, g_subln, w_dw, b_dw, g_conv_ln, b_conv_ln, w_out, g_norm2,
           w_ffn_gate, w_ffn_up, w_ffn_down, w_router, b_router, w_exp_gate, w_exp_up, w_exp_down,
           g_final):
    bsz, seq, _ = x.shape
    nctx = ctx.shape[1]
    t_all = seq + nctx
    depth = w_mod.shape[0]
    assert nctx == CTX_LEN == MLSTM_CHUNK and bsz + 1 <= MOD_ROWS
    assert t_all % TOK_TILE == 0 and seq % Q_TILE == 0 and seq % CONV_TILE == 0 and seq % GRID_W == 0

    xa = jnp.concatenate([x, ctx], axis=1)
    cond = jnp.concatenate([c, c_ctx[None, :], jnp.zeros((MOD_ROWS - bsz - 1, D_MODEL), F32)], axis=0)
    mod = _mod_table(cond, w_mod, b_mod)
    ropeT, ropeR = _rope_tables(seq, t_all)
    nkc = t_all // TOK_TILE
    lat_rows = seq

    for l in range(depth):
        wts = _prep_inproj_weights(w_in[l], b_gates[l])
        (qmT, km, vmT, omT, gT, g, qaT, ka, vaT, u) = _inproj(
            xa, mod, l, g_norm1[l].reshape(1, D_MODEL), ropeT, ropeR, wts, seq=seq)

        hTf, hTb = _mlstm(qmT, km, vmT, gT, g, seq=seq)

        lam_init = 0.8 - 0.6 * math.exp(-0.3 * l)
        lamv = jnp.zeros((8, LANE), F32).at[0:4, 0:DIFF_QK_DIM].set(
            jnp.stack([lambda_q1[l], lambda_k1[l], lambda_q2[l], lambda_k2[l]]).astype(F32))
        gs_col = g_subln[l].reshape(DIFF_V_DIM, 1).astype(F32)
        d_lat = _attention(lamv, gs_col, qaT, ka, vaT, lat_rows, q_tile=Q_TILE, q_blk0=0,
                           n_q=seq // Q_TILE, k_rows=t_all, k_blk0=0, v_chunks=nkc, v_chunk0=0,
                           v_cols=TOK_TILE, v_blk0=0, lam_init=lam_init)
        d_ctx = _attention(lamv, gs_col, qaT, ka, vaT, nctx, q_tile=nctx, q_blk0=seq // nctx, n_q=1,
                           k_rows=nctx, k_blk0=seq // nctx, v_chunks=1, v_chunk0=nkc - 1,
                           v_cols=nctx, v_blk0=TOK_TILE // nctx - 1, lam_init=lam_init)
        d = jnp.concatenate([d_lat, d_ctx], axis=1)

        cx = _conv(u, w_dw[l], b_dw[l], g_conv_ln[l], b_conv_ln[l], seq=seq)

        wo = w_out[l].astype(BF16)
        xa = _mixout(xa, mod, l, hTf, hTb, omT, g_mlstm[l].reshape(MLSTM_WIDTH, 1).astype(F32), d, cx,
                     wo[0:MLSTM_WIDTH], wo[MLSTM_WIDTH:MLSTM_WIDTH + DIFF_WIDTH],
                     wo[MLSTM_WIDTH + DIFF_WIDTH:], seq=seq)

        jj = l // 2
        g2 = g_norm2[l].reshape(1, D_MODEL)
        if l % 2 == 0:
            nf = 2
            tf = D_FF // nf
            split_cols = lambda w_: w_.astype(BF16).reshape(D_MODEL, nf, tf).transpose(1, 0, 2)
            xa = _ffn(xa, mod, l, g2, split_cols(w_ffn_gate[jj]), split_cols(w_ffn_up[jj]),
                      w_ffn_down[jj].astype(BF16).reshape(nf, tf, D_MODEL), seq=seq)
        else:
            wr = jnp.concatenate([w_router[jj], jnp.zeros((D_MODEL, LANE - N_EXPERTS), F32)], axis=1)
            br = jnp.concatenate([b_router[jj], jnp.zeros((LANE - N_EXPERTS,), F32)]).reshape(1, LANE)
            xa = _moe(xa, mod, l, g2, wr, br, w_exp_gate[jj].astype(BF16), w_exp_up[jj].astype(BF16),
                      w_exp_down[jj].astype(BF16), seq=seq)

    return _final_norm(xa, g_final, seq=seq)
```

```python
import functools
import math

import jax
import jax.numpy as jnp
import numpy as np
from jax import lax
from jax.experimental import pallas as pl
from jax.experimental.pallas import tpu as pltpu

F32 = jnp.float32
BF16 = jnp.bfloat16
HIGHEST = lax.Precision.HIGHEST

D_MODEL = 1024
DEPTH = 4
GRID_W = 64
CTX_LEN = 256
N_MLSTM_HEADS = 4
MLSTM_HEAD_DIM = 64
MLSTM_WIDTH = N_MLSTM_HEADS * MLSTM_HEAD_DIM
N_DIFF_HEADS = 4
DIFF_QK_DIM = 64
DIFF_V_DIM = 2 * DIFF_QK_DIM
DIFF_WIDTH = N_DIFF_HEADS * DIFF_V_DIM
ROPE_BASE = 10000.0
CONV_WIDTH = 256
CONV_KERNEL = 31
IN_SPLITS = (MLSTM_WIDTH, MLSTM_WIDTH, MLSTM_WIDTH, MLSTM_WIDTH, 4 * N_MLSTM_HEADS,
             2 * N_DIFF_HEADS * DIFF_QK_DIM, 2 * N_DIFF_HEADS * DIFF_QK_DIM, DIFF_WIDTH,
             2 * CONV_WIDTH)
D_FF = 2816
N_EXPERTS = 8
D_FF_EXPERT = 1408
EPS = 1e-6
M_INIT = -1e30
NEG_BIG = -1e30
LOG2E = 1.4426950408889634

LANE = 128
V7X_VMEM_LIMIT = 56 * 1024 * 1024
TOK_TILE = 768
MLSTM_CHUNK = 256
CONV_TILE = 256
CONV_HALO = 16
CONV_SHIFTS = 8
Q_TILE = 512
MOD_ROWS = 8
VAUG = 16
MOE_SUB = 224
FFN_SPLIT = 2
MOE_GROUP = 2
ATT_KEYS = 256
ATT_COLS = 256
ATT_AHEAD = 4


def _cparams(sem):
    return pltpu.CompilerParams(dimension_semantics=sem, vmem_limit_bytes=V7X_VMEM_LIMIT)


def _sigmoid(v):
    return 1.0 / (1.0 + jnp.exp(-v))


def _log_sigmoid(v):
    return jnp.minimum(v, 0.0) - jnp.log(1.0 + jnp.exp(-jnp.abs(v)))


def _mod_rows(mod_ref, b, ctx_row, k):
    lat = mod_ref[0, pl.ds(b, 1), k * D_MODEL:(k + 1) * D_MODEL]
    ctx = mod_ref[0, ctx_row:ctx_row + 1, k * D_MODEL:(k + 1) * D_MODEL]
    return lat, ctx


def _is_ctx_rows(j, tm, seq):
    rows = j * tm + lax.broadcasted_iota(jnp.int32, (tm, 1), 0)
    return rows >= seq


def _rms_mod(x, g, mod_ref, b, ctx_row, is_ctx, k_shift, k_scale):
    y = x * lax.rsqrt(jnp.mean(x * x, axis=-1, keepdims=True) + EPS) * g
    sh_l, sh_c = _mod_rows(mod_ref, b, ctx_row, k_shift)
    sc_l, sc_c = _mod_rows(mod_ref, b, ctx_row, k_scale)
    shift = jnp.where(is_ctx, sh_c, sh_l)
    scale = jnp.where(is_ctx, sc_c, sc_l)
    return y * (1.0 + scale) + shift


def _mod_kernel(cond_ref, w_ref, b_ref, o_ref):
    c = cond_ref[...]
    s = c * _sigmoid(c)
    o_ref[0] = jnp.dot(s, w_ref[0], preferred_element_type=F32, precision=HIGHEST) + b_ref[0]


def _mod_table(cond, w_mod, b_mod):
    depth = w_mod.shape[0]
    n = w_mod.shape[2] // D_MODEL
    return pl.pallas_call(
        _mod_kernel,
        grid=(depth, n),
        in_specs=[pl.BlockSpec((MOD_ROWS, D_MODEL), lambda l, c: (0, 0)),
                  pl.BlockSpec((1, D_MODEL, D_MODEL), lambda l, c: (l, 0, c)),
                  pl.BlockSpec((1, 1, D_MODEL), lambda l, c: (l, 0, c))],
        out_specs=pl.BlockSpec((1, MOD_ROWS, D_MODEL), lambda l, c: (l, 0, c)),
        out_shape=jax.ShapeDtypeStruct((depth, MOD_ROWS, n * D_MODEL), F32),
        compiler_params=_cparams(("parallel", "parallel")),
        name="mod_table",
    )(cond, w_mod, b_mod.reshape(depth, 1, n * D_MODEL))


def _inproj_kernel(x_ref, mod_ref, g1_ref, ropeT_ref, ropeR_ref,
                   wTm_ref, wTg_ref, bgT_ref, wkm_ref, wg_ref, bg_ref,
                   wTaq_ref, wak_ref, wTav_ref, wcv_ref,
                   qmT_o, km_o, vmT_o, omT_o, gT_o, g_o, qaT_o, ka_o, vaT_o, u_o,
                   *, seq, tm, ctx_row):
    b = pl.program_id(0)
    j = pl.program_id(1)
    is_ctx = _is_ctx_rows(j, tm, seq)
    h = _rms_mod(x_ref[0], g1_ref[...], mod_ref, b, ctx_row, is_ctx, 0, 1)
    hb = h.astype(BF16)
    hT = h.T.astype(BF16)

    mT = jnp.dot(wTm_ref[...], hT, preferred_element_type=F32)
    w = MLSTM_WIDTH
    qmT_o[0] = mT[0:w].astype(BF16)
    vmT_o[0] = mT[w:2 * w].astype(BF16)
    omT_o[0] = mT[2 * w:3 * w].astype(BF16)
    gT = jnp.dot(wTg_ref[...], hT, preferred_element_type=F32) + bgT_ref[...]
    rowi = lax.broadcasted_iota(jnp.int32, gT.shape, 0)
    gT_o[0] = jnp.where((rowi % 8) >= 4, _log_sigmoid(gT), gT)
    km_o[0] = jnp.dot(hb, wkm_ref[...], preferred_element_type=F32).astype(BF16)
    g = jnp.dot(hb, wg_ref[...], preferred_element_type=F32) + bg_ref[...]
    lanei = lax.broadcasted_iota(jnp.int32, g.shape, 1) % LANE
    g_o[0] = jnp.where((lanei >= 4) & (lanei < 8), _log_sigmoid(g), g)

    qT = jnp.dot(wTaq_ref[...], hT, preferred_element_type=F32)
    for grp in range(2 * N_DIFF_HEADS * 2):
        ax = grp % 2
        cos = ropeT_ref[ax * 16:(ax + 1) * 16, :]
        sin = ropeT_ref[32 + ax * 16:32 + (ax + 1) * 16, :]
        x1 = qT[grp * 32:grp * 32 + 16]
        x2 = qT[grp * 32 + 16:grp * 32 + 32]
        qaT_o[0, grp * 32:grp * 32 + 16, :] = (x1 * cos - x2 * sin).astype(BF16)
        qaT_o[0, grp * 32 + 16:grp * 32 + 32, :] = (x2 * cos + x1 * sin).astype(BF16)
    kk = jnp.dot(hb, wak_ref[...], preferred_element_type=F32)
    cosr = ropeR_ref[:, 0:LANE]
    sinr = ropeR_ref[:, LANE:2 * LANE]
    nk = 2 * N_DIFF_HEADS * DIFF_QK_DIM
    for sl in range(nk // LANE):
        k0 = kk[:, sl * LANE:(sl + 1) * LANE]
        k1 = kk[:, nk + sl * LANE:nk + (sl + 1) * LANE]
        ka_o[0, :, sl * LANE:(sl + 1) * LANE] = (k0 * cosr + k1 * sinr).astype(BF16)
    vT = jnp.dot(wTav_ref[...], hT, preferred_element_type=F32)
    for hh in range(N_DIFF_HEADS):
        vaT_o[0, 0, hh, 0:DIFF_V_DIM, :] = vT[hh * DIFF_V_DIM:(hh + 1) * DIFF_V_DIM].astype(BF16)
        vaT_o[0, 0, hh, DIFF_V_DIM:DIFF_V_DIM + VAUG, :] = jnp.ones((VAUG, tm), BF16)
    cv = jnp.dot(hb, wcv_ref[...], preferred_element_type=F32)
    u_o[0] = (cv[:, :CONV_WIDTH] * _sigmoid(cv[:, CONV_WIDTH:])).astype(BF16)


def _const_spec(shape):
    nd = len(shape)
    return pl.BlockSpec(shape, lambda *_: (0,) * nd)


def _inproj(xa, mod, layer, g1, ropeT, ropeR, wts, *, seq):
    bsz, t_all, _ = xa.shape
    tm = TOK_TILE
    nt = t_all // tm
    kern = functools.partial(_inproj_kernel, seq=seq, tm=tm, ctx_row=bsz)
    w = MLSTM_WIDTH
    out_shapes = (
        jax.ShapeDtypeStruct((bsz, w, t_all), BF16),
        jax.ShapeDtypeStruct((bsz, t_all, w), BF16),
        jax.ShapeDtypeStruct((bsz, w, t_all), BF16),
        jax.ShapeDtypeStruct((bsz, w, t_all), BF16),
        jax.ShapeDtypeStruct((bsz, 16, t_all), F32),
        jax.ShapeDtypeStruct((bsz, t_all, 2 * LANE), F32),
        jax.ShapeDtypeStruct((bsz, DIFF_WIDTH, t_all), BF16),
        jax.ShapeDtypeStruct((bsz, t_all, DIFF_WIDTH), BF16),
        jax.ShapeDtypeStruct((bsz, nt, N_DIFF_HEADS, DIFF_V_DIM + VAUG, tm), BF16),
        jax.ShapeDtypeStruct((bsz, t_all, CONV_WIDTH), BF16),
    )
    fm = lambda rows: pl.BlockSpec((1, rows, tm), lambda b, j: (b, 0, j))
    tk = lambda cols: pl.BlockSpec((1, tm, cols), lambda b, j: (b, j, 0))
    out_specs = (fm(w), tk(w), fm(w), fm(w), fm(16), tk(2 * LANE), fm(DIFF_WIDTH), tk(DIFF_WIDTH),
                 pl.BlockSpec((1, 1, N_DIFF_HEADS, DIFF_V_DIM + VAUG, tm), lambda b, j: (b, j, 0, 0, 0)),
                 tk(CONV_WIDTH))
    in_specs = [
        pl.BlockSpec((1, tm, D_MODEL), lambda b, j: (b, j, 0)),
        pl.BlockSpec((1, MOD_ROWS, 6 * D_MODEL), lambda b, j: (layer, 0, 0)),
        _const_spec((1, D_MODEL)),
        pl.BlockSpec((64, tm), lambda b, j: (0, j)),
        pl.BlockSpec((tm, 2 * LANE), lambda b, j: (j, 0)),
    ] + [_const_spec(a.shape) for a in wts]
    return pl.pallas_call(
        kern, grid=(bsz, nt), in_specs=in_specs, out_specs=out_specs, out_shape=out_shapes,
        compiler_params=_cparams(("parallel", "parallel")), name="inproj",
    )(xa, mod, g1, ropeT, ropeR, *wts)


def _mlstm_kernel(qf_ref, kf_ref, vf_ref, gTf_ref, gf_ref, qb_ref, kb_ref, vb_ref, gTb_ref, gb_ref,
                  hf_o, hb_o, c_scr, m_scr, *, chunk):
    step = pl.program_id(1)
    L = chunk
    hd = MLSTM_HEAD_DIM
    nh = N_MLSTM_HEADS

    @pl.when(step == 0)
    def _():
        c_scr[...] = jnp.zeros(c_scr.shape, F32)
        m_scr[...] = jnp.full(m_scr.shape, M_INIT, F32)

    si = lax.broadcasted_iota(jnp.int32, (L, L), 0)
    ti = lax.broadcasted_iota(jnp.int32, (L, L), 1)
    ones_rows = jnp.ones((VAUG, L), BF16)
    zero64 = jnp.zeros((hd, L), BF16)

    chains = []
    for d, (qT_ref, k_ref, vT_ref, gT_ref, g_ref, h_o) in enumerate(
            ((qf_ref, kf_ref, vf_ref, gTf_ref, gf_ref, hf_o), (qb_ref, kb_ref, vb_ref, gTb_ref, gb_ref, hb_o))):
        causal = (si <= ti) if d == 0 else (si >= ti)
        tri = jnp.where(causal, 1.0, 0.0).astype(F32)
        gT = gT_ref[0]
        gc = g_ref[0]
        b_rows = jnp.dot(gT, tri, preferred_element_type=F32, precision=HIGHEST)
        b_cols = jnp.dot(tri.T, gc, preferred_element_type=F32, precision=HIGHEST)
        totals = jnp.sum(gT, axis=1, keepdims=True)
        for hh in range(nh):
            pair, half = hh // 2, hh % 2
            q_h = qT_ref[0, hh * hd:(hh + 1) * hd, :]
            q_msk = jnp.concatenate([q_h, zero64] if half == 0 else [zero64, q_h], axis=0)
            k_pair = k_ref[0, :, pair * 2 * hd:(pair + 1) * 2 * hd]
            v_aug = jnp.concatenate([vT_ref[0, hh * hd:(hh + 1) * hd, :], ones_rows], axis=0)
            idx = d * nh + hh
            c_st = c_scr[idx]
            sT = jnp.dot(k_pair, q_msk, preferred_element_type=F32)
            cq = jnp.dot(c_st.astype(BF16), q_msk, preferred_element_type=F32)
            chains.append(dict(idx=idx, hh=hh, h_o=h_o, causal=causal, k_pair=k_pair, v_aug=v_aug,
                               c_st=c_st, sT=sT, cq=cq, li_row=gT[hh:hh + 1], b_row=b_rows[4 + hh:5 + hh],
                               total=totals[4 + hh:5 + hh],
                               a_col=gc[:, hh:hh + 1] - b_cols[:, 4 + hh:5 + hh]))

    for ch in chains:
        hh, b_row = ch["hh"], ch["b_row"]
        m_st = m_scr[ch["idx"], 0:1, :]
        dmat = jnp.where(ch["causal"], b_row + ch["a_col"], -jnp.inf)
        inter = b_row + m_st
        m_t = jnp.maximum(inter, jnp.max(dmat, axis=0, keepdims=True))
        wT = jnp.exp(dmat - m_t) * ch["sT"]
        e_inter = jnp.exp(inter - m_t)
        intra = jnp.dot(ch["v_aug"], wT.astype(BF16), preferred_element_type=F32)
        cq = ch["cq"]
        num = e_inter * cq[0:hd] + intra[0:hd]
        den = e_inter * cq[hd:hd + 1] + jnp.sum(wT, axis=0, keepdims=True)
        ch["h_o"][0, hh * hd:(hh + 1) * hd, :] = num / jnp.maximum(jnp.abs(den), jnp.exp(-m_t))

        total = ch["total"]
        g_row = total - b_row + ch["li_row"]
        m_prev = m_st[:, 0:1]
        m_new = jnp.maximum(total + m_prev, jnp.max(g_row, axis=1, keepdims=True))
        e_old = jnp.exp(total + m_prev - m_new)
        e_g = jnp.exp(g_row - m_new)
        upd = jnp.dot((ch["v_aug"].astype(F32) * e_g).astype(BF16), ch["k_pair"],
                      preferred_element_type=F32)
        c_scr[ch["idx"]] = e_old * ch["c_st"] + upd
        m_scr[ch["idx"]] = jnp.broadcast_to(m_new, m_scr.shape[1:])


def _mlstm(qmT, km, vmT, gT, g, *, seq):
    bsz, w, t_all = qmT.shape
    L = MLSTM_CHUNK
    nlat = seq // L
    nch = t_all // L
    nctx = nch - nlat
    fwd = lambda i: jnp.where(i < nctx, nlat + i, i - nctx)
    bwd = lambda i: nch - 1 - i

    def specs(chunk_of, d):
        return [pl.BlockSpec((1, w, L), lambda b, i: (b, 0, chunk_of(i))),
                pl.BlockSpec((1, L, w), lambda b, i: (b, chunk_of(i), 0)),
                pl.BlockSpec((1, w, L), lambda b, i: (b, 0, chunk_of(i))),
                pl.BlockSpec((1, 8, L), lambda b, i: (b, d, chunk_of(i))),
                pl.BlockSpec((1, L, LANE), lambda b, i: (b, chunk_of(i), d))]

    kern = functools.partial(_mlstm_kernel, chunk=L)
    out = jax.ShapeDtypeStruct((bsz, w, t_all), F32)
    return pl.pallas_call(
        kern, grid=(bsz, nch),
        in_specs=specs(fwd, 0) + specs(bwd, 1),
        out_specs=(pl.BlockSpec((1, w, L), lambda b, i: (b, 0, fwd(i))),
                   pl.BlockSpec((1, w, L), lambda b, i: (b, 0, bwd(i)))),
        out_shape=(out, out),
        scratch_shapes=[pltpu.VMEM((2 * N_MLSTM_HEADS, MLSTM_HEAD_DIM + VAUG, LANE), F32),
                        pltpu.VMEM((2 * N_MLSTM_HEADS, 8, L), F32)],
        compiler_params=_cparams(("parallel", "arbitrary")), name="mlstm_scan",
    )(qmT, km, vmT, gT, g, qmT, km, vmT, gT, g)


def _attn_kernel(lam_ref, gs_ref, qT_ref, k_ref, vT_ref, *rest, nch, tk, lam_init):
    o_ref = rest[-1]
    tq = qT_ref.shape[2]
    qT = qT_ref[0]
    z = jnp.zeros((DIFF_QK_DIM, tq), BF16)
    rhs = jnp.concatenate([jnp.concatenate([qT[:DIFF_QK_DIM], z], axis=0),
                           jnp.concatenate([z, qT[DIFF_QK_DIM:]], axis=0)], axis=1)
    ncb = 2 * tq // ATT_COLS
    ms = [jnp.full((1, ATT_COLS), NEG_BIG, F32) for _ in range(ncb)]
    accs = [jnp.zeros((DIFF_V_DIM + VAUG, ATT_COLS), F32) for _ in range(ncb)]
    sub = tk // ATT_KEYS
    units = [(c, cb) for c in range(nch * sub) for cb in range(ncb)]

    def scores(c, cb):
        return jnp.dot(k_ref[0, c * ATT_KEYS:(c + 1) * ATT_KEYS, :],
                       rhs[:, cb * ATT_COLS:(cb + 1) * ATT_COLS], preferred_element_type=F32)

    pending = [scores(*u) for u in units[:ATT_AHEAD]]
    for i, (c, cb) in enumerate(units):
        sT = pending.pop(0)
        if i + ATT_AHEAD < len(units):
            pending.append(scores(*units[i + ATT_AHEAD]))
        vT = vT_ref[0, c // sub, 0, :, (c % sub) * ATT_KEYS:(c % sub + 1) * ATT_KEYS]
        m_new = jnp.maximum(ms[cb], jnp.max(sT, axis=0, keepdims=True))
        p = jnp.exp2((sT - m_new).astype(BF16))
        alpha = jnp.exp2(ms[cb] - m_new)
        accs[cb] = alpha * accs[cb] + jnp.dot(vT, p, preferred_element_type=F32)
        ms[cb] = m_new
    acc = jnp.concatenate(accs, axis=1)
    l = acc[DIFF_V_DIM:DIFF_V_DIM + 1]
    acc = acc[0:DIFF_V_DIM]

    lv = lam_ref[...]
    lam = (jnp.exp(jnp.sum(lv[0:1] * lv[1:2], axis=1, keepdims=True))
           - jnp.exp(jnp.sum(lv[2:3] * lv[3:4], axis=1, keepdims=True)) + lam_init)
    oT = acc[:, :tq] / l[:, :tq] - lam * (acc[:, tq:] / l[:, tq:])
    oT = oT * lax.rsqrt(jnp.mean(oT * oT, axis=0, keepdims=True) + EPS) * gs_ref[...] * (1.0 - lam_init)
    o_ref[0] = oT.T.astype(BF16)


def _attention(lamv, gs_col, qaT, ka, vaT, d_prev, out_rows, *, q_tile, q_blk0, n_q, k_rows, k_blk0,
               v_chunks, v_chunk0, v_cols, v_blk0, lam_init):
    bsz = qaT.shape[0]
    kern = functools.partial(_attn_kernel, nch=v_chunks, tk=v_cols, lam_init=lam_init)
    in_specs = [_const_spec(lamv.shape), _const_spec((DIFF_V_DIM, 1)),
                pl.BlockSpec((1, DIFF_V_DIM, q_tile), lambda b, h, i: (b, h, q_blk0 + i)),
                pl.BlockSpec((1, k_rows, DIFF_V_DIM), lambda b, h, i: (b, k_blk0, h)),
                pl.BlockSpec((1, v_chunks, 1, DIFF_V_DIM + VAUG, v_cols),
                             lambda b, h, i: (b, v_chunk0, h, 0, v_blk0))]
    args = [lamv, gs_col, qaT, ka, vaT]
    aliases = {}
    if d_prev is not None:
        in_specs.append(pl.BlockSpec(memory_space=pl.ANY))
        aliases = {len(args): 0}
        args.append(d_prev)
    return pl.pallas_call(
        kern, grid=(bsz, N_DIFF_HEADS, n_q), in_specs=in_specs,
        out_specs=pl.BlockSpec((1, q_tile, DIFF_V_DIM), lambda b, h, i: (b, q_blk0 + i, h)),
        out_shape=jax.ShapeDtypeStruct((bsz, out_rows, DIFF_WIDTH), BF16),
        input_output_aliases=aliases,
        compiler_params=_cparams(("parallel", "parallel", "arbitrary")), name="diff_attn",
    )(*args)


def _conv_kernel(l_ref, c_ref, r_ref, w_ref, b_ref, g_ref, bb_ref, o_ref, buf, shifted, *, seq, tc, nt):
    j = pl.program_id(1)
    start = j * tc
    lvalid = jnp.logical_and(j > 0, start != seq)
    rvalid = jnp.logical_and(j < nt - 1, start + tc != seq)
    hl = CONV_HALO
    buf[0:hl, :] = jnp.where(lvalid, l_ref[0].astype(F32), 0.0)
    buf[hl:hl + tc, :] = c_ref[0].astype(F32)
    buf[hl + tc:2 * hl + tc, :] = jnp.where(rvalid, r_ref[0].astype(F32), 0.0)
    pad = CONV_KERNEL // 2
    sub = 128
    sl = CONV_SHIFTS
    for r in range(sl):
        shifted[r] = buf[pl.ds(r, shifted.shape[1]), :]
    for r0 in range(0, tc, sub):
        accs = [jnp.zeros((sub, LANE), F32) for _ in range(CONV_WIDTH // LANE)]
        for t in range(CONV_KERNEL):
            q, r = divmod(hl - pad + t, sl)
            for cb in range(CONV_WIDTH // LANE):
                cols = slice(cb * LANE, (cb + 1) * LANE)
                accs[cb] = accs[cb] + w_ref[t:t + 1, cols] * shifted[r, pl.ds(r0 + sl * q, sub), cols]
        y = jnp.concatenate(accs, axis=1) + b_ref[...]
        mu = jnp.mean(y, axis=-1, keepdims=True)
        var = jnp.mean(jnp.square(y - mu), axis=-1, keepdims=True)
        z = (y - mu) * lax.rsqrt(var + EPS) * g_ref[...] + bb_ref[...]
        o_ref[0, r0:r0 + sub, :] = (z * _sigmoid(z)).astype(BF16)


def _conv(u, w_dw, b_dw, g_ln, b_ln, *, seq):
    bsz, t_all, cw = u.shape
    tc = CONV_TILE
    nt = t_all // tc
    r = tc // CONV_HALO
    nhalo = t_all // CONV_HALO
    kern = functools.partial(_conv_kernel, seq=seq, tc=tc, nt=nt)
    row = lambda a: a.reshape(1, cw)
    return pl.pallas_call(
        kern, grid=(bsz, nt),
        in_specs=[pl.BlockSpec((1, CONV_HALO, cw), lambda b, j: (b, jnp.maximum(j * r - 1, 0), 0)),
                  pl.BlockSpec((1, tc, cw), lambda b, j: (b, j, 0)),
                  pl.BlockSpec((1, CONV_HALO, cw), lambda b, j: (b, jnp.minimum((j + 1) * r, nhalo - 1), 0)),
                  _const_spec((CONV_KERNEL, cw)), _const_spec((1, cw)), _const_spec((1, cw)),
                  _const_spec((1, cw))],
        out_specs=pl.BlockSpec((1, tc, cw), lambda b, j: (b, j, 0)),
        out_shape=jax.ShapeDtypeStruct((bsz, t_all, cw), BF16),
        scratch_shapes=[pltpu.VMEM((tc + 2 * CONV_HALO, cw), F32),
                        pltpu.VMEM((CONV_SHIFTS, tc + 2 * CONV_HALO - CONV_SHIFTS, cw), F32)],
        compiler_params=_cparams(("parallel", "parallel")), name="conv_mixer",
    )(u, u, u, w_dw, row(b_dw), row(g_ln), row(b_ln))


def _mixout_kernel(x_ref, mod_ref, hf_ref, hb_ref, omT_ref, gm_ref, d_ref, c_ref, wm_ref, wd_ref, wc_ref, o_ref,
                   *, seq, tm, ctx_row):
    b = pl.program_id(0)
    j = pl.program_id(1)
    is_ctx = _is_ctx_rows(j, tm, seq)
    hT = hf_ref[0] + hb_ref[0]
    h4 = hT.reshape(N_MLSTM_HEADS, MLSTM_HEAD_DIM, tm)
    mu = jnp.mean(h4, axis=1, keepdims=True)
    var = jnp.mean(jnp.square(h4 - mu), axis=1, keepdims=True)
    hn = ((h4 - mu) * lax.rsqrt(var + EPS)).reshape(MLSTM_WIDTH, tm)
    mT = _sigmoid(omT_ref[0].astype(F32)) * hn * gm_ref[...]
    m = mT.T.astype(BF16)
    y = (jnp.dot(m, wm_ref[...], preferred_element_type=F32)
         + jnp.dot(d_ref[0], wd_ref[...], preferred_element_type=F32)
         + jnp.dot(c_ref[0], wc_ref[...], preferred_element_type=F32))
    g_l, g_c = _mod_rows(mod_ref, b, ctx_row, 2)
    o_ref[0] = x_ref[0] + jnp.where(is_ctx, g_c, g_l) * y


def _mixout(xa, mod, layer, hTf, hTb, omT, gm_col, d, cx, wm, wd, wc, *, seq):
    bsz, t_all, _ = xa.shape
    tm = TOK_TILE
    nt = t_all // tm
    kern = functools.partial(_mixout_kernel, seq=seq, tm=tm, ctx_row=bsz)
    w = MLSTM_WIDTH
    return pl.pallas_call(
        kern, grid=(bsz, nt),
        in_specs=[pl.BlockSpec((1, tm, D_MODEL), lambda b, j: (b, j, 0)),
                  pl.BlockSpec((1, MOD_ROWS, 6 * D_MODEL), lambda b, j: (layer, 0, 0)),
                  pl.BlockSpec((1, w, tm), lambda b, j: (b, 0, j)),
                  pl.BlockSpec((1, w, tm), lambda b, j: (b, 0, j)),
                  pl.BlockSpec((1, w, tm), lambda b, j: (b, 0, j)),
                  _const_spec((w, 1)),
                  pl.BlockSpec((1, tm, DIFF_WIDTH), lambda b, j: (b, j, 0)),
                  pl.BlockSpec((1, tm, CONV_WIDTH), lambda b, j: (b, j, 0)),
                  _const_spec(wm.shape), _const_spec(wd.shape), _const_spec(wc.shape)],
        out_specs=pl.BlockSpec((1, tm, D_MODEL), lambda b, j: (b, j, 0)),
        out_shape=jax.ShapeDtypeStruct(xa.shape, F32),
        compiler_params=_cparams(("parallel", "parallel")), name="mix_out",
    )(xa, mod, hTf, hTb, omT, gm_col, d, cx, wm, wd, wc)


def _swiglu_partial(hb, wg, wu, wd):
    a = jnp.dot(hb, wg, preferred_element_type=F32)
    u = jnp.dot(hb, wu, preferred_element_type=F32)
    t = (a * _sigmoid(a) * u).astype(BF16)
    return jnp.dot(t, wd, preferred_element_type=F32)


def _ffn_kernel(x_ref, mod_ref, g2_ref, wg_ref, wu_ref, wd_ref, o_ref, hb_scr, acc_scr,
                *, seq, tm, ctx_row, nf):
    b = pl.program_id(0)
    j = pl.program_id(1)
    f = pl.program_id(2)
    is_ctx = _is_ctx_rows(j, tm, seq)

    @pl.when(f == 0)
    def _():
        h = _rms_mod(x_ref[0], g2_ref[...], mod_ref, b, ctx_row, is_ctx, 3, 4)
        hb_scr[...] = h.astype(BF16)
        acc_scr[...] = jnp.zeros(acc_scr.shape, F32)

    acc_scr[...] += _swiglu_partial(hb_scr[...], wg_ref[0], wu_ref[0], wd_ref[0])

    @pl.when(f == nf - 1)
    def _():
        g_l, g_c = _mod_rows(mod_ref, b, ctx_row, 5)
        o_ref[0] = x_ref[0] + jnp.where(is_ctx, g_c, g_l) * acc_scr[...]


def _ffn(xa, mod, layer, g2, wg, wu, wd, ffn_layer, *, seq):
    bsz, t_all, _ = xa.shape
    tm = TOK_TILE
    nt = t_all // tm
    nf = FFN_SPLIT
    tf = wg.shape[2] // nf
    assert tf % LANE == 0 and tf * nf == wg.shape[2]
    kern = functools.partial(_ffn_kernel, seq=seq, tm=tm, ctx_row=bsz, nf=nf)
    return pl.pallas_call(
        kern, grid=(bsz, nt, nf),
        in_specs=[pl.BlockSpec((1, tm, D_MODEL), lambda b, j, f: (b, j, 0)),
                  pl.BlockSpec((1, MOD_ROWS, 6 * D_MODEL), lambda b, j, f: (layer, 0, 0)),
                  _const_spec((1, D_MODEL)),
                  pl.BlockSpec((1, D_MODEL, tf), lambda b, j, f: (ffn_layer, 0, f)),
                  pl.BlockSpec((1, D_MODEL, tf), lambda b, j, f: (ffn_layer, 0, f)),
                  pl.BlockSpec((1, tf, D_MODEL), lambda b, j, f: (ffn_layer, f, 0))],
        out_specs=pl.BlockSpec((1, tm, D_MODEL), lambda b, j, f: (b, j, 0)),
        out_shape=jax.ShapeDtypeStruct(xa.shape, F32),
        scratch_shapes=[pltpu.VMEM((tm, D_MODEL), BF16), pltpu.VMEM((tm, D_MODEL), F32)],
        compiler_params=_cparams(("parallel", "parallel", "arbitrary")), name="ffn_swiglu",
    )(xa, mod, g2, wg, wu, wd)


def _moe_kernel(x_ref, mod_ref, g2_ref, wr_ref, br_ref, wg_ref, wu_ref, wd_ref, o_ref,
                hb_scr, acc_scr, comb_scr, slot_scr, slotT_scr, *, seq, tm, nt, ctx_row, ne, group):
    step = pl.program_id(0)
    e = pl.program_id(1)
    lane = lax.broadcasted_iota(jnp.int32, (tm, LANE), 1)
    sb = MOE_SUB
    row_i = lax.broadcasted_iota(jnp.int32, (sb, 1), 0).astype(F32)
    col_i = lax.broadcasted_iota(jnp.int32, (1, sb), 1).astype(F32)

    for half in range(group):
        tile = step * group + half
        b = tile // nt
        is_ctx = _is_ctx_rows(tile % nt, tm, seq)
        rows = slice(half * tm, (half + 1) * tm)

        @pl.when(e == 0)
        def _():
            h = _rms_mod(x_ref[rows, :], g2_ref[...], mod_ref, b, ctx_row, is_ctx, 3, 4)
            hb_scr[half] = h.astype(BF16)
            acc_scr[half] = jnp.zeros((tm, D_MODEL), F32)
            logits = jnp.dot(h, wr_ref[...], preferred_element_type=F32, precision=HIGHEST) + br_ref[...]
            logits = jnp.where(lane < ne, logits, -jnp.inf)
            ex = jnp.exp(logits - jnp.max(logits, axis=-1, keepdims=True))
            probs = ex / jnp.sum(ex, axis=-1, keepdims=True)
            v1 = jnp.max(probs, axis=-1, keepdims=True)
            i1 = jnp.min(jnp.where(probs == v1, lane, LANE), axis=-1, keepdims=True)
            rest = jnp.where(lane == i1, -1.0, probs)
            v2 = jnp.max(rest, axis=-1, keepdims=True)
            i2 = jnp.min(jnp.where(rest == v2, lane, LANE), axis=-1, keepdims=True)
            tot = v1 + v2
            comb_scr[half] = jnp.where(lane == i1, v1 / tot, 0.0) + jnp.where(lane == i2, v2 / tot, 0.0)
            sel = jnp.logical_or(lane == i1, lane == i2)
            ri = lax.broadcasted_iota(jnp.int32, (tm, tm), 0)
            ci = lax.broadcasted_iota(jnp.int32, (tm, tm), 1)
            before = jnp.where(ci < ri, 1.0, 0.0).astype(BF16)
            rank = jnp.dot(before, jnp.where(sel, 1.0, 0.0).astype(BF16), preferred_element_type=F32)
            slot = jnp.where(sel, rank, -1.0)
            slot_scr[half] = slot
            slotT_scr[half] = slot.T

        onlane = lane == e
        cw = jnp.sum(jnp.where(onlane, comb_scr[half], 0.0), axis=-1, keepdims=True)
        slot_c = jnp.max(jnp.where(onlane, slot_scr[half], -1.0), axis=-1, keepdims=True)
        slot_r = slotT_scr[half, pl.ds(e, 1), :]
        count = (jnp.max(slot_r) + 1.0).astype(jnp.int32)

        def sub_block(i, carry):
            base = (i * sb).astype(F32)
            gather = jnp.where(slot_r == base + row_i, 1.0, 0.0).astype(BF16)
            xs = jnp.dot(gather, hb_scr[half], preferred_element_type=F32).astype(BF16)
            y = _swiglu_partial(xs, wg_ref[0, 0], wu_ref[0, 0], wd_ref[0, 0])
            scatter = jnp.where(slot_c == base + col_i, 1.0, 0.0).astype(BF16)
            acc_scr[half] += cw * jnp.dot(scatter, y.astype(BF16), preferred_element_type=F32)
            return carry

        lax.fori_loop(0, (count + sb - 1) // sb, sub_block, 0)

        @pl.when(e == ne - 1)
        def _():
            g_l, g_c = _mod_rows(mod_ref, b, ctx_row, 5)
            o_ref[rows, :] = x_ref[rows, :] + jnp.where(is_ctx, g_c, g_l) * acc_scr[half]


def _moe(xa, mod, layer, g2, wr, br, wg, wu, wd, moe_layer, *, seq):
    bsz, t_all, _ = xa.shape
    tm = TOK_TILE
    nt = t_all // tm
    group = MOE_GROUP
    assert (bsz * nt) % group == 0
    ne, fe = wg.shape[1], wg.shape[3]
    kern = functools.partial(_moe_kernel, seq=seq, tm=tm, nt=nt, ctx_row=bsz, ne=ne, group=group)
    once = pl.Buffered(1)
    out = pl.pallas_call(
        kern, grid=(bsz * nt // group, ne),
        in_specs=[pl.BlockSpec((group * tm, D_MODEL), lambda s, e: (s, 0), pipeline_mode=once),
                  pl.BlockSpec((1, MOD_ROWS, 6 * D_MODEL), lambda s, e: (layer, 0, 0)),
                  _const_spec((1, D_MODEL)), _const_spec(wr.shape), _const_spec(br.shape),
                  pl.BlockSpec((1, 1, D_MODEL, fe), lambda s, e: (moe_layer, e, 0, 0)),
                  pl.BlockSpec((1, 1, D_MODEL, fe), lambda s, e: (moe_layer, e, 0, 0)),
                  pl.BlockSpec((1, 1, fe, D_MODEL), lambda s, e: (moe_layer, e, 0, 0))],
        out_specs=pl.BlockSpec((group * tm, D_MODEL), lambda s, e: (s, 0), pipeline_mode=once),
        out_shape=jax.ShapeDtypeStruct((bsz * t_all, D_MODEL), F32),
        scratch_shapes=[pltpu.VMEM((group, tm, D_MODEL), BF16), pltpu.VMEM((group, tm, D_MODEL), F32),
                        pltpu.VMEM((group, tm, LANE), F32), pltpu.VMEM((group, tm, LANE), F32),
                        pltpu.VMEM((group, LANE, tm), F32)],
        compiler_params=_cparams(("parallel", "arbitrary")), name="moe_swiglu",
    )(xa.reshape(bsz * t_all, D_MODEL), mod, g2, wr, br, wg, wu, wd)
    return out.reshape(xa.shape)


def _final_kernel(x_ref, g_ref, o_ref):
    x = x_ref[0]
    o_ref[0] = x * lax.rsqrt(jnp.mean(x * x, axis=-1, keepdims=True) + EPS) * g_ref[...]


def _final_norm(xa, g, *, seq):
    bsz = xa.shape[0]
    tf = 512
    return pl.pallas_call(
        _final_kernel, grid=(bsz, seq // tf),
        in_specs=[pl.BlockSpec((1, tf, D_MODEL), lambda b, j: (b, j, 0)), _const_spec((1, D_MODEL))],
        out_specs=pl.BlockSpec((1, tf, D_MODEL), lambda b, j: (b, j, 0)),
        out_shape=jax.ShapeDtypeStruct((bsz, seq, D_MODEL), F32),
        compiler_params=_cparams(("parallel", "parallel")), name="final_norm",
    )(xa, g.reshape(1, D_MODEL))


def _rope_tables(seq, t_all):
    pos = np.arange(seq)
    per_axis = DIFF_QK_DIM // 2
    inv = (ROPE_BASE ** (-np.arange(0, per_axis, 2, dtype=np.float32) / per_axis)).astype(np.float32)
    rowp = (pos // GRID_W).astype(np.float32)
    colp = (pos % GRID_W).astype(np.float32)
    ang = np.stack([rowp[:, None] * inv, colp[:, None] * inv], axis=1).astype(np.float64)
    cos = np.concatenate([np.cos(ang), np.ones((t_all - seq, 2, 16))], axis=0)
    sin = np.concatenate([np.sin(ang), np.zeros((t_all - seq, 2, 16))], axis=0)
    qscale = (DIFF_QK_DIM ** -0.5) * LOG2E
    ropeT = np.concatenate([cos.reshape(t_all, 32).T, sin.reshape(t_all, 32).T], axis=0) * qscale
    cos64 = np.concatenate([cos[:, 0], cos[:, 0], cos[:, 1], cos[:, 1]], axis=-1)
    sin64 = np.concatenate([-sin[:, 0], sin[:, 0], -sin[:, 1], sin[:, 1]], axis=-1)
    ropeR = np.concatenate([cos64, cos64, sin64, sin64], axis=-1)
    return jnp.asarray(ropeT, F32), jnp.asarray(ropeR, F32)


def _swap_perm():
    idx = np.arange(2 * N_DIFF_HEADS * DIFF_QK_DIM)
    return np.where((idx % 32) < 16, idx + 16, idx - 16)


def _prep_inproj_weights(w_in_l, b_gates_l):
    offs = np.cumsum((0,) + IN_SPLITS)
    col = lambda i: w_in_l[:, offs[i]:offs[i + 1]]
    mq, mk, mv, mo, gt, aq, ak, av, cv = (col(i) for i in range(9))
    nh = N_MLSTM_HEADS
    wTm = jnp.concatenate([mq, mv, mo], axis=1).T.astype(BF16)
    wTg = gt.T.astype(BF16)
    bgT = b_gates_l.reshape(4 * nh, 1).astype(F32)
    wkm = (mk * (MLSTM_HEAD_DIM ** -0.5)).astype(BF16)
    zpad = jnp.zeros((D_MODEL, LANE - 2 * nh), F32)
    wg = jnp.concatenate([gt[:, 0:2 * nh], zpad, gt[:, 2 * nh:4 * nh], zpad], axis=1).astype(BF16)
    bpad = jnp.zeros((LANE - 2 * nh,), F32)
    bg = jnp.concatenate([b_gates_l[0:2 * nh], bpad, b_gates_l[2 * nh:4 * nh], bpad]).reshape(1, 2 * LANE)
    wTaq = aq.T.astype(BF16)
    wak = jnp.concatenate([ak, ak[:, _swap_perm()]], axis=1).astype(BF16)
    wTav = av.T.astype(BF16)
    wcv = cv.astype(BF16)
    return (wTm, wTg, bgT, wkm, wg, bg, wTaq, wak, wTav, wcv)


def kernel(x, c, ctx, c_ctx, w_mod, b_mod, g_norm1, w_in, b_gates, g_mlstm, lambda_q1, lambda_k1,
           lambda_q2, lambda_k2, g_subln, w_dw, b_dw, g_conv_ln, b_conv_ln, w_out, g_norm2,
           w_ffn_gate, w_ffn_up, w_ffn_down, w_router, b_router, w_exp_gate, w_exp_up, w_exp_down,
           g_final):
    bsz, seq, _ = x.shape
    nctx = ctx.shape[1]
    t_all = seq + nctx
    depth = w_mod.shape[0]
    assert nctx == CTX_LEN == MLSTM_CHUNK and bsz + 1 <= MOD_ROWS
    assert t_all % TOK_TILE == 0 and seq % Q_TILE == 0 and seq % CONV_TILE == 0 and seq % GRID_W == 0

    xa = jnp.concatenate([x, ctx], axis=1)
    cond = jnp.concatenate([c, c_ctx[None, :], jnp.zeros((MOD_ROWS - bsz - 1, D_MODEL), F32)], axis=0)
    mod = _mod_table(cond, w_mod, b_mod)
    ropeT, ropeR = _rope_tables(seq, t_all)
    nkc = t_all // TOK_TILE
    weg, weu, wed = w_exp_gate.astype(BF16), w_exp_up.astype(BF16), w_exp_down.astype(BF16)
    wfg, wfu, wfd = w_ffn_gate.astype(BF16), w_ffn_up.astype(BF16), w_ffn_down.astype(BF16)

    for l in range(depth):
        wts = _prep_inproj_weights(w_in[l], b_gates[l])
        (qmT, km, vmT, omT, gT, g, qaT, ka, vaT, u) = _inproj(
            xa, mod, l, g_norm1[l].reshape(1, D_MODEL), ropeT, ropeR, wts, seq=seq)

        hTf, hTb = _mlstm(qmT, km, vmT, gT, g, seq=seq)

        lam_init = 0.8 - 0.6 * math.exp(-0.3 * l)
        lamv = jnp.zeros((8, LANE), F32).at[0:4, 0:DIFF_QK_DIM].set(
            jnp.stack([lambda_q1[l], lambda_k1[l], lambda_q2[l], lambda_k2[l]]).astype(F32))
        gs_col = g_subln[l].reshape(DIFF_V_DIM, 1).astype(F32)
        d = _attention(lamv, gs_col, qaT, ka, vaT, None, t_all, q_tile=Q_TILE, q_blk0=0,
                       n_q=seq // Q_TILE, k_rows=t_all, k_blk0=0, v_chunks=nkc, v_chunk0=0,
                       v_cols=TOK_TILE, v_blk0=0, lam_init=lam_init)
        d = _attention(lamv, gs_col, qaT, ka, vaT, d, t_all, q_tile=nctx, q_blk0=seq // nctx, n_q=1,
                       k_rows=nctx, k_blk0=seq // nctx, v_chunks=1, v_chunk0=nkc - 1,
                       v_cols=nctx, v_blk0=TOK_TILE // nctx - 1, lam_init=lam_init)

        cx = _conv(u, w_dw[l], b_dw[l], g_conv_ln[l], b_conv_ln[l], seq=seq)

        wo = w_out[l].astype(BF16)
        xa = _mixout(xa, mod, l, hTf, hTb, omT, g_mlstm[l].reshape(MLSTM_WIDTH, 1).astype(F32), d, cx,
                     wo[0:MLSTM_WIDTH], wo[MLSTM_WIDTH:MLSTM_WIDTH + DIFF_WIDTH],
                     wo[MLSTM_WIDTH + DIFF_WIDTH:], seq=seq)

        jj = l // 2
        g2 = g_norm2[l].reshape(1, D_MODEL)
        if l % 2 == 0:
            xa = _ffn(xa, mod, l, g2, wfg, wfu, wfd, jj, seq=seq)
        else:
            wr = jnp.concatenate([w_router[jj], jnp.zeros((D_MODEL, LANE - N_EXPERTS), F32)], axis=1)
            br = jnp.concatenate([b_router[jj], jnp.zeros((LANE - N_EXPERTS,), F32)]).reshape(1, LANE)
            xa = _moe(xa, mod, l, g2, wr, br, weg, weu, wed, jj, seq=seq)

    return _final_norm(xa, g_final, seq=seq)
```

```python
import functools
import math

import jax
import jax.numpy as jnp
import numpy as np
from jax import lax
from jax.experimental import pallas as pl
from jax.experimental.pallas import tpu as pltpu

F32 = jnp.float32
BF16 = jnp.bfloat16
HIGHEST = lax.Precision.HIGHEST

D_MODEL = 1024
DEPTH = 4
GRID_W = 64
CTX_LEN = 256
N_MLSTM_HEADS = 4
MLSTM_HEAD_DIM = 64
MLSTM_WIDTH = N_MLSTM_HEADS * MLSTM_HEAD_DIM
N_DIFF_HEADS = 4
DIFF_QK_DIM = 64
DIFF_V_DIM = 2 * DIFF_QK_DIM
DIFF_WIDTH = N_DIFF_HEADS * DIFF_V_DIM
ROPE_BASE = 10000.0
CONV_WIDTH = 256
CONV_KERNEL = 31
IN_SPLITS = (MLSTM_WIDTH, MLSTM_WIDTH, MLSTM_WIDTH, MLSTM_WIDTH, 4 * N_MLSTM_HEADS,
             2 * N_DIFF_HEADS * DIFF_QK_DIM, 2 * N_DIFF_HEADS * DIFF_QK_DIM, DIFF_WIDTH,
             2 * CONV_WIDTH)
D_FF = 2816
N_EXPERTS = 8
D_FF_EXPERT = 1408
EPS = 1e-6
M_INIT = -1e30
NEG_BIG = -1e30
LOG2E = 1.4426950408889634

LANE = 128
V7X_VMEM_LIMIT = 56 * 1024 * 1024
TOK_TILE = 768
MLSTM_CHUNK = 256
CONV_TILE = 256
CONV_HALO = 16
CONV_SHIFTS = 8
Q_TILE = 1024
MOD_ROWS = 8
VAUG = 16
MOE_SUB = 224
FFN_SPLIT = 2
MOE_GROUP = 1
MLSTM_BATCH = 2
ATT_KEYS = 256
ATT_COLS = 256
ATT_AHEAD = 4


def _cparams(sem):
    return pltpu.CompilerParams(dimension_semantics=sem, vmem_limit_bytes=V7X_VMEM_LIMIT)


def _sigmoid(v):
    return 1.0 / (1.0 + jnp.exp(-v))


def _log_sigmoid(v):
    return jnp.minimum(v, 0.0) - jnp.log(1.0 + jnp.exp(-jnp.abs(v)))


def _mod_rows(mod_ref, b, ctx_row, k):
    lat = mod_ref[0, pl.ds(b, 1), k * D_MODEL:(k + 1) * D_MODEL]
    ctx = mod_ref[0, ctx_row:ctx_row + 1, k * D_MODEL:(k + 1) * D_MODEL]
    return lat, ctx


def _is_ctx_rows(j, tm, seq):
    rows = j * tm + lax.broadcasted_iota(jnp.int32, (tm, 1), 0)
    return rows >= seq


def _rms_mod(x, g, mod_ref, b, ctx_row, is_ctx, k_shift, k_scale):
    y = x * lax.rsqrt(jnp.mean(x * x, axis=-1, keepdims=True) + EPS) * g
    sh_l, sh_c = _mod_rows(mod_ref, b, ctx_row, k_shift)
    sc_l, sc_c = _mod_rows(mod_ref, b, ctx_row, k_scale)
    shift = jnp.where(is_ctx, sh_c, sh_l)
    scale = jnp.where(is_ctx, sc_c, sc_l)
    return y * (1.0 + scale) + shift


def _mod_kernel(cond_ref, w_ref, b_ref, o_ref):
    c = cond_ref[...]
    s = c * _sigmoid(c)
    o_ref[0] = jnp.dot(s, w_ref[0], preferred_element_type=F32, precision=HIGHEST) + b_ref[0]


def _mod_table(cond, w_mod, b_mod):
    depth = w_mod.shape[0]
    n = w_mod.shape[2] // D_MODEL
    return pl.pallas_call(
        _mod_kernel,
        grid=(depth, n),
        in_specs=[pl.BlockSpec((MOD_ROWS, D_MODEL), lambda l, c: (0, 0)),
                  pl.BlockSpec((1, D_MODEL, D_MODEL), lambda l, c: (l, 0, c)),
                  pl.BlockSpec((1, 1, D_MODEL), lambda l, c: (l, 0, c))],
        out_specs=pl.BlockSpec((1, MOD_ROWS, D_MODEL), lambda l, c: (l, 0, c)),
        out_shape=jax.ShapeDtypeStruct((depth, MOD_ROWS, n * D_MODEL), F32),
        compiler_params=_cparams(("parallel", "parallel")),
        name="mod_table",
    )(cond, w_mod, b_mod.reshape(depth, 1, n * D_MODEL))


def _inproj_kernel(x_ref, mod_ref, g1_ref, ropeT_ref, ropeR_ref,
                   wTm_ref, wTg_ref, bgT_ref, wkm_ref, wg_ref, bg_ref,
                   wTaq_ref, wak_ref, wTav_ref, wcv_ref,
                   qmT_o, km_o, vmT_o, omT_o, gT_o, g_o, qaT_o, ka_o, vaT_o, u_o,
                   *, seq, tm, ctx_row):
    b = pl.program_id(0)
    j = pl.program_id(1)
    is_ctx = _is_ctx_rows(j, tm, seq)
    h = _rms_mod(x_ref[0], g1_ref[...], mod_ref, b, ctx_row, is_ctx, 0, 1)
    hb = h.astype(BF16)
    hT = h.T.astype(BF16)

    mT = jnp.dot(wTm_ref[...], hT, preferred_element_type=F32)
    w = MLSTM_WIDTH
    qmT_o[0] = mT[0:w].astype(BF16)
    vmT_o[0] = mT[w:2 * w].astype(BF16)
    omT_o[0] = mT[2 * w:3 * w].astype(BF16)
    gT = jnp.dot(wTg_ref[...], hT, preferred_element_type=F32) + bgT_ref[...]
    rowi = lax.broadcasted_iota(jnp.int32, gT.shape, 0)
    gT_o[0] = jnp.where((rowi % 8) >= 4, _log_sigmoid(gT), gT)
    km_o[0] = jnp.dot(hb, wkm_ref[...], preferred_element_type=F32).astype(BF16)
    g = jnp.dot(hb, wg_ref[...], preferred_element_type=F32) + bg_ref[...]
    lanei = lax.broadcasted_iota(jnp.int32, g.shape, 1) % LANE
    g_o[0] = jnp.where((lanei >= 4) & (lanei < 8), _log_sigmoid(g), g)

    qT = jnp.dot(wTaq_ref[...], hT, preferred_element_type=F32)
    for grp in range(2 * N_DIFF_HEADS * 2):
        ax = grp % 2
        cos = ropeT_ref[ax * 16:(ax + 1) * 16, :]
        sin = ropeT_ref[32 + ax * 16:32 + (ax + 1) * 16, :]
        x1 = qT[grp * 32:grp * 32 + 16]
        x2 = qT[grp * 32 + 16:grp * 32 + 32]
        qaT_o[0, grp * 32:grp * 32 + 16, :] = (x1 * cos - x2 * sin).astype(BF16)
        qaT_o[0, grp * 32 + 16:grp * 32 + 32, :] = (x2 * cos + x1 * sin).astype(BF16)
    kk = jnp.dot(hb, wak_ref[...], preferred_element_type=F32)
    cosr = ropeR_ref[:, 0:LANE]
    sinr = ropeR_ref[:, LANE:2 * LANE]
    nk = 2 * N_DIFF_HEADS * DIFF_QK_DIM
    for sl in range(nk // LANE):
        k0 = kk[:, sl * LANE:(sl + 1) * LANE]
        k1 = kk[:, nk + sl * LANE:nk + (sl + 1) * LANE]
        ka_o[0, :, sl * LANE:(sl + 1) * LANE] = (k0 * cosr + k1 * sinr).astype(BF16)
    vT = jnp.dot(wTav_ref[...], hT, preferred_element_type=F32)
    for hh in range(N_DIFF_HEADS):
        vaT_o[0, 0, hh, 0:DIFF_V_DIM, :] = vT[hh * DIFF_V_DIM:(hh + 1) * DIFF_V_DIM].astype(BF16)
        vaT_o[0, 0, hh, DIFF_V_DIM:DIFF_V_DIM + VAUG, :] = jnp.ones((VAUG, tm), BF16)
    cv = jnp.dot(hb, wcv_ref[...], preferred_element_type=F32)
    u_o[0] = (cv[:, :CONV_WIDTH] * _sigmoid(cv[:, CONV_WIDTH:])).astype(BF16)


def _const_spec(shape):
    nd = len(shape)
    return pl.BlockSpec(shape, lambda *_: (0,) * nd)


def _inproj(xa, mod, layer, g1, ropeT, ropeR, wts, *, seq):
    bsz, t_all, _ = xa.shape
    tm = TOK_TILE
    nt = t_all // tm
    kern = functools.partial(_inproj_kernel, seq=seq, tm=tm, ctx_row=bsz)
    w = MLSTM_WIDTH
    out_shapes = (
        jax.ShapeDtypeStruct((bsz, w, t_all), BF16),
        jax.ShapeDtypeStruct((bsz, t_all, w), BF16),
        jax.ShapeDtypeStruct((bsz, w, t_all), BF16),
        jax.ShapeDtypeStruct((bsz, w, t_all), BF16),
        jax.ShapeDtypeStruct((bsz, 16, t_all), F32),
        jax.ShapeDtypeStruct((bsz, t_all, 2 * LANE), F32),
        jax.ShapeDtypeStruct((bsz, DIFF_WIDTH, t_all), BF16),
        jax.ShapeDtypeStruct((bsz, t_all, DIFF_WIDTH), BF16),
        jax.ShapeDtypeStruct((bsz, nt, N_DIFF_HEADS, DIFF_V_DIM + VAUG, tm), BF16),
        jax.ShapeDtypeStruct((bsz, t_all, CONV_WIDTH), BF16),
    )
    fm = lambda rows: pl.BlockSpec((1, rows, tm), lambda b, j: (b, 0, j))
    tk = lambda cols: pl.BlockSpec((1, tm, cols), lambda b, j: (b, j, 0))
    out_specs = (fm(w), tk(w), fm(w), fm(w), fm(16), tk(2 * LANE), fm(DIFF_WIDTH), tk(DIFF_WIDTH),
                 pl.BlockSpec((1, 1, N_DIFF_HEADS, DIFF_V_DIM + VAUG, tm), lambda b, j: (b, j, 0, 0, 0)),
                 tk(CONV_WIDTH))
    in_specs = [
        pl.BlockSpec((1, tm, D_MODEL), lambda b, j: (b, j, 0)),
        pl.BlockSpec((1, MOD_ROWS, 6 * D_MODEL), lambda b, j: (layer, 0, 0)),
        _const_spec((1, D_MODEL)),
        pl.BlockSpec((64, tm), lambda b, j: (0, j)),
        pl.BlockSpec((tm, 2 * LANE), lambda b, j: (j, 0)),
    ] + [_const_spec(a.shape) for a in wts]
    return pl.pallas_call(
        kern, grid=(bsz, nt), in_specs=in_specs, out_specs=out_specs, out_shape=out_shapes,
        compiler_params=_cparams(("parallel", "parallel")), name="inproj",
    )(xa, mod, g1, ropeT, ropeR, *wts)


def _mlstm_kernel(qf_ref, kf_ref, vf_ref, gTf_ref, gf_ref, qb_ref, kb_ref, vb_ref, gTb_ref, gb_ref,
                  hf_o, hb_o, c_scr, m_scr, *, chunk, nb):
    step = pl.program_id(1)
    L = chunk
    hd = MLSTM_HEAD_DIM
    nh = N_MLSTM_HEADS

    @pl.when(step == 0)
    def _():
        c_scr[...] = jnp.zeros(c_scr.shape, F32)
        m_scr[...] = jnp.full(m_scr.shape, M_INIT, F32)

    si = lax.broadcasted_iota(jnp.int32, (L, L), 0)
    ti = lax.broadcasted_iota(jnp.int32, (L, L), 1)
    ones_rows = jnp.ones((VAUG, L), BF16)
    zero64 = jnp.zeros((hd, L), BF16)

    dirs = ((qf_ref, kf_ref, vf_ref, gTf_ref, gf_ref, hf_o), (qb_ref, kb_ref, vb_ref, gTb_ref, gb_ref, hb_o))
    masks = []
    for d in range(2):
        causal = (si <= ti) if d == 0 else (si >= ti)
        tri = jnp.where(causal, 1.0, 0.0).astype(F32)
        masks.append((causal, tri, tri.T))

    chains = []
    for bb, d in [(bb, d) for bb in range(nb) for d in range(2)]:
        qT_ref, k_ref, vT_ref, gT_ref, g_ref, h_o = dirs[d]
        causal, tri, tri_t = masks[d]
        gT = gT_ref[bb]
        gc = g_ref[bb]
        b_rows = jnp.dot(gT, tri, preferred_element_type=F32, precision=HIGHEST)
        b_cols = jnp.dot(tri_t, gc, preferred_element_type=F32, precision=HIGHEST)
        totals = jnp.sum(gT, axis=1, keepdims=True)
        for hh in range(nh):
            pair, half = hh // 2, hh % 2
            q_h = qT_ref[bb, hh * hd:(hh + 1) * hd, :]
            q_msk = jnp.concatenate([q_h, zero64] if half == 0 else [zero64, q_h], axis=0)
            k_pair = k_ref[bb, :, pair * 2 * hd:(pair + 1) * 2 * hd]
            v_aug = jnp.concatenate([vT_ref[bb, hh * hd:(hh + 1) * hd, :], ones_rows], axis=0)
            idx = (bb * 2 + d) * nh + hh
            c_st = c_scr[idx]
            sT = jnp.dot(k_pair, q_msk, preferred_element_type=F32)
            cq = jnp.dot(c_st.astype(BF16), q_msk, preferred_element_type=F32)
            chains.append(dict(idx=idx, hh=hh, bb=bb, h_o=h_o, causal=causal, k_pair=k_pair, v_aug=v_aug,
                               c_st=c_st, sT=sT, cq=cq, li_row=gT[hh:hh + 1], b_row=b_rows[4 + hh:5 + hh],
                               total=totals[4 + hh:5 + hh],
                               a_col=gc[:, hh:hh + 1] - b_cols[:, 4 + hh:5 + hh]))

    for ch in chains:
        hh, b_row = ch["hh"], ch["b_row"]
        m_st = m_scr[ch["idx"], 0:1, :]
        dmat = jnp.where(ch["causal"], b_row + ch["a_col"], -jnp.inf)
        inter = b_row + m_st
        m_t = jnp.maximum(inter, jnp.max(dmat, axis=0, keepdims=True))
        wT = jnp.exp(dmat - m_t) * ch["sT"]
        e_inter = jnp.exp(inter - m_t)
        intra = jnp.dot(ch["v_aug"], wT.astype(BF16), preferred_element_type=F32)
        cq = ch["cq"]
        num = e_inter * cq[0:hd] + intra[0:hd]
        den = e_inter * cq[hd:hd + 1] + jnp.sum(wT, axis=0, keepdims=True)
        ch["h_o"][ch["bb"], hh * hd:(hh + 1) * hd, :] = num / jnp.maximum(jnp.abs(den), jnp.exp(-m_t))

        total = ch["total"]
        g_row = total - b_row + ch["li_row"]
        m_prev = m_st[:, 0:1]
        m_new = jnp.maximum(total + m_prev, jnp.max(g_row, axis=1, keepdims=True))
        e_old = jnp.exp(total + m_prev - m_new)
        e_g = jnp.exp(g_row - m_new)
        upd = jnp.dot((ch["v_aug"].astype(F32) * e_g).astype(BF16), ch["k_pair"],
                      preferred_element_type=F32)
        c_scr[ch["idx"]] = e_old * ch["c_st"] + upd
        m_scr[ch["idx"]] = jnp.broadcast_to(m_new, m_scr.shape[1:])


def _mlstm(qmT, km, vmT, gT, g, *, seq):
    bsz, w, t_all = qmT.shape
    L = MLSTM_CHUNK
    nlat = seq // L
    nch = t_all // L
    nctx = nch - nlat
    fwd = lambda i: jnp.where(i < nctx, nlat + i, i - nctx)
    bwd = lambda i: nch - 1 - i

    nb = MLSTM_BATCH if bsz % MLSTM_BATCH == 0 else 1

    def specs(chunk_of, d):
        return [pl.BlockSpec((nb, w, L), lambda b, i: (b, 0, chunk_of(i))),
                pl.BlockSpec((nb, L, w), lambda b, i: (b, chunk_of(i), 0)),
                pl.BlockSpec((nb, w, L), lambda b, i: (b, 0, chunk_of(i))),
                pl.BlockSpec((nb, 8, L), lambda b, i: (b, d, chunk_of(i))),
                pl.BlockSpec((nb, L, LANE), lambda b, i: (b, chunk_of(i), d))]

    kern = functools.partial(_mlstm_kernel, chunk=L, nb=nb)
    out = jax.ShapeDtypeStruct((bsz, w, t_all), F32)
    nchain = 2 * nb * N_MLSTM_HEADS
    return pl.pallas_call(
        kern, grid=(bsz // nb, nch),
        in_specs=specs(fwd, 0) + specs(bwd, 1),
        out_specs=(pl.BlockSpec((nb, w, L), lambda b, i: (b, 0, fwd(i))),
                   pl.BlockSpec((nb, w, L), lambda b, i: (b, 0, bwd(i)))),
        out_shape=(out, out),
        scratch_shapes=[pltpu.VMEM((nchain, MLSTM_HEAD_DIM + VAUG, LANE), F32),
                        pltpu.VMEM((nchain, 8, L), F32)],
        compiler_params=_cparams(("parallel", "arbitrary")), name="mlstm_scan",
    )(qmT, km, vmT, gT, g, qmT, km, vmT, gT, g)


def _attn_kernel(lam_ref, gs_ref, qT_ref, k_ref, vT_ref, *rest, nch, tk):
    o_ref = rest[-1]
    tq = qT_ref.shape[2]
    qT = qT_ref[0]
    z = jnp.zeros((DIFF_QK_DIM, tq), BF16)
    rhs = jnp.concatenate([jnp.concatenate([qT[:DIFF_QK_DIM], z], axis=0),
                           jnp.concatenate([z, qT[DIFF_QK_DIM:]], axis=0)], axis=1)
    ncb = 2 * tq // ATT_COLS
    ms = [jnp.full((1, ATT_COLS), NEG_BIG, F32) for _ in range(ncb)]
    accs = [jnp.zeros((DIFF_V_DIM + VAUG, ATT_COLS), F32) for _ in range(ncb)]
    sub = tk // ATT_KEYS
    units = [(c, cb) for c in range(nch * sub) for cb in range(ncb)]

    def scores(c, cb):
        return jnp.dot(k_ref[0, c * ATT_KEYS:(c + 1) * ATT_KEYS, :],
                       rhs[:, cb * ATT_COLS:(cb + 1) * ATT_COLS], preferred_element_type=F32)

    pending = [scores(*u) for u in units[:ATT_AHEAD]]
    for i, (c, cb) in enumerate(units):
        sT = pending.pop(0)
        if i + ATT_AHEAD < len(units):
            pending.append(scores(*units[i + ATT_AHEAD]))
        vT = vT_ref[0, c // sub, 0, :, (c % sub) * ATT_KEYS:(c % sub + 1) * ATT_KEYS]
        m_new = jnp.maximum(ms[cb], jnp.max(sT, axis=0, keepdims=True))
        p = jnp.exp2((sT - m_new).astype(BF16))
        alpha = jnp.exp2(ms[cb] - m_new)
        accs[cb] = alpha * accs[cb] + jnp.dot(vT, p, preferred_element_type=F32)
        ms[cb] = m_new
    acc = jnp.concatenate(accs, axis=1)
    l = acc[DIFF_V_DIM:DIFF_V_DIM + 1]
    acc = acc[0:DIFF_V_DIM]

    lv = lam_ref[...]
    lam_init = lv[4:5, 0:1]
    lam = (jnp.exp(jnp.sum(lv[0:1] * lv[1:2], axis=1, keepdims=True))
           - jnp.exp(jnp.sum(lv[2:3] * lv[3:4], axis=1, keepdims=True)) + lam_init)
    oT = acc[:, :tq] / l[:, :tq] - lam * (acc[:, tq:] / l[:, tq:])
    oT = oT * lax.rsqrt(jnp.mean(oT * oT, axis=0, keepdims=True) + EPS) * gs_ref[...] * (1.0 - lam_init)
    o_ref[0] = oT.T.astype(BF16)


def _attention(lamv, gs_col, qaT, ka, vaT, d_prev, out_rows, *, q_tile, q_blk0, n_q, k_rows, k_blk0,
               v_chunks, v_chunk0, v_cols, v_blk0):
    bsz = qaT.shape[0]
    kern = functools.partial(_attn_kernel, nch=v_chunks, tk=v_cols)
    in_specs = [_const_spec(lamv.shape), _const_spec((DIFF_V_DIM, 1)),
                pl.BlockSpec((1, DIFF_V_DIM, q_tile), lambda b, h, i: (b, h, q_blk0 + i)),
                pl.BlockSpec((1, k_rows, DIFF_V_DIM), lambda b, h, i: (b, k_blk0, h)),
                pl.BlockSpec((1, v_chunks, 1, DIFF_V_DIM + VAUG, v_cols),
                             lambda b, h, i: (b, v_chunk0, h, 0, v_blk0))]
    args = [lamv, gs_col, qaT, ka, vaT]
    aliases = {}
    if d_prev is not None:
        in_specs.append(pl.BlockSpec(memory_space=pl.ANY))
        aliases = {len(args): 0}
        args.append(d_prev)
    return pl.pallas_call(
        kern, grid=(bsz, N_DIFF_HEADS, n_q), in_specs=in_specs,
        out_specs=pl.BlockSpec((1, q_tile, DIFF_V_DIM), lambda b, h, i: (b, q_blk0 + i, h)),
        out_shape=jax.ShapeDtypeStruct((bsz, out_rows, DIFF_WIDTH), BF16),
        input_output_aliases=aliases,
        compiler_params=_cparams(("parallel", "parallel", "arbitrary")), name="diff_attn",
    )(*args)


def _conv_kernel(l_ref, c_ref, r_ref, w_ref, b_ref, g_ref, bb_ref, o_ref, buf, shifted, *, seq, tc, nt):
    j = pl.program_id(1)
    start = j * tc
    lvalid = jnp.logical_and(j > 0, start != seq)
    rvalid = jnp.logical_and(j < nt - 1, start + tc != seq)
    hl = CONV_HALO
    buf[0:hl, :] = jnp.where(lvalid, l_ref[0].astype(F32), 0.0)
    buf[hl:hl + tc, :] = c_ref[0].astype(F32)
    buf[hl + tc:2 * hl + tc, :] = jnp.where(rvalid, r_ref[0].astype(F32), 0.0)
    pad = CONV_KERNEL // 2
    sub = 128
    sl = CONV_SHIFTS
    for r in range(sl):
        shifted[r] = buf[pl.ds(r, shifted.shape[1]), :]
    for r0 in range(0, tc, sub):
        accs = [jnp.zeros((sub, LANE), F32) for _ in range(CONV_WIDTH // LANE)]
        for t in range(CONV_KERNEL):
            q, r = divmod(hl - pad + t, sl)
            for cb in range(CONV_WIDTH // LANE):
                cols = slice(cb * LANE, (cb + 1) * LANE)
                accs[cb] = accs[cb] + w_ref[t:t + 1, cols] * shifted[r, pl.ds(r0 + sl * q, sub), cols]
        y = jnp.concatenate(accs, axis=1) + b_ref[...]
        mu = jnp.mean(y, axis=-1, keepdims=True)
        var = jnp.mean(jnp.square(y - mu), axis=-1, keepdims=True)
        z = (y - mu) * lax.rsqrt(var + EPS) * g_ref[...] + bb_ref[...]
        o_ref[0, r0:r0 + sub, :] = (z * _sigmoid(z)).astype(BF16)


def _conv(u, w_dw, b_dw, g_ln, b_ln, *, seq):
    bsz, t_all, cw = u.shape
    tc = CONV_TILE
    nt = t_all // tc
    r = tc // CONV_HALO
    nhalo = t_all // CONV_HALO
    kern = functools.partial(_conv_kernel, seq=seq, tc=tc, nt=nt)
    row = lambda a: a.reshape(1, cw)
    return pl.pallas_call(
        kern, grid=(bsz, nt),
        in_specs=[pl.BlockSpec((1, CONV_HALO, cw), lambda b, j: (b, jnp.maximum(j * r - 1, 0), 0)),
                  pl.BlockSpec((1, tc, cw), lambda b, j: (b, j, 0)),
                  pl.BlockSpec((1, CONV_HALO, cw), lambda b, j: (b, jnp.minimum((j + 1) * r, nhalo - 1), 0)),
                  _const_spec((CONV_KERNEL, cw)), _const_spec((1, cw)), _const_spec((1, cw)),
                  _const_spec((1, cw))],
        out_specs=pl.BlockSpec((1, tc, cw), lambda b, j: (b, j, 0)),
        out_shape=jax.ShapeDtypeStruct((bsz, t_all, cw), BF16),
        scratch_shapes=[pltpu.VMEM((tc + 2 * CONV_HALO, cw), F32),
                        pltpu.VMEM((CONV_SHIFTS, tc + 2 * CONV_HALO - CONV_SHIFTS, cw), F32)],
        compiler_params=_cparams(("parallel", "parallel")), name="conv_mixer",
    )(u, u, u, w_dw, row(b_dw), row(g_ln), row(b_ln))


def _mixout_kernel(x_ref, mod_ref, hf_ref, hb_ref, omT_ref, gm_ref, d_ref, c_ref, wm_ref, wd_ref, wc_ref, o_ref,
                   *, seq, tm, ctx_row):
    b = pl.program_id(0)
    j = pl.program_id(1)
    is_ctx = _is_ctx_rows(j, tm, seq)
    hT = hf_ref[0] + hb_ref[0]
    h4 = hT.reshape(N_MLSTM_HEADS, MLSTM_HEAD_DIM, tm)
    mu = jnp.mean(h4, axis=1, keepdims=True)
    var = jnp.mean(jnp.square(h4 - mu), axis=1, keepdims=True)
    hn = ((h4 - mu) * lax.rsqrt(var + EPS)).reshape(MLSTM_WIDTH, tm)
    mT = _sigmoid(omT_ref[0].astype(F32)) * hn * gm_ref[...]
    m = mT.T.astype(BF16)
    y = (jnp.dot(m, wm_ref[...], preferred_element_type=F32)
         + jnp.dot(d_ref[0], wd_ref[...], preferred_element_type=F32)
         + jnp.dot(c_ref[0], wc_ref[...], preferred_element_type=F32))
    g_l, g_c = _mod_rows(mod_ref, b, ctx_row, 2)
    o_ref[0] = x_ref[0] + jnp.where(is_ctx, g_c, g_l) * y


def _mixout(xa, mod, layer, hTf, hTb, omT, gm_col, d, cx, wm, wd, wc, *, seq):
    bsz, t_all, _ = xa.shape
    tm = TOK_TILE
    nt = t_all // tm
    kern = functools.partial(_mixout_kernel, seq=seq, tm=tm, ctx_row=bsz)
    w = MLSTM_WIDTH
    return pl.pallas_call(
        kern, grid=(bsz, nt),
        in_specs=[pl.BlockSpec((1, tm, D_MODEL), lambda b, j: (b, j, 0)),
                  pl.BlockSpec((1, MOD_ROWS, 6 * D_MODEL), lambda b, j: (layer, 0, 0)),
                  pl.BlockSpec((1, w, tm), lambda b, j: (b, 0, j)),
                  pl.BlockSpec((1, w, tm), lambda b, j: (b, 0, j)),
                  pl.BlockSpec((1, w, tm), lambda b, j: (b, 0, j)),
                  _const_spec((w, 1)),
                  pl.BlockSpec((1, tm, DIFF_WIDTH), lambda b, j: (b, j, 0)),
                  pl.BlockSpec((1, tm, CONV_WIDTH), lambda b, j: (b, j, 0)),
                  _const_spec(wm.shape), _const_spec(wd.shape), _const_spec(wc.shape)],
        out_specs=pl.BlockSpec((1, tm, D_MODEL), lambda b, j: (b, j, 0)),
        out_shape=jax.ShapeDtypeStruct(xa.shape, F32),
        compiler_params=_cparams(("parallel", "parallel")), name="mix_out",
    )(xa, mod, hTf, hTb, omT, gm_col, d, cx, wm, wd, wc)


def _swiglu_partial(hb, wg, wu, wd):
    a = jnp.dot(hb, wg, preferred_element_type=F32)
    u = jnp.dot(hb, wu, preferred_element_type=F32)
    t = (a * _sigmoid(a) * u).astype(BF16)
    return jnp.dot(t, wd, preferred_element_type=F32)


def _ffn_kernel(x_ref, mod_ref, g2_ref, wg_ref, wu_ref, wd_ref, o_ref, hb_scr, acc_scr,
                *, seq, tm, ctx_row, nf):
    b = pl.program_id(0)
    j = pl.program_id(1)
    f = pl.program_id(2)
    is_ctx = _is_ctx_rows(j, tm, seq)

    @pl.when(f == 0)
    def _():
        h = _rms_mod(x_ref[0], g2_ref[...], mod_ref, b, ctx_row, is_ctx, 3, 4)
        hb_scr[...] = h.astype(BF16)
        acc_scr[...] = jnp.zeros(acc_scr.shape, F32)

    acc_scr[...] += _swiglu_partial(hb_scr[...], wg_ref[0], wu_ref[0], wd_ref[0])

    @pl.when(f == nf - 1)
    def _():
        g_l, g_c = _mod_rows(mod_ref, b, ctx_row, 5)
        o_ref[0] = x_ref[0] + jnp.where(is_ctx, g_c, g_l) * acc_scr[...]


def _ffn(xa, mod, layer, g2, wg, wu, wd, ffn_layer, *, seq):
    bsz, t_all, _ = xa.shape
    tm = TOK_TILE
    nt = t_all // tm
    nf = FFN_SPLIT
    tf = wg.shape[2] // nf
    assert tf % LANE == 0 and tf * nf == wg.shape[2]
    kern = functools.partial(_ffn_kernel, seq=seq, tm=tm, ctx_row=bsz, nf=nf)
    return pl.pallas_call(
        kern, grid=(bsz, nt, nf),
        in_specs=[pl.BlockSpec((1, tm, D_MODEL), lambda b, j, f: (b, j, 0)),
                  pl.BlockSpec((1, MOD_ROWS, 6 * D_MODEL), lambda b, j, f: (layer, 0, 0)),
                  _const_spec((1, D_MODEL)),
                  pl.BlockSpec((1, D_MODEL, tf), lambda b, j, f: (ffn_layer, 0, f)),
                  pl.BlockSpec((1, D_MODEL, tf), lambda b, j, f: (ffn_layer, 0, f)),
                  pl.BlockSpec((1, tf, D_MODEL), lambda b, j, f: (ffn_layer, f, 0))],
        out_specs=pl.BlockSpec((1, tm, D_MODEL), lambda b, j, f: (b, j, 0)),
        out_shape=jax.ShapeDtypeStruct(xa.shape, F32),
        scratch_shapes=[pltpu.VMEM((tm, D_MODEL), BF16), pltpu.VMEM((tm, D_MODEL), F32)],
        compiler_params=_cparams(("parallel", "parallel", "arbitrary")), name="ffn_swiglu",
    )(xa, mod, g2, wg, wu, wd)


def _moe_kernel(x_ref, mod_ref, g2_ref, wr_ref, br_ref, wg_ref, wu_ref, wd_ref, o_ref,
                hb_scr, acc_scr, comb_scr, slot_scr, slotT_scr, *, seq, tm, nt, ctx_row, ne, group):
    step = pl.program_id(0)
    e = pl.program_id(1)
    lane = lax.broadcasted_iota(jnp.int32, (tm, LANE), 1)
    sb = MOE_SUB
    row_i = lax.broadcasted_iota(jnp.int32, (sb, 1), 0).astype(F32)
    col_i = lax.broadcasted_iota(jnp.int32, (1, sb), 1).astype(F32)

    for half in range(group):
        tile = step * group + half
        b = tile // nt
        is_ctx = _is_ctx_rows(tile % nt, tm, seq)
        rows = slice(half * tm, (half + 1) * tm)

        @pl.when(e == 0)
        def _():
            h = _rms_mod(x_ref[rows, :], g2_ref[...], mod_ref, b, ctx_row, is_ctx, 3, 4)
            hb_scr[half] = h.astype(BF16)
            acc_scr[half] = jnp.zeros((tm, D_MODEL), F32)
            logits = jnp.dot(h, wr_ref[...], preferred_element_type=F32, precision=HIGHEST) + br_ref[...]
            logits = jnp.where(lane < ne, logits, -jnp.inf)
            ex = jnp.exp(logits - jnp.max(logits, axis=-1, keepdims=True))
            probs = ex / jnp.sum(ex, axis=-1, keepdims=True)
            v1 = jnp.max(probs, axis=-1, keepdims=True)
            i1 = jnp.min(jnp.where(probs == v1, lane, LANE), axis=-1, keepdims=True)
            rest = jnp.where(lane == i1, -1.0, probs)
            v2 = jnp.max(rest, axis=-1, keepdims=True)
            i2 = jnp.min(jnp.where(rest == v2, lane, LANE), axis=-1, keepdims=True)
            tot = v1 + v2
            comb_scr[half] = jnp.where(lane == i1, v1 / tot, 0.0) + jnp.where(lane == i2, v2 / tot, 0.0)
            sel = jnp.logical_or(lane == i1, lane == i2)
            ri = lax.broadcasted_iota(jnp.int32, (tm, tm), 0)
            ci = lax.broadcasted_iota(jnp.int32, (tm, tm), 1)
            before = jnp.where(ci < ri, 1.0, 0.0).astype(BF16)
            rank = jnp.dot(before, jnp.where(sel, 1.0, 0.0).astype(BF16), preferred_element_type=F32)
            slot = jnp.where(sel, rank, -1.0)
            slot_scr[half] = slot
            slotT_scr[half] = slot.T

        onlane = lane == e
        cw = jnp.sum(jnp.where(onlane, comb_scr[half], 0.0), axis=-1, keepdims=True)
        slot_c = jnp.max(jnp.where(onlane, slot_scr[half], -1.0), axis=-1, keepdims=True)
        slot_r = slotT_scr[half, pl.ds(e, 1), :]
        count = (jnp.max(slot_r) + 1.0).astype(jnp.int32)

        def sub_block(i, carry):
            base = (i * sb).astype(F32)
            gather = jnp.where(slot_r == base + row_i, 1.0, 0.0).astype(BF16)
            xs = jnp.dot(gather, hb_scr[half], preferred_element_type=F32).astype(BF16)
            y = _swiglu_partial(xs, wg_ref[0, 0], wu_ref[0, 0], wd_ref[0, 0])
            scatter = jnp.where(slot_c == base + col_i, 1.0, 0.0).astype(BF16)
            acc_scr[half] += cw * jnp.dot(scatter, y.astype(BF16), preferred_element_type=F32)
            return carry

        lax.fori_loop(0, (count + sb - 1) // sb, sub_block, 0)

        @pl.when(e == ne - 1)
        def _():
            g_l, g_c = _mod_rows(mod_ref, b, ctx_row, 5)
            o_ref[rows, :] = x_ref[rows, :] + jnp.where(is_ctx, g_c, g_l) * acc_scr[half]


def _moe(xa, mod, layer, g2, wr, br, wg, wu, wd, moe_layer, *, seq):
    bsz, t_all, _ = xa.shape
    tm = TOK_TILE
    nt = t_all // tm
    group = MOE_GROUP
    assert (bsz * nt) % group == 0
    ne, fe = wg.shape[1], wg.shape[3]
    kern = functools.partial(_moe_kernel, seq=seq, tm=tm, nt=nt, ctx_row=bsz, ne=ne, group=group)
    out = pl.pallas_call(
        kern, grid=(bsz * nt // group, ne),
        in_specs=[pl.BlockSpec((group * tm, D_MODEL), lambda s, e: (s, 0)),
                  pl.BlockSpec((1, MOD_ROWS, 6 * D_MODEL), lambda s, e: (layer, 0, 0)),
                  _const_spec((1, D_MODEL)), _const_spec(wr.shape), _const_spec(br.shape),
                  pl.BlockSpec((1, 1, D_MODEL, fe), lambda s, e: (moe_layer, e, 0, 0)),
                  pl.BlockSpec((1, 1, D_MODEL, fe), lambda s, e: (moe_layer, e, 0, 0)),
                  pl.BlockSpec((1, 1, fe, D_MODEL), lambda s, e: (moe_layer, e, 0, 0))],
        out_specs=pl.BlockSpec((group * tm, D_MODEL), lambda s, e: (s, 0)),
        out_shape=jax.ShapeDtypeStruct((bsz * t_all, D_MODEL), F32),
        scratch_shapes=[pltpu.VMEM((group, tm, D_MODEL), BF16), pltpu.VMEM((group, tm, D_MODEL), F32),
                        pltpu.VMEM((group, tm, LANE), F32), pltpu.VMEM((group, tm, LANE), F32),
                        pltpu.VMEM((group, LANE, tm), F32)],
        compiler_params=_cparams(("parallel", "arbitrary")), name="moe_swiglu",
    )(xa.reshape(bsz * t_all, D_MODEL), mod, g2, wr, br, wg, wu, wd)
    return out.reshape(xa.shape)


def _final_kernel(x_ref, g_ref, o_ref):
    x = x_ref[0]
    o_ref[0] = x * lax.rsqrt(jnp.mean(x * x, axis=-1, keepdims=True) + EPS) * g_ref[...]


def _final_norm(xa, g, *, seq):
    bsz = xa.shape[0]
    tf = 512
    return pl.pallas_call(
        _final_kernel, grid=(bsz, seq // tf),
        in_specs=[pl.BlockSpec((1, tf, D_MODEL), lambda b, j: (b, j, 0)), _const_spec((1, D_MODEL))],
        out_specs=pl.BlockSpec((1, tf, D_MODEL), lambda b, j: (b, j, 0)),
        out_shape=jax.ShapeDtypeStruct((bsz, seq, D_MODEL), F32),
        compiler_params=_cparams(("parallel", "parallel")), name="final_norm",
    )(xa, g.reshape(1, D_MODEL))


def _rope_tables(seq, t_all):
    pos = np.arange(seq)
    per_axis = DIFF_QK_DIM // 2
    inv = (ROPE_BASE ** (-np.arange(0, per_axis, 2, dtype=np.float32) / per_axis)).astype(np.float32)
    rowp = (pos // GRID_W).astype(np.float32)
    colp = (pos % GRID_W).astype(np.float32)
    ang = np.stack([rowp[:, None] * inv, colp[:, None] * inv], axis=1).astype(np.float64)
    cos = np.concatenate([np.cos(ang), np.ones((t_all - seq, 2, 16))], axis=0)
    sin = np.concatenate([np.sin(ang), np.zeros((t_all - seq, 2, 16))], axis=0)
    qscale = (DIFF_QK_DIM ** -0.5) * LOG2E
    ropeT = np.concatenate([cos.reshape(t_all, 32).T, sin.reshape(t_all, 32).T], axis=0) * qscale
    cos64 = np.concatenate([cos[:, 0], cos[:, 0], cos[:, 1], cos[:, 1]], axis=-1)
    sin64 = np.concatenate([-sin[:, 0], sin[:, 0], -sin[:, 1], sin[:, 1]], axis=-1)
    ropeR = np.concatenate([cos64, cos64, sin64, sin64], axis=-1)
    return jnp.asarray(ropeT, F32), jnp.asarray(ropeR, F32)


def _swap_perm():
    idx = np.arange(2 * N_DIFF_HEADS * DIFF_QK_DIM)
    return np.where((idx % 32) < 16, idx + 16, idx - 16)


def _prep_inproj_weights(w_in_l, b_gates_l):
    offs = np.cumsum((0,) + IN_SPLITS)
    col = lambda i: w_in_l[:, offs[i]:offs[i + 1]]
    mq, mk, mv, mo, gt, aq, ak, av, cv = (col(i) for i in range(9))
    nh = N_MLSTM_HEADS
    wTm = jnp.concatenate([mq, mv, mo], axis=1).T.astype(BF16)
    wTg = gt.T.astype(BF16)
    bgT = b_gates_l.reshape(4 * nh, 1).astype(F32)
    wkm = (mk * (MLSTM_HEAD_DIM ** -0.5)).astype(BF16)
    zpad = jnp.zeros((D_MODEL, LANE - 2 * nh), F32)
    wg = jnp.concatenate([gt[:, 0:2 * nh], zpad, gt[:, 2 * nh:4 * nh], zpad], axis=1).astype(BF16)
    bpad = jnp.zeros((LANE - 2 * nh,), F32)
    bg = jnp.concatenate([b_gates_l[0:2 * nh], bpad, b_gates_l[2 * nh:4 * nh], bpad]).reshape(1, 2 * LANE)
    wTaq = aq.T.astype(BF16)
    wak = jnp.concatenate([ak, ak[:, _swap_perm()]], axis=1).astype(BF16)
    wTav = av.T.astype(BF16)
    wcv = cv.astype(BF16)
    return (wTm, wTg, bgT, wkm, wg, bg, wTaq, wak, wTav, wcv)


def kernel(x, c, ctx, c_ctx, w_mod, b_mod, g_norm1, w_in, b_gates, g_mlstm, lambda_q1, lambda_k1,
           lambda_q2, lambda_k2, g_subln, w_dw, b_dw, g_conv_ln, b_conv_ln, w_out, g_norm2,
           w_ffn_gate, w_ffn_up, w_ffn_down, w_router, b_router, w_exp_gate, w_exp_up, w_exp_down,
           g_final):
    bsz, seq, _ = x.shape
    nctx = ctx.shape[1]
    t_all = seq + nctx
    depth = w_mod.shape[0]
    assert nctx == CTX_LEN == MLSTM_CHUNK and bsz + 1 <= MOD_ROWS
    assert t_all % TOK_TILE == 0 and seq % Q_TILE == 0 and seq % CONV_TILE == 0 and seq % GRID_W == 0

    xa = jnp.concatenate([x, ctx], axis=1)
    cond = jnp.concatenate([c, c_ctx[None, :], jnp.zeros((MOD_ROWS - bsz - 1, D_MODEL), F32)], axis=0)
    mod = _mod_table(cond, w_mod, b_mod)
    ropeT, ropeR = _rope_tables(seq, t_all)
    nkc = t_all // TOK_TILE
    weg, weu, wed = w_exp_gate.astype(BF16), w_exp_up.astype(BF16), w_exp_down.astype(BF16)
    wfg, wfu, wfd = w_ffn_gate.astype(BF16), w_ffn_up.astype(BF16), w_ffn_down.astype(BF16)

    for l in range(depth):
        wts = _prep_inproj_weights(w_in[l], b_gates[l])
        (qmT, km, vmT, omT, gT, g, qaT, ka, vaT, u) = _inproj(
            xa, mod, l, g_norm1[l].reshape(1, D_MODEL), ropeT, ropeR, wts, seq=seq)

        hTf, hTb = _mlstm(qmT, km, vmT, gT, g, seq=seq)

        lam_init = 0.8 - 0.6 * math.exp(-0.3 * l)
        lamv = jnp.zeros((8, LANE), F32).at[0:4, 0:DIFF_QK_DIM].set(
            jnp.stack([lambda_q1[l], lambda_k1[l], lambda_q2[l], lambda_k2[l]]).astype(F32))
        lamv = lamv.at[4, :].set(lam_init)
        gs_col = g_subln[l].reshape(DIFF_V_DIM, 1).astype(F32)
        d = _attention(lamv, gs_col, qaT, ka, vaT, None, t_all, q_tile=Q_TILE, q_blk0=0,
                       n_q=seq // Q_TILE, k_rows=t_all, k_blk0=0, v_chunks=nkc, v_chunk0=0,
                       v_cols=TOK_TILE, v_blk0=0)
        d = _attention(lamv, gs_col, qaT, ka, vaT, d, t_all, q_tile=nctx, q_blk0=seq // nctx, n_q=1,
                       k_rows=nctx, k_blk0=seq // nctx, v_chunks=1, v_chunk0=nkc - 1,
                       v_cols=nctx, v_blk0=TOK_TILE // nctx - 1)

        cx = _conv(u, w_dw[l], b_dw[l], g_conv_ln[l], b_conv_ln[l], seq=seq)

        wo = w_out[l].astype(BF16)
        xa = _mixout(xa, mod, l, hTf, hTb, omT, g_mlstm[l].reshape(MLSTM_WIDTH, 1).astype(F32), d, cx,
                     wo[0:MLSTM_WIDTH], wo[MLSTM_WIDTH:MLSTM_WIDTH + DIFF_WIDTH],
                     wo[MLSTM_WIDTH + DIFF_WIDTH:], seq=seq)

        jj = l // 2
        g2 = g_norm2[l].reshape(1, D_MODEL)
        if l % 2 == 0:
            xa = _ffn(xa, mod, l, g2, wfg, wfu, wfd, jj, seq=seq)
        else:
            wr = jnp.concatenate([w_router[jj], jnp.zeros((D_MODEL, LANE - N_EXPERTS), F32)], axis=1)
            br = jnp.concatenate([b_router[jj], jnp.zeros((LANE - N_EXPERTS,), F32)]).reshape(1, LANE)
            xa = _moe(xa, mod, l, g2, wr, br, weg, weu, wed, jj, seq=seq)

    return _final_norm(xa, g_final, seq=seq)
```

```python
import functools
import math

import jax
import jax.numpy as jnp
import numpy as np
from jax import lax
from jax.experimental import pallas as pl
from jax.experimental.pallas import tpu as pltpu

F32 = jnp.float32
BF16 = jnp.bfloat16
HIGHEST = lax.Precision.HIGHEST

D_MODEL = 1024
DEPTH = 4
GRID_W = 64
CTX_LEN = 256
N_MLSTM_HEADS = 4
MLSTM_HEAD_DIM = 64
MLSTM_WIDTH = N_MLSTM_HEADS * MLSTM_HEAD_DIM
N_DIFF_HEADS = 4
DIFF_QK_DIM = 64
DIFF_V_DIM = 2 * DIFF_QK_DIM
DIFF_WIDTH = N_DIFF_HEADS * DIFF_V_DIM
ROPE_BASE = 10000.0
CONV_WIDTH = 256
CONV_KERNEL = 31
IN_SPLITS = (MLSTM_WIDTH, MLSTM_WIDTH, MLSTM_WIDTH, MLSTM_WIDTH, 4 * N_MLSTM_HEADS,
             2 * N_DIFF_HEADS * DIFF_QK_DIM, 2 * N_DIFF_HEADS * DIFF_QK_DIM, DIFF_WIDTH,
             2 * CONV_WIDTH)
D_FF = 2816
N_EXPERTS = 8
D_FF_EXPERT = 1408
EPS = 1e-6
M_INIT = -1e30
NEG_BIG = -1e30
LOG2E = 1.4426950408889634

LANE = 128
V7X_VMEM_LIMIT = 56 * 1024 * 1024
TOK_TILE = 768
MLSTM_CHUNK = 256
CONV_TILE = 256
CONV_HALO = 16
CONV_SHIFTS = 8
Q_TILE = 1024
MOD_ROWS = 8
VAUG = 16
MOE_BLOCKS = (128, 192, 256, 320, 384)
FFN_SPLIT = 2
MOE_GROUP = 1
MLSTM_BATCH = 2
ATT_KEYS = 256
ATT_COLS = 256
ATT_AHEAD = 4


def _cparams(sem):
    return pltpu.CompilerParams(dimension_semantics=sem, vmem_limit_bytes=V7X_VMEM_LIMIT)


def _sigmoid(v):
    return 1.0 / (1.0 + jnp.exp(-v))


def _log_sigmoid(v):
    return jnp.minimum(v, 0.0) - jnp.log(1.0 + jnp.exp(-jnp.abs(v)))


def _mod_rows(mod_ref, b, ctx_row, k):
    lat = mod_ref[0, pl.ds(b, 1), k * D_MODEL:(k + 1) * D_MODEL]
    ctx = mod_ref[0, ctx_row:ctx_row + 1, k * D_MODEL:(k + 1) * D_MODEL]
    return lat, ctx


def _is_ctx_rows(j, tm, seq):
    rows = j * tm + lax.broadcasted_iota(jnp.int32, (tm, 1), 0)
    return rows >= seq


def _rms_mod(x, g, mod_ref, b, ctx_row, is_ctx, k_shift, k_scale):
    y = x * lax.rsqrt(jnp.mean(x * x, axis=-1, keepdims=True) + EPS) * g
    sh_l, sh_c = _mod_rows(mod_ref, b, ctx_row, k_shift)
    sc_l, sc_c = _mod_rows(mod_ref, b, ctx_row, k_scale)
    shift = jnp.where(is_ctx, sh_c, sh_l)
    scale = jnp.where(is_ctx, sc_c, sc_l)
    return y * (1.0 + scale) + shift


def _mod_kernel(cond_ref, w_ref, b_ref, o_ref):
    c = cond_ref[...]
    s = c * _sigmoid(c)
    o_ref[0] = jnp.dot(s, w_ref[0], preferred_element_type=F32, precision=HIGHEST) + b_ref[0]


def _mod_table(cond, w_mod, b_mod):
    depth = w_mod.shape[0]
    n = w_mod.shape[2] // D_MODEL
    return pl.pallas_call(
        _mod_kernel,
        grid=(depth, n),
        in_specs=[pl.BlockSpec((MOD_ROWS, D_MODEL), lambda l, c: (0, 0)),
                  pl.BlockSpec((1, D_MODEL, D_MODEL), lambda l, c: (l, 0, c)),
                  pl.BlockSpec((1, 1, D_MODEL), lambda l, c: (l, 0, c))],
        out_specs=pl.BlockSpec((1, MOD_ROWS, D_MODEL), lambda l, c: (l, 0, c)),
        out_shape=jax.ShapeDtypeStruct((depth, MOD_ROWS, n * D_MODEL), F32),
        compiler_params=_cparams(("parallel", "parallel")),
        name="mod_table",
    )(cond, w_mod, b_mod.reshape(depth, 1, n * D_MODEL))


def _inproj_kernel(x_ref, mod_ref, g1_ref, ropeT_ref, ropeR_ref,
                   wTm_ref, wTg_ref, bgT_ref, wkm_ref, wg_ref, bg_ref,
                   wTaq_ref, wak_ref, wTav_ref, wcv_ref,
                   qmT_o, km_o, vmT_o, omT_o, gT_o, g_o, qaT_o, ka_o, vaT_o, u_o,
                   *, seq, tm, ctx_row):
    b = pl.program_id(0)
    j = pl.program_id(1)
    is_ctx = _is_ctx_rows(j, tm, seq)
    h = _rms_mod(x_ref[0], g1_ref[...], mod_ref, b, ctx_row, is_ctx, 0, 1)
    hb = h.astype(BF16)
    hT = h.T.astype(BF16)

    mT = jnp.dot(wTm_ref[...], hT, preferred_element_type=F32)
    w = MLSTM_WIDTH
    qmT_o[0] = mT[0:w].astype(BF16)
    vmT_o[0] = mT[w:2 * w].astype(BF16)
    omT_o[0] = mT[2 * w:3 * w].astype(BF16)
    gT = jnp.dot(wTg_ref[...], hT, preferred_element_type=F32) + bgT_ref[...]
    rowi = lax.broadcasted_iota(jnp.int32, gT.shape, 0)
    gT_o[0] = jnp.where((rowi % 8) >= 4, _log_sigmoid(gT), gT)
    km_o[0] = jnp.dot(hb, wkm_ref[...], preferred_element_type=F32).astype(BF16)
    g = jnp.dot(hb, wg_ref[...], preferred_element_type=F32) + bg_ref[...]
    lanei = lax.broadcasted_iota(jnp.int32, g.shape, 1) % LANE
    g_o[0] = jnp.where((lanei >= 4) & (lanei < 8), _log_sigmoid(g), g)

    qT = jnp.dot(wTaq_ref[...], hT, preferred_element_type=F32)
    for grp in range(2 * N_DIFF_HEADS * 2):
        ax = grp % 2
        cos = ropeT_ref[ax * 16:(ax + 1) * 16, :]
        sin = ropeT_ref[32 + ax * 16:32 + (ax + 1) * 16, :]
        x1 = qT[grp * 32:grp * 32 + 16]
        x2 = qT[grp * 32 + 16:grp * 32 + 32]
        qaT_o[0, grp * 32:grp * 32 + 16, :] = (x1 * cos - x2 * sin).astype(BF16)
        qaT_o[0, grp * 32 + 16:grp * 32 + 32, :] = (x2 * cos + x1 * sin).astype(BF16)
    kk = jnp.dot(hb, wak_ref[...], preferred_element_type=F32)
    cosr = ropeR_ref[:, 0:LANE]
    sinr = ropeR_ref[:, LANE:2 * LANE]
    nk = 2 * N_DIFF_HEADS * DIFF_QK_DIM
    for sl in range(nk // LANE):
        k0 = kk[:, sl * LANE:(sl + 1) * LANE]
        k1 = kk[:, nk + sl * LANE:nk + (sl + 1) * LANE]
        ka_o[0, :, sl * LANE:(sl + 1) * LANE] = (k0 * cosr + k1 * sinr).astype(BF16)
    vT = jnp.dot(wTav_ref[...], hT, preferred_element_type=F32)
    for hh in range(N_DIFF_HEADS):
        vaT_o[0, 0, hh, 0:DIFF_V_DIM, :] = vT[hh * DIFF_V_DIM:(hh + 1) * DIFF_V_DIM].astype(BF16)
        vaT_o[0, 0, hh, DIFF_V_DIM:DIFF_V_DIM + VAUG, :] = jnp.ones((VAUG, tm), BF16)
    cv = jnp.dot(hb, wcv_ref[...], preferred_element_type=F32)
    u_o[0] = (cv[:, :CONV_WIDTH] * _sigmoid(cv[:, CONV_WIDTH:])).astype(BF16)


def _const_spec(shape):
    nd = len(shape)
    return pl.BlockSpec(shape, lambda *_: (0,) * nd)


def _inproj(xa, mod, layer, g1, ropeT, ropeR, wts, *, seq):
    bsz, t_all, _ = xa.shape
    tm = TOK_TILE
    nt = t_all // tm
    kern = functools.partial(_inproj_kernel, seq=seq, tm=tm, ctx_row=bsz)
    w = MLSTM_WIDTH
    out_shapes = (
        jax.ShapeDtypeStruct((bsz, w, t_all), BF16),
        jax.ShapeDtypeStruct((bsz, t_all, w), BF16),
        jax.ShapeDtypeStruct((bsz, w, t_all), BF16),
        jax.ShapeDtypeStruct((bsz, w, t_all), BF16),
        jax.ShapeDtypeStruct((bsz, 16, t_all), F32),
        jax.ShapeDtypeStruct((bsz, t_all, 2 * LANE), F32),
        jax.ShapeDtypeStruct((bsz, DIFF_WIDTH, t_all), BF16),
        jax.ShapeDtypeStruct((bsz, t_all, DIFF_WIDTH), BF16),
        jax.ShapeDtypeStruct((bsz, nt, N_DIFF_HEADS, DIFF_V_DIM + VAUG, tm), BF16),
        jax.ShapeDtypeStruct((bsz, t_all, CONV_WIDTH), BF16),
    )
    fm = lambda rows: pl.BlockSpec((1, rows, tm), lambda b, j: (b, 0, j))
    tk = lambda cols: pl.BlockSpec((1, tm, cols), lambda b, j: (b, j, 0))
    out_specs = (fm(w), tk(w), fm(w), fm(w), fm(16), tk(2 * LANE), fm(DIFF_WIDTH), tk(DIFF_WIDTH),
                 pl.BlockSpec((1, 1, N_DIFF_HEADS, DIFF_V_DIM + VAUG, tm), lambda b, j: (b, j, 0, 0, 0)),
                 tk(CONV_WIDTH))
    in_specs = [
        pl.BlockSpec((1, tm, D_MODEL), lambda b, j: (b, j, 0)),
        pl.BlockSpec((1, MOD_ROWS, 6 * D_MODEL), lambda b, j: (layer, 0, 0)),
        _const_spec((1, D_MODEL)),
        pl.BlockSpec((64, tm), lambda b, j: (0, j)),
        pl.BlockSpec((tm, 2 * LANE), lambda b, j: (j, 0)),
    ] + [_const_spec(a.shape) for a in wts]
    return pl.pallas_call(
        kern, grid=(bsz, nt), in_specs=in_specs, out_specs=out_specs, out_shape=out_shapes,
        compiler_params=_cparams(("parallel", "parallel")), name="inproj",
    )(xa, mod, g1, ropeT, ropeR, *wts)


def _mlstm_kernel(qf_ref, kf_ref, vf_ref, gTf_ref, gf_ref, qb_ref, kb_ref, vb_ref, gTb_ref, gb_ref,
                  hf_o, hb_o, c_scr, m_scr, *, chunk, nb):
    step = pl.program_id(1)
    L = chunk
    hd = MLSTM_HEAD_DIM
    nh = N_MLSTM_HEADS

    @pl.when(step == 0)
    def _():
        c_scr[...] = jnp.zeros(c_scr.shape, F32)
        m_scr[...] = jnp.full(m_scr.shape, M_INIT, F32)

    si = lax.broadcasted_iota(jnp.int32, (L, L), 0)
    ti = lax.broadcasted_iota(jnp.int32, (L, L), 1)
    ones_rows = jnp.ones((VAUG, L), BF16)
    zero64 = jnp.zeros((hd, L), BF16)

    dirs = ((qf_ref, kf_ref, vf_ref, gTf_ref, gf_ref, hf_o), (qb_ref, kb_ref, vb_ref, gTb_ref, gb_ref, hb_o))
    masks = []
    for d in range(2):
        causal = (si <= ti) if d == 0 else (si >= ti)
        tri = jnp.where(causal, 1.0, 0.0).astype(F32)
        masks.append((causal, tri, tri.T))

    chains = []
    for bb, d in [(bb, d) for bb in range(nb) for d in range(2)]:
        qT_ref, k_ref, vT_ref, gT_ref, g_ref, h_o = dirs[d]
        causal, tri, tri_t = masks[d]
        gT = gT_ref[bb]
        gc = g_ref[bb]
        b_rows = jnp.dot(gT, tri, preferred_element_type=F32, precision=HIGHEST)
        b_cols = jnp.dot(tri_t, gc, preferred_element_type=F32, precision=HIGHEST)
        totals = jnp.sum(gT, axis=1, keepdims=True)
        for hh in range(nh):
            pair, half = hh // 2, hh % 2
            q_h = qT_ref[bb, hh * hd:(hh + 1) * hd, :]
            q_msk = jnp.concatenate([q_h, zero64] if half == 0 else [zero64, q_h], axis=0)
            k_pair = k_ref[bb, :, pair * 2 * hd:(pair + 1) * 2 * hd]
            v_aug = jnp.concatenate([vT_ref[bb, hh * hd:(hh + 1) * hd, :], ones_rows], axis=0)
            idx = (bb * 2 + d) * nh + hh
            c_st = c_scr[idx]
            sT = jnp.dot(k_pair, q_msk, preferred_element_type=F32)
            cq = jnp.dot(c_st.astype(BF16), q_msk, preferred_element_type=F32)
            chains.append(dict(idx=idx, hh=hh, bb=bb, h_o=h_o, causal=causal, k_pair=k_pair, v_aug=v_aug,
                               c_st=c_st, sT=sT, cq=cq, li_row=gT[hh:hh + 1], b_row=b_rows[4 + hh:5 + hh],
                               total=totals[4 + hh:5 + hh],
                               a_col=gc[:, hh:hh + 1] - b_cols[:, 4 + hh:5 + hh]))

    for ch in chains:
        hh, b_row = ch["hh"], ch["b_row"]
        m_st = m_scr[ch["idx"], 0:1, :]
        dmat = jnp.where(ch["causal"], b_row + ch["a_col"], -jnp.inf)
        inter = b_row + m_st
        m_t = jnp.maximum(inter, jnp.max(dmat, axis=0, keepdims=True))
        wT = jnp.exp(dmat - m_t) * ch["sT"]
        e_inter = jnp.exp(inter - m_t)
        intra = jnp.dot(ch["v_aug"], wT.astype(BF16), preferred_element_type=F32)
        cq = ch["cq"]
        num = e_inter * cq[0:hd] + intra[0:hd]
        den = e_inter * cq[hd:hd + 1] + jnp.sum(wT, axis=0, keepdims=True)
        ch["h_o"][ch["bb"], hh * hd:(hh + 1) * hd, :] = num / jnp.maximum(jnp.abs(den), jnp.exp(-m_t))

        total = ch["total"]
        g_row = total - b_row + ch["li_row"]
        m_prev = m_st[:, 0:1]
        m_new = jnp.maximum(total + m_prev, jnp.max(g_row, axis=1, keepdims=True))
        e_old = jnp.exp(total + m_prev - m_new)
        e_g = jnp.exp(g_row - m_new)
        upd = jnp.dot((ch["v_aug"].astype(F32) * e_g).astype(BF16), ch["k_pair"],
                      preferred_element_type=F32)
        c_scr[ch["idx"]] = e_old * ch["c_st"] + upd
        m_scr[ch["idx"]] = jnp.broadcast_to(m_new, m_scr.shape[1:])


def _mlstm(qmT, km, vmT, gT, g, *, seq):
    bsz, w, t_all = qmT.shape
    L = MLSTM_CHUNK
    nlat = seq // L
    nch = t_all // L
    nctx = nch - nlat
    fwd = lambda i: jnp.where(i < nctx, nlat + i, i - nctx)
    bwd = lambda i: nch - 1 - i

    nb = MLSTM_BATCH if bsz % MLSTM_BATCH == 0 else 1

    def specs(chunk_of, d):
        return [pl.BlockSpec((nb, w, L), lambda b, i: (b, 0, chunk_of(i))),
                pl.BlockSpec((nb, L, w), lambda b, i: (b, chunk_of(i), 0)),
                pl.BlockSpec((nb, w, L), lambda b, i: (b, 0, chunk_of(i))),
                pl.BlockSpec((nb, 8, L), lambda b, i: (b, d, chunk_of(i))),
                pl.BlockSpec((nb, L, LANE), lambda b, i: (b, chunk_of(i), d))]

    kern = functools.partial(_mlstm_kernel, chunk=L, nb=nb)
    out = jax.ShapeDtypeStruct((bsz, w, t_all), F32)
    nchain = 2 * nb * N_MLSTM_HEADS
    return pl.pallas_call(
        kern, grid=(bsz // nb, nch),
        in_specs=specs(fwd, 0) + specs(bwd, 1),
        out_specs=(pl.BlockSpec((nb, w, L), lambda b, i: (b, 0, fwd(i))),
                   pl.BlockSpec((nb, w, L), lambda b, i: (b, 0, bwd(i)))),
        out_shape=(out, out),
        scratch_shapes=[pltpu.VMEM((nchain, MLSTM_HEAD_DIM + VAUG, LANE), F32),
                        pltpu.VMEM((nchain, 8, L), F32)],
        compiler_params=_cparams(("parallel", "arbitrary")), name="mlstm_scan",
    )(qmT, km, vmT, gT, g, qmT, km, vmT, gT, g)


def _attn_kernel(lam_ref, gs_ref, qT_ref, k_ref, vT_ref, *rest, nch, tk):
    o_ref = rest[-1]
    tq = qT_ref.shape[2]
    qT = qT_ref[0]
    z = jnp.zeros((DIFF_QK_DIM, tq), BF16)
    rhs = jnp.concatenate([jnp.concatenate([qT[:DIFF_QK_DIM], z], axis=0),
                           jnp.concatenate([z, qT[DIFF_QK_DIM:]], axis=0)], axis=1)
    ncb = 2 * tq // ATT_COLS
    ms = [jnp.full((1, ATT_COLS), NEG_BIG, F32) for _ in range(ncb)]
    accs = [jnp.zeros((DIFF_V_DIM + VAUG, ATT_COLS), F32) for _ in range(ncb)]
    sub = tk // ATT_KEYS
    units = [(c, cb) for c in range(nch * sub) for cb in range(ncb)]

    def scores(c, cb):
        return jnp.dot(k_ref[0, c * ATT_KEYS:(c + 1) * ATT_KEYS, :],
                       rhs[:, cb * ATT_COLS:(cb + 1) * ATT_COLS], preferred_element_type=F32)

    pending = [scores(*u) for u in units[:ATT_AHEAD]]
    for i, (c, cb) in enumerate(units):
        sT = pending.pop(0)
        if i + ATT_AHEAD < len(units):
            pending.append(scores(*units[i + ATT_AHEAD]))
        vT = vT_ref[0, c // sub, 0, :, (c % sub) * ATT_KEYS:(c % sub + 1) * ATT_KEYS]
        m_new = jnp.maximum(ms[cb], jnp.max(sT, axis=0, keepdims=True))
        p = jnp.exp2((sT - m_new).astype(BF16))
        alpha = jnp.exp2(ms[cb] - m_new)
        accs[cb] = alpha * accs[cb] + jnp.dot(vT, p, preferred_element_type=F32)
        ms[cb] = m_new
    acc = jnp.concatenate(accs, axis=1)
    l = acc[DIFF_V_DIM:DIFF_V_DIM + 1]
    acc = acc[0:DIFF_V_DIM]

    lv = lam_ref[...]
    lam_init = lv[4:5, 0:1]
    lam = (jnp.exp(jnp.sum(lv[0:1] * lv[1:2], axis=1, keepdims=True))
           - jnp.exp(jnp.sum(lv[2:3] * lv[3:4], axis=1, keepdims=True)) + lam_init)
    oT = acc[:, :tq] / l[:, :tq] - lam * (acc[:, tq:] / l[:, tq:])
    oT = oT * lax.rsqrt(jnp.mean(oT * oT, axis=0, keepdims=True) + EPS) * gs_ref[...] * (1.0 - lam_init)
    o_ref[0] = oT.T.astype(BF16)


def _attention(lamv, gs_col, qaT, ka, vaT, d_prev, out_rows, *, q_tile, q_blk0, n_q, k_rows, k_blk0,
               v_chunks, v_chunk0, v_cols, v_blk0):
    bsz = qaT.shape[0]
    kern = functools.partial(_attn_kernel, nch=v_chunks, tk=v_cols)
    in_specs = [_const_spec(lamv.shape), _const_spec((DIFF_V_DIM, 1)),
                pl.BlockSpec((1, DIFF_V_DIM, q_tile), lambda b, h, i: (b, h, q_blk0 + i)),
                pl.BlockSpec((1, k_rows, DIFF_V_DIM), lambda b, h, i: (b, k_blk0, h)),
                pl.BlockSpec((1, v_chunks, 1, DIFF_V_DIM + VAUG, v_cols),
                             lambda b, h, i: (b, v_chunk0, h, 0, v_blk0))]
    args = [lamv, gs_col, qaT, ka, vaT]
    aliases = {}
    if d_prev is not None:
        in_specs.append(pl.BlockSpec(memory_space=pl.ANY))
        aliases = {len(args): 0}
        args.append(d_prev)
    return pl.pallas_call(
        kern, grid=(bsz, N_DIFF_HEADS, n_q), in_specs=in_specs,
        out_specs=pl.BlockSpec((1, q_tile, DIFF_V_DIM), lambda b, h, i: (b, q_blk0 + i, h)),
        out_shape=jax.ShapeDtypeStruct((bsz, out_rows, DIFF_WIDTH), BF16),
        input_output_aliases=aliases,
        compiler_params=_cparams(("parallel", "parallel", "arbitrary")), name="diff_attn",
    )(*args)


def _conv_kernel(l_ref, c_ref, r_ref, w_ref, b_ref, g_ref, bb_ref, o_ref, buf, shifted, *, seq, tc, nt):
    j = pl.program_id(1)
    start = j * tc
    lvalid = jnp.logical_and(j > 0, start != seq)
    rvalid = jnp.logical_and(j < nt - 1, start + tc != seq)
    hl = CONV_HALO
    buf[0:hl, :] = jnp.where(lvalid, l_ref[0].astype(F32), 0.0)
    buf[hl:hl + tc, :] = c_ref[0].astype(F32)
    buf[hl + tc:2 * hl + tc, :] = jnp.where(rvalid, r_ref[0].astype(F32), 0.0)
    pad = CONV_KERNEL // 2
    sub = 128
    sl = CONV_SHIFTS
    for r in range(sl):
        shifted[r] = buf[pl.ds(r, shifted.shape[1]), :]
    for r0 in range(0, tc, sub):
        accs = [jnp.zeros((sub, LANE), F32) for _ in range(CONV_WIDTH // LANE)]
        for t in range(CONV_KERNEL):
            q, r = divmod(hl - pad + t, sl)
            for cb in range(CONV_WIDTH // LANE):
                cols = slice(cb * LANE, (cb + 1) * LANE)
                accs[cb] = accs[cb] + w_ref[t:t + 1, cols] * shifted[r, pl.ds(r0 + sl * q, sub), cols]
        y = jnp.concatenate(accs, axis=1) + b_ref[...]
        mu = jnp.mean(y, axis=-1, keepdims=True)
        var = jnp.mean(jnp.square(y - mu), axis=-1, keepdims=True)
        z = (y - mu) * lax.rsqrt(var + EPS) * g_ref[...] + bb_ref[...]
        o_ref[0, r0:r0 + sub, :] = (z * _sigmoid(z)).astype(BF16)


def _conv(u, w_dw, b_dw, g_ln, b_ln, *, seq):
    bsz, t_all, cw = u.shape
    tc = CONV_TILE
    nt = t_all // tc
    r = tc // CONV_HALO
    nhalo = t_all // CONV_HALO
    kern = functools.partial(_conv_kernel, seq=seq, tc=tc, nt=nt)
    row = lambda a: a.reshape(1, cw)
    return pl.pallas_call(
        kern, grid=(bsz, nt),
        in_specs=[pl.BlockSpec((1, CONV_HALO, cw), lambda b, j: (b, jnp.maximum(j * r - 1, 0), 0)),
                  pl.BlockSpec((1, tc, cw), lambda b, j: (b, j, 0)),
                  pl.BlockSpec((1, CONV_HALO, cw), lambda b, j: (b, jnp.minimum((j + 1) * r, nhalo - 1), 0)),
                  _const_spec((CONV_KERNEL, cw)), _const_spec((1, cw)), _const_spec((1, cw)),
                  _const_spec((1, cw))],
        out_specs=pl.BlockSpec((1, tc, cw), lambda b, j: (b, j, 0)),
        out_shape=jax.ShapeDtypeStruct((bsz, t_all, cw), BF16),
        scratch_shapes=[pltpu.VMEM((tc + 2 * CONV_HALO, cw), F32),
                        pltpu.VMEM((CONV_SHIFTS, tc + 2 * CONV_HALO - CONV_SHIFTS, cw), F32)],
        compiler_params=_cparams(("parallel", "parallel")), name="conv_mixer",
    )(u, u, u, w_dw, row(b_dw), row(g_ln), row(b_ln))


def _mixout_kernel(x_ref, mod_ref, hf_ref, hb_ref, omT_ref, gm_ref, d_ref, c_ref, wm_ref, wd_ref, wc_ref, o_ref,
                   *, seq, tm, ctx_row):
    b = pl.program_id(0)
    j = pl.program_id(1)
    is_ctx = _is_ctx_rows(j, tm, seq)
    hT = hf_ref[0] + hb_ref[0]
    h4 = hT.reshape(N_MLSTM_HEADS, MLSTM_HEAD_DIM, tm)
    mu = jnp.mean(h4, axis=1, keepdims=True)
    var = jnp.mean(jnp.square(h4 - mu), axis=1, keepdims=True)
    hn = ((h4 - mu) * lax.rsqrt(var + EPS)).reshape(MLSTM_WIDTH, tm)
    mT = _sigmoid(omT_ref[0].astype(F32)) * hn * gm_ref[...]
    m = mT.T.astype(BF16)
    y = (jnp.dot(m, wm_ref[...], preferred_element_type=F32)
         + jnp.dot(d_ref[0], wd_ref[...], preferred_element_type=F32)
         + jnp.dot(c_ref[0], wc_ref[...], preferred_element_type=F32))
    g_l, g_c = _mod_rows(mod_ref, b, ctx_row, 2)
    o_ref[0] = x_ref[0] + jnp.where(is_ctx, g_c, g_l) * y


def _mixout(xa, mod, layer, hTf, hTb, omT, gm_col, d, cx, wm, wd, wc, *, seq):
    bsz, t_all, _ = xa.shape
    tm = TOK_TILE
    nt = t_all // tm
    kern = functools.partial(_mixout_kernel, seq=seq, tm=tm, ctx_row=bsz)
    w = MLSTM_WIDTH
    return pl.pallas_call(
        kern, grid=(bsz, nt),
        in_specs=[pl.BlockSpec((1, tm, D_MODEL), lambda b, j: (b, j, 0)),
                  pl.BlockSpec((1, MOD_ROWS, 6 * D_MODEL), lambda b, j: (layer, 0, 0)),
                  pl.BlockSpec((1, w, tm), lambda b, j: (b, 0, j)),
                  pl.BlockSpec((1, w, tm), lambda b, j: (b, 0, j)),
                  pl.BlockSpec((1, w, tm), lambda b, j: (b, 0, j)),
                  _const_spec((w, 1)),
                  pl.BlockSpec((1, tm, DIFF_WIDTH), lambda b, j: (b, j, 0)),
                  pl.BlockSpec((1, tm, CONV_WIDTH), lambda b, j: (b, j, 0)),
                  _const_spec(wm.shape), _const_spec(wd.shape), _const_spec(wc.shape)],
        out_specs=pl.BlockSpec((1, tm, D_MODEL), lambda b, j: (b, j, 0)),
        out_shape=jax.ShapeDtypeStruct(xa.shape, F32),
        compiler_params=_cparams(("parallel", "parallel")), name="mix_out",
    )(xa, mod, hTf, hTb, omT, gm_col, d, cx, wm, wd, wc)


def _swiglu_partial(hb, wg, wu, wd):
    a = jnp.dot(hb, wg, preferred_element_type=F32)
    u = jnp.dot(hb, wu, preferred_element_type=F32)
    t = (a * _sigmoid(a) * u).astype(BF16)
    return jnp.dot(t, wd, preferred_element_type=F32)


def _ffn_kernel(x_ref, mod_ref, g2_ref, wg_ref, wu_ref, wd_ref, o_ref, hb_scr, acc_scr,
                *, seq, tm, ctx_row, nf):
    b = pl.program_id(0)
    j = pl.program_id(1)
    f = pl.program_id(2)
    is_ctx = _is_ctx_rows(j, tm, seq)

    @pl.when(f == 0)
    def _():
        h = _rms_mod(x_ref[0], g2_ref[...], mod_ref, b, ctx_row, is_ctx, 3, 4)
        hb_scr[...] = h.astype(BF16)
        acc_scr[...] = jnp.zeros(acc_scr.shape, F32)

    acc_scr[...] += _swiglu_partial(hb_scr[...], wg_ref[0], wu_ref[0], wd_ref[0])

    @pl.when(f == nf - 1)
    def _():
        g_l, g_c = _mod_rows(mod_ref, b, ctx_row, 5)
        o_ref[0] = x_ref[0] + jnp.where(is_ctx, g_c, g_l) * acc_scr[...]


def _ffn(xa, mod, layer, g2, wg, wu, wd, ffn_layer, *, seq):
    bsz, t_all, _ = xa.shape
    tm = TOK_TILE
    nt = t_all // tm
    nf = FFN_SPLIT
    tf = wg.shape[2] // nf
    assert tf % LANE == 0 and tf * nf == wg.shape[2]
    kern = functools.partial(_ffn_kernel, seq=seq, tm=tm, ctx_row=bsz, nf=nf)
    return pl.pallas_call(
        kern, grid=(bsz, nt, nf),
        in_specs=[pl.BlockSpec((1, tm, D_MODEL), lambda b, j, f: (b, j, 0)),
                  pl.BlockSpec((1, MOD_ROWS, 6 * D_MODEL), lambda b, j, f: (layer, 0, 0)),
                  _const_spec((1, D_MODEL)),
                  pl.BlockSpec((1, D_MODEL, tf), lambda b, j, f: (ffn_layer, 0, f)),
                  pl.BlockSpec((1, D_MODEL, tf), lambda b, j, f: (ffn_layer, 0, f)),
                  pl.BlockSpec((1, tf, D_MODEL), lambda b, j, f: (ffn_layer, f, 0))],
        out_specs=pl.BlockSpec((1, tm, D_MODEL), lambda b, j, f: (b, j, 0)),
        out_shape=jax.ShapeDtypeStruct(xa.shape, F32),
        scratch_shapes=[pltpu.VMEM((tm, D_MODEL), BF16), pltpu.VMEM((tm, D_MODEL), F32)],
        compiler_params=_cparams(("parallel", "parallel", "arbitrary")), name="ffn_swiglu",
    )(xa, mod, g2, wg, wu, wd)


def _moe_kernel(x_ref, mod_ref, g2_ref, wr_ref, br_ref, wg_ref, wu_ref, wd_ref, o_ref,
                hb_scr, acc_scr, comb_scr, slot_scr, slotT_scr, *, seq, tm, nt, ctx_row, ne, group):
    step = pl.program_id(0)
    e = pl.program_id(1)
    lane = lax.broadcasted_iota(jnp.int32, (tm, LANE), 1)

    for half in range(group):
        tile = step * group + half
        b = tile // nt
        is_ctx = _is_ctx_rows(tile % nt, tm, seq)
        rows = slice(half * tm, (half + 1) * tm)

        @pl.when(e == 0)
        def _():
            h = _rms_mod(x_ref[rows, :], g2_ref[...], mod_ref, b, ctx_row, is_ctx, 3, 4)
            hb_scr[half] = h.astype(BF16)
            acc_scr[half] = jnp.zeros((tm, D_MODEL), F32)
            logits = jnp.dot(h, wr_ref[...], preferred_element_type=F32, precision=HIGHEST) + br_ref[...]
            logits = jnp.where(lane < ne, logits, -jnp.inf)
            ex = jnp.exp(logits - jnp.max(logits, axis=-1, keepdims=True))
            probs = ex / jnp.sum(ex, axis=-1, keepdims=True)
            v1 = jnp.max(probs, axis=-1, keepdims=True)
            i1 = jnp.min(jnp.where(probs == v1, lane, LANE), axis=-1, keepdims=True)
            rest = jnp.where(lane == i1, -1.0, probs)
            v2 = jnp.max(rest, axis=-1, keepdims=True)
            i2 = jnp.min(jnp.where(rest == v2, lane, LANE), axis=-1, keepdims=True)
            tot = v1 + v2
            comb_scr[half] = jnp.where(lane == i1, v1 / tot, 0.0) + jnp.where(lane == i2, v2 / tot, 0.0)
            sel = jnp.logical_or(lane == i1, lane == i2)
            ri = lax.broadcasted_iota(jnp.int32, (tm, tm), 0)
            ci = lax.broadcasted_iota(jnp.int32, (tm, tm), 1)
            before = jnp.where(ci < ri, 1.0, 0.0).astype(BF16)
            rank = jnp.dot(before, jnp.where(sel, 1.0, 0.0).astype(BF16), preferred_element_type=F32)
            slot = jnp.where(sel, rank, -1.0)
            slot_scr[half] = slot
            slotT_scr[half] = slot.T

        onlane = lane == e
        cw = jnp.sum(jnp.where(onlane, comb_scr[half], 0.0), axis=-1, keepdims=True)
        slot_c = jnp.max(jnp.where(onlane, slot_scr[half], -1.0), axis=-1, keepdims=True)
        slot_r = slotT_scr[half, pl.ds(e, 1), :]
        count = (jnp.max(slot_r) + 1.0).astype(jnp.int32)

        def run_block(base, size):
            row_i = lax.broadcasted_iota(jnp.int32, (size, 1), 0).astype(F32)
            col_i = lax.broadcasted_iota(jnp.int32, (1, size), 1).astype(F32)
            gather = jnp.where(slot_r == base + row_i, 1.0, 0.0).astype(BF16)
            xs = jnp.dot(gather, hb_scr[half], preferred_element_type=F32).astype(BF16)
            y = _swiglu_partial(xs, wg_ref[0, 0], wu_ref[0, 0], wd_ref[0, 0])
            scatter = jnp.where(slot_c == base + col_i, 1.0, 0.0).astype(BF16)
            acc_scr[half] += cw * jnp.dot(scatter, y.astype(BF16), preferred_element_type=F32)

        lo = 0
        for size in MOE_BLOCKS:
            pl.when(jnp.logical_and(count > lo, count <= size))(functools.partial(run_block, 0.0, size))
            lo = size
        big = MOE_BLOCKS[-1]

        @pl.when(count > big)
        def _():
            def body(i, carry):
                run_block((i * big).astype(F32), big)
                return carry
            lax.fori_loop(0, (count + big - 1) // big, body, 0)

        @pl.when(e == ne - 1)
        def _():
            g_l, g_c = _mod_rows(mod_ref, b, ctx_row, 5)
            o_ref[rows, :] = x_ref[rows, :] + jnp.where(is_ctx, g_c, g_l) * acc_scr[half]


def _moe(xa, mod, layer, g2, wr, br, wg, wu, wd, moe_layer, *, seq):
    bsz, t_all, _ = xa.shape
    tm = TOK_TILE
    nt = t_all // tm
    group = MOE_GROUP
    assert (bsz * nt) % group == 0
    ne, fe = wg.shape[1], wg.shape[3]
    kern = functools.partial(_moe_kernel, seq=seq, tm=tm, nt=nt, ctx_row=bsz, ne=ne, group=group)
    out = pl.pallas_call(
        kern, grid=(bsz * nt // group, ne),
        in_specs=[pl.BlockSpec((group * tm, D_MODEL), lambda s, e: (s, 0)),
                  pl.BlockSpec((1, MOD_ROWS, 6 * D_MODEL), lambda s, e: (layer, 0, 0)),
                  _const_spec((1, D_MODEL)), _const_spec(wr.shape), _const_spec(br.shape),
                  pl.BlockSpec((1, 1, D_MODEL, fe), lambda s, e: (moe_layer, e, 0, 0)),
                  pl.BlockSpec((1, 1, D_MODEL, fe), lambda s, e: (moe_layer, e, 0, 0)),
                  pl.BlockSpec((1, 1, fe, D_MODEL), lambda s, e: (moe_layer, e, 0, 0))],
        out_specs=pl.BlockSpec((group * tm, D_MODEL), lambda s, e: (s, 0)),
        out_shape=jax.ShapeDtypeStruct((bsz * t_all, D_MODEL), F32),
        scratch_shapes=[pltpu.VMEM((group, tm, D_MODEL), BF16), pltpu.VMEM((group, tm, D_MODEL), F32),
                        pltpu.VMEM((group, tm, LANE), F32), pltpu.VMEM((group, tm, LANE), F32),
                        pltpu.VMEM((group, LANE, tm), F32)],
        compiler_params=_cparams(("parallel", "arbitrary")), name="moe_swiglu",
    )(xa.reshape(bsz * t_all, D_MODEL), mod, g2, wr, br, wg, wu, wd)
    return out.reshape(xa.shape)


def _final_kernel(x_ref, g_ref, o_ref):
    x = x_ref[0]
    o_ref[0] = x * lax.rsqrt(jnp.mean(x * x, axis=-1, keepdims=True) + EPS) * g_ref[...]


def _final_norm(xa, g, *, seq):
    bsz = xa.shape[0]
    tf = 512
    return pl.pallas_call(
        _final_kernel, grid=(bsz, seq // tf),
        in_specs=[pl.BlockSpec((1, tf, D_MODEL), lambda b, j: (b, j, 0)), _const_spec((1, D_MODEL))],
        out_specs=pl.BlockSpec((1, tf, D_MODEL), lambda b, j: (b, j, 0)),
        out_shape=jax.ShapeDtypeStruct((bsz, seq, D_MODEL), F32),
        compiler_params=_cparams(("parallel", "parallel")), name="final_norm",
    )(xa, g.reshape(1, D_MODEL))


def _rope_tables(seq, t_all):
    pos = np.arange(seq)
    per_axis = DIFF_QK_DIM // 2
    inv = (ROPE_BASE ** (-np.arange(0, per_axis, 2, dtype=np.float32) / per_axis)).astype(np.float32)
    rowp = (pos // GRID_W).astype(np.float32)
    colp = (pos % GRID_W).astype(np.float32)
    ang = np.stack([rowp[:, None] * inv, colp[:, None] * inv], axis=1).astype(np.float64)
    cos = np.concatenate([np.cos(ang), np.ones((t_all - seq, 2, 16))], axis=0)
    sin = np.concatenate([np.sin(ang), np.zeros((t_all - seq, 2, 16))], axis=0)
    qscale = (DIFF_QK_DIM ** -0.5) * LOG2E
    ropeT = np.concatenate([cos.reshape(t_all, 32).T, sin.reshape(t_all, 32).T], axis=0) * qscale
    cos64 = np.concatenate([cos[:, 0], cos[:, 0], cos[:, 1], cos[:, 1]], axis=-1)
    sin64 = np.concatenate([-sin[:, 0], sin[:, 0], -sin[:, 1], sin[:, 1]], axis=-1)
    ropeR = np.concatenate([cos64, cos64, sin64, sin64], axis=-1)
    return jnp.asarray(ropeT, F32), jnp.asarray(ropeR, F32)


def _swap_perm():
    idx = np.arange(2 * N_DIFF_HEADS * DIFF_QK_DIM)
    return np.where((idx % 32) < 16, idx + 16, idx - 16)


def _prep_inproj_weights(w_in_l, b_gates_l):
    offs = np.cumsum((0,) + IN_SPLITS)
    col = lambda i: w_in_l[:, offs[i]:offs[i + 1]]
    mq, mk, mv, mo, gt, aq, ak, av, cv = (col(i) for i in range(9))
    nh = N_MLSTM_HEADS
    wTm = jnp.concatenate([mq, mv, mo], axis=1).T.astype(BF16)
    wTg = gt.T.astype(BF16)
    bgT = b_gates_l.reshape(4 * nh, 1).astype(F32)
    wkm = (mk * (MLSTM_HEAD_DIM ** -0.5)).astype(BF16)
    zpad = jnp.zeros((D_MODEL, LANE - 2 * nh), F32)
    wg = jnp.concatenate([gt[:, 0:2 * nh], zpad, gt[:, 2 * nh:4 * nh], zpad], axis=1).astype(BF16)
    bpad = jnp.zeros((LANE - 2 * nh,), F32)
    bg = jnp.concatenate([b_gates_l[0:2 * nh], bpad, b_gates_l[2 * nh:4 * nh], bpad]).reshape(1, 2 * LANE)
    wTaq = aq.T.astype(BF16)
    wak = jnp.concatenate([ak, ak[:, _swap_perm()]], axis=1).astype(BF16)
    wTav = av.T.astype(BF16)
    wcv = cv.astype(BF16)
    return (wTm, wTg, bgT, wkm, wg, bg, wTaq, wak, wTav, wcv)


def kernel(x, c, ctx, c_ctx, w_mod, b_mod, g_norm1, w_in, b_gates, g_mlstm, lambda_q1, lambda_k1,
           lambda_q2, lambda_k2, g_subln, w_dw, b_dw, g_conv_ln, b_conv_ln, w_out, g_norm2,
           w_ffn_gate, w_ffn_up, w_ffn_down, w_router, b_router, w_exp_gate, w_exp_up, w_exp_down,
           g_final):
    bsz, seq, _ = x.shape
    nctx = ctx.shape[1]
    t_all = seq + nctx
    depth = w_mod.shape[0]
    assert nctx == CTX_LEN == MLSTM_CHUNK and bsz + 1 <= MOD_ROWS
    assert t_all % TOK_TILE == 0 and seq % Q_TILE == 0 and seq % CONV_TILE == 0 and seq % GRID_W == 0

    xa = jnp.concatenate([x, ctx], axis=1)
    cond = jnp.concatenate([c, c_ctx[None, :], jnp.zeros((MOD_ROWS - bsz - 1, D_MODEL), F32)], axis=0)
    mod = _mod_table(cond, w_mod, b_mod)
    ropeT, ropeR = _rope_tables(seq, t_all)
    nkc = t_all // TOK_TILE
    weg, weu, wed = w_exp_gate.astype(BF16), w_exp_up.astype(BF16), w_exp_down.astype(BF16)
    wfg, wfu, wfd = w_ffn_gate.astype(BF16), w_ffn_up.astype(BF16), w_ffn_down.astype(BF16)

    for l in range(depth):
        wts = _prep_inproj_weights(w_in[l], b_gates[l])
        (qmT, km, vmT, omT, gT, g, qaT, ka, vaT, u) = _inproj(
            xa, mod, l, g_norm1[l].reshape(1, D_MODEL), ropeT, ropeR, wts, seq=seq)

        hTf, hTb = _mlstm(qmT, km, vmT, gT, g, seq=seq)

        lam_init = 0.8 - 0.6 * math.exp(-0.3 * l)
        lamv = jnp.zeros((8, LANE), F32).at[0:4, 0:DIFF_QK_DIM].set(
            jnp.stack([lambda_q1[l], lambda_k1[l], lambda_q2[l], lambda_k2[l]]).astype(F32))
        lamv = lamv.at[4, :].set(lam_init)
        gs_col = g_subln[l].reshape(DIFF_V_DIM, 1).astype(F32)
        d = _attention(lamv, gs_col, qaT, ka, vaT, None, t_all, q_tile=Q_TILE, q_blk0=0,
                       n_q=seq // Q_TILE, k_rows=t_all, k_blk0=0, v_chunks=nkc, v_chunk0=0,
                       v_cols=TOK_TILE, v_blk0=0)
        d = _attention(lamv, gs_col, qaT, ka, vaT, d, t_all, q_tile=nctx, q_blk0=seq // nctx, n_q=1,
                       k_rows=nctx, k_blk0=seq // nctx, v_chunks=1, v_chunk0=nkc - 1,
                       v_cols=nctx, v_blk0=TOK_TILE // nctx - 1)

        cx = _conv(u, w_dw[l], b_dw[l], g_conv_ln[l], b_conv_ln[l], seq=seq)

        wo = w_out[l].astype(BF16)
        xa = _mixout(xa, mod, l, hTf, hTb, omT, g_mlstm[l].reshape(MLSTM_WIDTH, 1).astype(F32), d, cx,
                     wo[0:MLSTM_WIDTH], wo[MLSTM_WIDTH:MLSTM_WIDTH + DIFF_WIDTH],
                     wo[MLSTM_WIDTH + DIFF_WIDTH:], seq=seq)

        jj = l // 2
        g2 = g_norm2[l].reshape(1, D_MODEL)
        if l % 2 == 0:
            xa = _ffn(xa, mod, l, g2, wfg, wfu, wfd, jj, seq=seq)
        else:
            wr = jnp.concatenate([w_router[jj], jnp.zeros((D_MODEL, LANE - N_EXPERTS), F32)], axis=1)
            br = jnp.concatenate([b_router[jj], jnp.zeros((LANE - N_EXPERTS,), F32)]).reshape(1, LANE)
            xa = _moe(xa, mod, l, g2, wr, br, weg, weu, wed, jj, seq=seq)

    return _final_norm(xa, g_final, seq=seq)
```

```python
import functools
import math

import jax
import jax.numpy as jnp
import numpy as np
from jax import lax
from jax.experimental import pallas as pl
from jax.experimental.pallas import tpu as pltpu

F32 = jnp.float32
BF16 = jnp.bfloat16
HIGHEST = lax.Precision.HIGHEST

D_MODEL = 1024
DEPTH = 4
GRID_W = 64
CTX_LEN = 256
N_MLSTM_HEADS = 4
MLSTM_HEAD_DIM = 64
MLSTM_WIDTH = N_MLSTM_HEADS * MLSTM_HEAD_DIM
N_DIFF_HEADS = 4
DIFF_QK_DIM = 64
DIFF_V_DIM = 2 * DIFF_QK_DIM
DIFF_WIDTH = N_DIFF_HEADS * DIFF_V_DIM
ROPE_BASE = 10000.0
CONV_WIDTH = 256
CONV_KERNEL = 31
IN_SPLITS = (MLSTM_WIDTH, MLSTM_WIDTH, MLSTM_WIDTH, MLSTM_WIDTH, 4 * N_MLSTM_HEADS,
             2 * N_DIFF_HEADS * DIFF_QK_DIM, 2 * N_DIFF_HEADS * DIFF_QK_DIM, DIFF_WIDTH,
             2 * CONV_WIDTH)
D_FF = 2816
N_EXPERTS = 8
D_FF_EXPERT = 1408
EPS = 1e-6
M_INIT = -1e30
NEG_BIG = -1e30
LOG2E = 1.4426950408889634

LANE = 128
V7X_VMEM_LIMIT = 56 * 1024 * 1024
TOK_TILE = 768
MLSTM_CHUNK = 256
CONV_TILE = 256
CONV_HALO = 16
CONV_SHIFTS = 8
Q_TILE = 1024
MOD_ROWS = 8
VAUG = 16
MOE_BLOCKS = (128, 192, 256, 320, 384)
MOE_LOOP_BLOCK = 256
FFN_SPLIT = 2
MOE_GROUP = 1
MLSTM_BATCH = 2
ATT_KEYS = 256
ATT_COLS = 256
ATT_AHEAD = 4


def _cparams(sem):
    return pltpu.CompilerParams(dimension_semantics=sem, vmem_limit_bytes=V7X_VMEM_LIMIT)


def _sigmoid(v):
    return 1.0 / (1.0 + jnp.exp(-v))


def _log_sigmoid(v):
    return jnp.minimum(v, 0.0) - jnp.log(1.0 + jnp.exp(-jnp.abs(v)))


def _mod_rows(mod_ref, b, ctx_row, k):
    lat = mod_ref[0, pl.ds(b, 1), k * D_MODEL:(k + 1) * D_MODEL]
    ctx = mod_ref[0, ctx_row:ctx_row + 1, k * D_MODEL:(k + 1) * D_MODEL]
    return lat, ctx


def _is_ctx_rows(j, tm, seq):
    rows = j * tm + lax.broadcasted_iota(jnp.int32, (tm, 1), 0)
    return rows >= seq


def _rms_mod(x, g, mod_ref, b, ctx_row, is_ctx, k_shift, k_scale):
    y = x * lax.rsqrt(jnp.mean(x * x, axis=-1, keepdims=True) + EPS) * g
    sh_l, sh_c = _mod_rows(mod_ref, b, ctx_row, k_shift)
    sc_l, sc_c = _mod_rows(mod_ref, b, ctx_row, k_scale)
    shift = jnp.where(is_ctx, sh_c, sh_l)
    scale = jnp.where(is_ctx, sc_c, sc_l)
    return y * (1.0 + scale) + shift


def _mod_kernel(cond_ref, w_ref, b_ref, o_ref):
    c = cond_ref[...]
    s = c * _sigmoid(c)
    o_ref[0] = jnp.dot(s, w_ref[0], preferred_element_type=F32, precision=HIGHEST) + b_ref[0]


def _mod_table(cond, w_mod, b_mod):
    depth = w_mod.shape[0]
    n = w_mod.shape[2] // D_MODEL
    return pl.pallas_call(
        _mod_kernel,
        grid=(depth, n),
        in_specs=[pl.BlockSpec((MOD_ROWS, D_MODEL), lambda l, c: (0, 0)),
                  pl.BlockSpec((1, D_MODEL, D_MODEL), lambda l, c: (l, 0, c)),
                  pl.BlockSpec((1, 1, D_MODEL), lambda l, c: (l, 0, c))],
        out_specs=pl.BlockSpec((1, MOD_ROWS, D_MODEL), lambda l, c: (l, 0, c)),
        out_shape=jax.ShapeDtypeStruct((depth, MOD_ROWS, n * D_MODEL), F32),
        compiler_params=_cparams(("parallel", "parallel")),
        name="mod_table",
    )(cond, w_mod, b_mod.reshape(depth, 1, n * D_MODEL))


def _inproj_kernel(x_ref, mod_ref, g1_ref, ropeT_ref, ropeR_ref,
                   wTm_ref, wTg_ref, bgT_ref, wkm_ref, wg_ref, bg_ref,
                   wTaq_ref, wak_ref, wTav_ref, wcv_ref,
                   qmT_o, km_o, vmT_o, omT_o, gT_o, g_o, qaT_o, ka_o, vaT_o, u_o,
                   *, seq, tm, ctx_row):
    b = pl.program_id(0)
    j = pl.program_id(1)
    is_ctx = _is_ctx_rows(j, tm, seq)
    h = _rms_mod(x_ref[0], g1_ref[...], mod_ref, b, ctx_row, is_ctx, 0, 1)
    hb = h.astype(BF16)
    hT = h.T.astype(BF16)

    mT = jnp.dot(wTm_ref[...], hT, preferred_element_type=F32)
    w = MLSTM_WIDTH
    qmT_o[0] = mT[0:w].astype(BF16)
    vmT_o[0] = mT[w:2 * w].astype(BF16)
    omT_o[0] = mT[2 * w:3 * w].astype(BF16)
    gT = jnp.dot(wTg_ref[...], hT, preferred_element_type=F32) + bgT_ref[...]
    rowi = lax.broadcasted_iota(jnp.int32, gT.shape, 0)
    gT_o[0] = jnp.where((rowi % 8) >= 4, _log_sigmoid(gT), gT)
    km_o[0] = jnp.dot(hb, wkm_ref[...], preferred_element_type=F32).astype(BF16)
    g = jnp.dot(hb, wg_ref[...], preferred_element_type=F32) + bg_ref[...]
    lanei = lax.broadcasted_iota(jnp.int32, g.shape, 1) % LANE
    g_o[0] = jnp.where((lanei >= 4) & (lanei < 8), _log_sigmoid(g), g)

    qT = jnp.dot(wTaq_ref[...], hT, preferred_element_type=F32)
    for grp in range(2 * N_DIFF_HEADS * 2):
        ax = grp % 2
        cos = ropeT_ref[ax * 16:(ax + 1) * 16, :]
        sin = ropeT_ref[32 + ax * 16:32 + (ax + 1) * 16, :]
        x1 = qT[grp * 32:grp * 32 + 16]
        x2 = qT[grp * 32 + 16:grp * 32 + 32]
        qaT_o[0, grp * 32:grp * 32 + 16, :] = (x1 * cos - x2 * sin).astype(BF16)
        qaT_o[0, grp * 32 + 16:grp * 32 + 32, :] = (x2 * cos + x1 * sin).astype(BF16)
    kk = jnp.dot(hb, wak_ref[...], preferred_element_type=F32)
    cosr = ropeR_ref[:, 0:LANE]
    sinr = ropeR_ref[:, LANE:2 * LANE]
    nk = 2 * N_DIFF_HEADS * DIFF_QK_DIM
    for sl in range(nk // LANE):
        k0 = kk[:, sl * LANE:(sl + 1) * LANE]
        k1 = kk[:, nk + sl * LANE:nk + (sl + 1) * LANE]
        ka_o[0, :, sl * LANE:(sl + 1) * LANE] = (k0 * cosr + k1 * sinr).astype(BF16)
    vT = jnp.dot(wTav_ref[...], hT, preferred_element_type=F32)
    for hh in range(N_DIFF_HEADS):
        vaT_o[0, 0, hh, 0:DIFF_V_DIM, :] = vT[hh * DIFF_V_DIM:(hh + 1) * DIFF_V_DIM].astype(BF16)
        vaT_o[0, 0, hh, DIFF_V_DIM:DIFF_V_DIM + VAUG, :] = jnp.ones((VAUG, tm), BF16)
    cv = jnp.dot(hb, wcv_ref[...], preferred_element_type=F32)
    u_o[0] = (cv[:, :CONV_WIDTH] * _sigmoid(cv[:, CONV_WIDTH:])).astype(BF16)


def _const_spec(shape):
    nd = len(shape)
    return pl.BlockSpec(shape, lambda *_: (0,) * nd)


def _inproj(xa, mod, layer, g1, ropeT, ropeR, wts, *, seq):
    bsz, t_all, _ = xa.shape
    tm = TOK_TILE
    nt = t_all // tm
    kern = functools.partial(_inproj_kernel, seq=seq, tm=tm, ctx_row=bsz)
    w = MLSTM_WIDTH
    out_shapes = (
        jax.ShapeDtypeStruct((bsz, w, t_all), BF16),
        jax.ShapeDtypeStruct((bsz, t_all, w), BF16),
        jax.ShapeDtypeStruct((bsz, w, t_all), BF16),
        jax.ShapeDtypeStruct((bsz, w, t_all), BF16),
        jax.ShapeDtypeStruct((bsz, 16, t_all), F32),
        jax.ShapeDtypeStruct((bsz, t_all, 2 * LANE), F32),
        jax.ShapeDtypeStruct((bsz, DIFF_WIDTH, t_all), BF16),
        jax.ShapeDtypeStruct((bsz, t_all, DIFF_WIDTH), BF16),
        jax.ShapeDtypeStruct((bsz, nt, N_DIFF_HEADS, DIFF_V_DIM + VAUG, tm), BF16),
        jax.ShapeDtypeStruct((bsz, t_all, CONV_WIDTH), BF16),
    )
    fm = lambda rows: pl.BlockSpec((1, rows, tm), lambda b, j: (b, 0, j))
    tk = lambda cols: pl.BlockSpec((1, tm, cols), lambda b, j: (b, j, 0))
    out_specs = (fm(w), tk(w), fm(w), fm(w), fm(16), tk(2 * LANE), fm(DIFF_WIDTH), tk(DIFF_WIDTH),
                 pl.BlockSpec((1, 1, N_DIFF_HEADS, DIFF_V_DIM + VAUG, tm), lambda b, j: (b, j, 0, 0, 0)),
                 tk(CONV_WIDTH))
    in_specs = [
        pl.BlockSpec((1, tm, D_MODEL), lambda b, j: (b, j, 0)),
        pl.BlockSpec((1, MOD_ROWS, 6 * D_MODEL), lambda b, j: (layer, 0, 0)),
        _const_spec((1, D_MODEL)),
        pl.BlockSpec((64, tm), lambda b, j: (0, j)),
        pl.BlockSpec((tm, 2 * LANE), lambda b, j: (j, 0)),
    ] + [_const_spec(a.shape) for a in wts]
    return pl.pallas_call(
        kern, grid=(bsz, nt), in_specs=in_specs, out_specs=out_specs, out_shape=out_shapes,
        compiler_params=_cparams(("parallel", "parallel")), name="inproj",
    )(xa, mod, g1, ropeT, ropeR, *wts)


def _mlstm_kernel(qf_ref, kf_ref, vf_ref, gTf_ref, gf_ref, qb_ref, kb_ref, vb_ref, gTb_ref, gb_ref,
                  hf_o, hb_o, c_scr, m_scr, *, chunk, nb):
    step = pl.program_id(1)
    L = chunk
    hd = MLSTM_HEAD_DIM
    nh = N_MLSTM_HEADS

    @pl.when(step == 0)
    def _():
        c_scr[...] = jnp.zeros(c_scr.shape, F32)
        m_scr[...] = jnp.full(m_scr.shape, M_INIT, F32)

    si = lax.broadcasted_iota(jnp.int32, (L, L), 0)
    ti = lax.broadcasted_iota(jnp.int32, (L, L), 1)
    ones_rows = jnp.ones((VAUG, L), BF16)
    zero64 = jnp.zeros((hd, L), BF16)

    dirs = ((qf_ref, kf_ref, vf_ref, gTf_ref, gf_ref, hf_o), (qb_ref, kb_ref, vb_ref, gTb_ref, gb_ref, hb_o))
    masks = []
    for d in range(2):
        causal = (si <= ti) if d == 0 else (si >= ti)
        tri = jnp.where(causal, 1.0, 0.0).astype(F32)
        masks.append((causal, tri, tri.T))

    chains = []
    for bb, d in [(bb, d) for bb in range(nb) for d in range(2)]:
        qT_ref, k_ref, vT_ref, gT_ref, g_ref, h_o = dirs[d]
        causal, tri, tri_t = masks[d]
        gT = gT_ref[bb]
        gc = g_ref[bb]
        b_rows = jnp.dot(gT, tri, preferred_element_type=F32, precision=HIGHEST)
        b_cols = jnp.dot(tri_t, gc, preferred_element_type=F32, precision=HIGHEST)
        totals = jnp.sum(gT, axis=1, keepdims=True)
        for hh in range(nh):
            pair, half = hh // 2, hh % 2
            q_h = qT_ref[bb, hh * hd:(hh + 1) * hd, :]
            q_msk = jnp.concatenate([q_h, zero64] if half == 0 else [zero64, q_h], axis=0)
            k_pair = k_ref[bb, :, pair * 2 * hd:(pair + 1) * 2 * hd]
            v_aug = jnp.concatenate([vT_ref[bb, hh * hd:(hh + 1) * hd, :], ones_rows], axis=0)
            idx = (bb * 2 + d) * nh + hh
            c_st = c_scr[idx]
            sT = jnp.dot(k_pair, q_msk, preferred_element_type=F32)
            cq = jnp.dot(c_st.astype(BF16), q_msk, preferred_element_type=F32)
            chains.append(dict(idx=idx, hh=hh, bb=bb, h_o=h_o, causal=causal, k_pair=k_pair, v_aug=v_aug,
                               c_st=c_st, sT=sT, cq=cq, li_row=gT[hh:hh + 1], b_row=b_rows[4 + hh:5 + hh],
                               total=totals[4 + hh:5 + hh],
                               a_col=gc[:, hh:hh + 1] - b_cols[:, 4 + hh:5 + hh]))

    for ch in chains:
        hh, b_row = ch["hh"], ch["b_row"]
        m_st = m_scr[ch["idx"], 0:1, :]
        dmat = jnp.where(ch["causal"], b_row + ch["a_col"], -jnp.inf)
        inter = b_row + m_st
        m_t = jnp.maximum(inter, jnp.max(dmat, axis=0, keepdims=True))
        wT = jnp.exp(dmat - m_t) * ch["sT"]
        e_inter = jnp.exp(inter - m_t)
        intra = jnp.dot(ch["v_aug"], wT.astype(BF16), preferred_element_type=F32)
        cq = ch["cq"]
        num = e_inter * cq[0:hd] + intra[0:hd]
        den = e_inter * cq[hd:hd + 1] + jnp.sum(wT, axis=0, keepdims=True)
        ch["h_o"][ch["bb"], hh * hd:(hh + 1) * hd, :] = num / jnp.maximum(jnp.abs(den), jnp.exp(-m_t))

        total = ch["total"]
        g_row = total - b_row + ch["li_row"]
        m_prev = m_st[:, 0:1]
        m_new = jnp.maximum(total + m_prev, jnp.max(g_row, axis=1, keepdims=True))
        e_old = jnp.exp(total + m_prev - m_new)
        e_g = jnp.exp(g_row - m_new)
        upd = jnp.dot((ch["v_aug"].astype(F32) * e_g).astype(BF16), ch["k_pair"],
                      preferred_element_type=F32)
        c_scr[ch["idx"]] = e_old * ch["c_st"] + upd
        m_scr[ch["idx"]] = jnp.broadcast_to(m_new, m_scr.shape[1:])


def _mlstm(qmT, km, vmT, gT, g, *, seq):
    bsz, w, t_all = qmT.shape
    L = MLSTM_CHUNK
    nlat = seq // L
    nch = t_all // L
    nctx = nch - nlat
    fwd = lambda i: jnp.where(i < nctx, nlat + i, i - nctx)
    bwd = lambda i: nch - 1 - i

    nb = MLSTM_BATCH if bsz % MLSTM_BATCH == 0 else 1

    def specs(chunk_of, d):
        return [pl.BlockSpec((nb, w, L), lambda b, i: (b, 0, chunk_of(i))),
                pl.BlockSpec((nb, L, w), lambda b, i: (b, chunk_of(i), 0)),
                pl.BlockSpec((nb, w, L), lambda b, i: (b, 0, chunk_of(i))),
                pl.BlockSpec((nb, 8, L), lambda b, i: (b, d, chunk_of(i))),
                pl.BlockSpec((nb, L, LANE), lambda b, i: (b, chunk_of(i), d))]

    kern = functools.partial(_mlstm_kernel, chunk=L, nb=nb)
    out = jax.ShapeDtypeStruct((bsz, w, t_all), F32)
    nchain = 2 * nb * N_MLSTM_HEADS
    return pl.pallas_call(
        kern, grid=(bsz // nb, nch),
        in_specs=specs(fwd, 0) + specs(bwd, 1),
        out_specs=(pl.BlockSpec((nb, w, L), lambda b, i: (b, 0, fwd(i))),
                   pl.BlockSpec((nb, w, L), lambda b, i: (b, 0, bwd(i)))),
        out_shape=(out, out),
        scratch_shapes=[pltpu.VMEM((nchain, MLSTM_HEAD_DIM + VAUG, LANE), F32),
                        pltpu.VMEM((nchain, 8, L), F32)],
        compiler_params=_cparams(("parallel", "arbitrary")), name="mlstm_scan",
    )(qmT, km, vmT, gT, g, qmT, km, vmT, gT, g)


def _attn_kernel(lam_ref, gs_ref, qT_ref, k_ref, vT_ref, *rest, nch, tk):
    o_ref = rest[-1]
    tq = qT_ref.shape[2]
    qT = qT_ref[0]
    z = jnp.zeros((DIFF_QK_DIM, tq), BF16)
    rhs = jnp.concatenate([jnp.concatenate([qT[:DIFF_QK_DIM], z], axis=0),
                           jnp.concatenate([z, qT[DIFF_QK_DIM:]], axis=0)], axis=1)
    ncb = 2 * tq // ATT_COLS
    ms = [jnp.full((1, ATT_COLS), NEG_BIG, F32) for _ in range(ncb)]
    accs = [jnp.zeros((DIFF_V_DIM + VAUG, ATT_COLS), F32) for _ in range(ncb)]
    sub = tk // ATT_KEYS
    units = [(c, cb) for c in range(nch * sub) for cb in range(ncb)]

    def scores(c, cb):
        return jnp.dot(k_ref[0, c * ATT_KEYS:(c + 1) * ATT_KEYS, :],
                       rhs[:, cb * ATT_COLS:(cb + 1) * ATT_COLS], preferred_element_type=F32)

    pending = [scores(*u) for u in units[:ATT_AHEAD]]
    for i, (c, cb) in enumerate(units):
        sT = pending.pop(0)
        if i + ATT_AHEAD < len(units):
            pending.append(scores(*units[i + ATT_AHEAD]))
        vT = vT_ref[0, c // sub, 0, :, (c % sub) * ATT_KEYS:(c % sub + 1) * ATT_KEYS]
        m_new = jnp.maximum(ms[cb], jnp.max(sT, axis=0, keepdims=True))
        p = jnp.exp2((sT - m_new).astype(BF16))
        alpha = jnp.exp2(ms[cb] - m_new)
        accs[cb] = alpha * accs[cb] + jnp.dot(vT, p, preferred_element_type=F32)
        ms[cb] = m_new
    acc = jnp.concatenate(accs, axis=1)
    l = acc[DIFF_V_DIM:DIFF_V_DIM + 1]
    acc = acc[0:DIFF_V_DIM]

    lv = lam_ref[...]
    lam_init = lv[4:5, 0:1]
    lam = (jnp.exp(jnp.sum(lv[0:1] * lv[1:2], axis=1, keepdims=True))
           - jnp.exp(jnp.sum(lv[2:3] * lv[3:4], axis=1, keepdims=True)) + lam_init)
    oT = acc[:, :tq] / l[:, :tq] - lam * (acc[:, tq:] / l[:, tq:])
    oT = oT * lax.rsqrt(jnp.mean(oT * oT, axis=0, keepdims=True) + EPS) * gs_ref[...] * (1.0 - lam_init)
    o_ref[0] = oT.T.astype(BF16)


def _attention(lamv, gs_col, qaT, ka, vaT, d_prev, out_rows, *, q_tile, q_blk0, n_q, k_rows, k_blk0,
               v_chunks, v_chunk0, v_cols, v_blk0):
    bsz = qaT.shape[0]
    kern = functools.partial(_attn_kernel, nch=v_chunks, tk=v_cols)
    in_specs = [_const_spec(lamv.shape), _const_spec((DIFF_V_DIM, 1)),
                pl.BlockSpec((1, DIFF_V_DIM, q_tile), lambda b, h, i: (b, h, q_blk0 + i)),
                pl.BlockSpec((1, k_rows, DIFF_V_DIM), lambda b, h, i: (b, k_blk0, h)),
                pl.BlockSpec((1, v_chunks, 1, DIFF_V_DIM + VAUG, v_cols),
                             lambda b, h, i: (b, v_chunk0, h, 0, v_blk0))]
    args = [lamv, gs_col, qaT, ka, vaT]
    aliases = {}
    if d_prev is not None:
        in_specs.append(pl.BlockSpec(memory_space=pl.ANY))
        aliases = {len(args): 0}
        args.append(d_prev)
    return pl.pallas_call(
        kern, grid=(bsz, N_DIFF_HEADS, n_q), in_specs=in_specs,
        out_specs=pl.BlockSpec((1, q_tile, DIFF_V_DIM), lambda b, h, i: (b, q_blk0 + i, h)),
        out_shape=jax.ShapeDtypeStruct((bsz, out_rows, DIFF_WIDTH), BF16),
        input_output_aliases=aliases,
        compiler_params=_cparams(("parallel", "parallel", "arbitrary")), name="diff_attn",
    )(*args)


def _conv_kernel(l_ref, c_ref, r_ref, w_ref, b_ref, g_ref, bb_ref, o_ref, buf, shifted, *, seq, tc, nt):
    j = pl.program_id(1)
    start = j * tc
    lvalid = jnp.logical_and(j > 0, start != seq)
    rvalid = jnp.logical_and(j < nt - 1, start + tc != seq)
    hl = CONV_HALO
    buf[0:hl, :] = jnp.where(lvalid, l_ref[0].astype(F32), 0.0)
    buf[hl:hl + tc, :] = c_ref[0].astype(F32)
    buf[hl + tc:2 * hl + tc, :] = jnp.where(rvalid, r_ref[0].astype(F32), 0.0)
    pad = CONV_KERNEL // 2
    sub = 128
    sl = CONV_SHIFTS
    for r in range(sl):
        shifted[r] = buf[pl.ds(r, shifted.shape[1]), :]
    for r0 in range(0, tc, sub):
        accs = [jnp.zeros((sub, LANE), F32) for _ in range(CONV_WIDTH // LANE)]
        for t in range(CONV_KERNEL):
            q, r = divmod(hl - pad + t, sl)
            for cb in range(CONV_WIDTH // LANE):
                cols = slice(cb * LANE, (cb + 1) * LANE)
                accs[cb] = accs[cb] + w_ref[t:t + 1, cols] * shifted[r, pl.ds(r0 + sl * q, sub), cols]
        y = jnp.concatenate(accs, axis=1) + b_ref[...]
        mu = jnp.mean(y, axis=-1, keepdims=True)
        var = jnp.mean(jnp.square(y - mu), axis=-1, keepdims=True)
        z = (y - mu) * lax.rsqrt(var + EPS) * g_ref[...] + bb_ref[...]
        o_ref[0, r0:r0 + sub, :] = (z * _sigmoid(z)).astype(BF16)


def _conv(u, w_dw, b_dw, g_ln, b_ln, *, seq):
    bsz, t_all, cw = u.shape
    tc = CONV_TILE
    nt = t_all // tc
    r = tc // CONV_HALO
    nhalo = t_all // CONV_HALO
    kern = functools.partial(_conv_kernel, seq=seq, tc=tc, nt=nt)
    row = lambda a: a.reshape(1, cw)
    return pl.pallas_call(
        kern, grid=(bsz, nt),
        in_specs=[pl.BlockSpec((1, CONV_HALO, cw), lambda b, j: (b, jnp.maximum(j * r - 1, 0), 0)),
                  pl.BlockSpec((1, tc, cw), lambda b, j: (b, j, 0)),
                  pl.BlockSpec((1, CONV_HALO, cw), lambda b, j: (b, jnp.minimum((j + 1) * r, nhalo - 1), 0)),
                  _const_spec((CONV_KERNEL, cw)), _const_spec((1, cw)), _const_spec((1, cw)),
                  _const_spec((1, cw))],
        out_specs=pl.BlockSpec((1, tc, cw), lambda b, j: (b, j, 0)),
        out_shape=jax.ShapeDtypeStruct((bsz, t_all, cw), BF16),
        scratch_shapes=[pltpu.VMEM((tc + 2 * CONV_HALO, cw), F32),
                        pltpu.VMEM((CONV_SHIFTS, tc + 2 * CONV_HALO - CONV_SHIFTS, cw), F32)],
        compiler_params=_cparams(("parallel", "parallel")), name="conv_mixer",
    )(u, u, u, w_dw, row(b_dw), row(g_ln), row(b_ln))


def _mixout_kernel(x_ref, mod_ref, hf_ref, hb_ref, omT_ref, gm_ref, d_ref, c_ref, wm_ref, wd_ref, wc_ref, o_ref,
                   *, seq, tm, ctx_row):
    b = pl.program_id(0)
    j = pl.program_id(1)
    is_ctx = _is_ctx_rows(j, tm, seq)
    hT = hf_ref[0] + hb_ref[0]
    h4 = hT.reshape(N_MLSTM_HEADS, MLSTM_HEAD_DIM, tm)
    mu = jnp.mean(h4, axis=1, keepdims=True)
    var = jnp.mean(jnp.square(h4 - mu), axis=1, keepdims=True)
    hn = ((h4 - mu) * lax.rsqrt(var + EPS)).reshape(MLSTM_WIDTH, tm)
    mT = _sigmoid(omT_ref[0].astype(F32)) * hn * gm_ref[...]
    m = mT.T.astype(BF16)
    y = (jnp.dot(m, wm_ref[...], preferred_element_type=F32)
         + jnp.dot(d_ref[0], wd_ref[...], preferred_element_type=F32)
         + jnp.dot(c_ref[0], wc_ref[...], preferred_element_type=F32))
    g_l, g_c = _mod_rows(mod_ref, b, ctx_row, 2)
    o_ref[0] = x_ref[0] + jnp.where(is_ctx, g_c, g_l) * y


def _mixout(xa, mod, layer, hTf, hTb, omT, gm_col, d, cx, wm, wd, wc, *, seq):
    bsz, t_all, _ = xa.shape
    tm = TOK_TILE
    nt = t_all // tm
    kern = functools.partial(_mixout_kernel, seq=seq, tm=tm, ctx_row=bsz)
    w = MLSTM_WIDTH
    return pl.pallas_call(
        kern, grid=(bsz, nt),
        in_specs=[pl.BlockSpec((1, tm, D_MODEL), lambda b, j: (b, j, 0)),
                  pl.BlockSpec((1, MOD_ROWS, 6 * D_MODEL), lambda b, j: (layer, 0, 0)),
                  pl.BlockSpec((1, w, tm), lambda b, j: (b, 0, j)),
                  pl.BlockSpec((1, w, tm), lambda b, j: (b, 0, j)),
                  pl.BlockSpec((1, w, tm), lambda b, j: (b, 0, j)),
                  _const_spec((w, 1)),
                  pl.BlockSpec((1, tm, DIFF_WIDTH), lambda b, j: (b, j, 0)),
                  pl.BlockSpec((1, tm, CONV_WIDTH), lambda b, j: (b, j, 0)),
                  _const_spec(wm.shape), _const_spec(wd.shape), _const_spec(wc.shape)],
        out_specs=pl.BlockSpec((1, tm, D_MODEL), lambda b, j: (b, j, 0)),
        out_shape=jax.ShapeDtypeStruct(xa.shape, F32),
        compiler_params=_cparams(("parallel", "parallel")), name="mix_out",
    )(xa, mod, hTf, hTb, omT, gm_col, d, cx, wm, wd, wc)


def _swiglu_partial(hb, wg, wu, wd):
    a = jnp.dot(hb, wg, preferred_element_type=F32)
    u = jnp.dot(hb, wu, preferred_element_type=F32)
    t = (a * _sigmoid(a) * u).astype(BF16)
    return jnp.dot(t, wd, preferred_element_type=F32)


def _ffn_kernel(x_ref, mod_ref, g2_ref, wg_ref, wu_ref, wd_ref, o_ref, hb_scr, acc_scr,
                *, seq, tm, ctx_row, nf):
    b = pl.program_id(0)
    j = pl.program_id(1)
    f = pl.program_id(2)
    is_ctx = _is_ctx_rows(j, tm, seq)

    @pl.when(f == 0)
    def _():
        h = _rms_mod(x_ref[0], g2_ref[...], mod_ref, b, ctx_row, is_ctx, 3, 4)
        hb_scr[...] = h.astype(BF16)
        acc_scr[...] = jnp.zeros(acc_scr.shape, F32)

    acc_scr[...] += _swiglu_partial(hb_scr[...], wg_ref[0], wu_ref[0], wd_ref[0])

    @pl.when(f == nf - 1)
    def _():
        g_l, g_c = _mod_rows(mod_ref, b, ctx_row, 5)
        o_ref[0] = x_ref[0] + jnp.where(is_ctx, g_c, g_l) * acc_scr[...]


def _ffn(xa, mod, layer, g2, wg, wu, wd, ffn_layer, *, seq):
    bsz, t_all, _ = xa.shape
    tm = TOK_TILE
    nt = t_all // tm
    nf = FFN_SPLIT
    tf = wg.shape[2] // nf
    assert tf % LANE == 0 and tf * nf == wg.shape[2]
    kern = functools.partial(_ffn_kernel, seq=seq, tm=tm, ctx_row=bsz, nf=nf)
    return pl.pallas_call(
        kern, grid=(bsz, nt, nf),
        in_specs=[pl.BlockSpec((1, tm, D_MODEL), lambda b, j, f: (b, j, 0)),
                  pl.BlockSpec((1, MOD_ROWS, 6 * D_MODEL), lambda b, j, f: (layer, 0, 0)),
                  _const_spec((1, D_MODEL)),
                  pl.BlockSpec((1, D_MODEL, tf), lambda b, j, f: (ffn_layer, 0, f)),
                  pl.BlockSpec((1, D_MODEL, tf), lambda b, j, f: (ffn_layer, 0, f)),
                  pl.BlockSpec((1, tf, D_MODEL), lambda b, j, f: (ffn_layer, f, 0))],
        out_specs=pl.BlockSpec((1, tm, D_MODEL), lambda b, j, f: (b, j, 0)),
        out_shape=jax.ShapeDtypeStruct(xa.shape, F32),
        scratch_shapes=[pltpu.VMEM((tm, D_MODEL), BF16), pltpu.VMEM((tm, D_MODEL), F32)],
        compiler_params=_cparams(("parallel", "parallel", "arbitrary")), name="ffn_swiglu",
    )(xa, mod, g2, wg, wu, wd)


def _moe_kernel(x_ref, mod_ref, g2_ref, wr_ref, br_ref, wg_ref, wu_ref, wd_ref, o_ref,
                hb_scr, acc_scr, comb_scr, slot_scr, slotT_scr, *, seq, tm, nt, ctx_row, ne, group):
    step = pl.program_id(0)
    e = pl.program_id(1)
    lane = lax.broadcasted_iota(jnp.int32, (tm, LANE), 1)

    for half in range(group):
        tile = step * group + half
        b = tile // nt
        is_ctx = _is_ctx_rows(tile % nt, tm, seq)
        rows = slice(half * tm, (half + 1) * tm)

        @pl.when(e == 0)
        def _():
            h = _rms_mod(x_ref[rows, :], g2_ref[...], mod_ref, b, ctx_row, is_ctx, 3, 4)
            h_hi = h.astype(BF16)
            hb_scr[half] = h_hi
            acc_scr[half] = jnp.zeros((tm, D_MODEL), F32)
            h_lo = (h - h_hi.astype(F32)).astype(BF16)
            logits = (jnp.dot(h_hi, wr_ref[0], preferred_element_type=F32)
                      + jnp.dot(h_lo, wr_ref[0], preferred_element_type=F32)
                      + jnp.dot(h_hi, wr_ref[1], preferred_element_type=F32)) + br_ref[...]
            logits = jnp.where(lane < ne, logits, -jnp.inf)
            ex = jnp.exp(logits - jnp.max(logits, axis=-1, keepdims=True))
            probs = ex / jnp.sum(ex, axis=-1, keepdims=True)
            v1 = jnp.max(probs, axis=-1, keepdims=True)
            i1 = jnp.min(jnp.where(probs == v1, lane, LANE), axis=-1, keepdims=True)
            rest = jnp.where(lane == i1, -1.0, probs)
            v2 = jnp.max(rest, axis=-1, keepdims=True)
            i2 = jnp.min(jnp.where(rest == v2, lane, LANE), axis=-1, keepdims=True)
            tot = v1 + v2
            comb_scr[half] = jnp.where(lane == i1, v1 / tot, 0.0) + jnp.where(lane == i2, v2 / tot, 0.0)
            sel = jnp.logical_or(lane == i1, lane == i2)
            ri = lax.broadcasted_iota(jnp.int32, (tm, tm), 0)
            ci = lax.broadcasted_iota(jnp.int32, (tm, tm), 1)
            before = jnp.where(ci < ri, 1.0, 0.0).astype(BF16)
            rank = jnp.dot(before, jnp.where(sel, 1.0, 0.0).astype(BF16), preferred_element_type=F32)
            slot = jnp.where(sel, rank, -1.0)
            slot_scr[half] = slot
            slotT_scr[half] = slot.T

        onlane = lane == e
        cw = jnp.sum(jnp.where(onlane, comb_scr[half], 0.0), axis=-1, keepdims=True)
        slot_c = jnp.max(jnp.where(onlane, slot_scr[half], -1.0), axis=-1, keepdims=True)
        slot_r = slotT_scr[half, pl.ds(e, 1), :]
        count = (jnp.max(slot_r) + 1.0).astype(jnp.int32)

        def run_block(base, size):
            row_i = lax.broadcasted_iota(jnp.int32, (size, 1), 0).astype(F32)
            col_i = lax.broadcasted_iota(jnp.int32, (1, size), 1).astype(F32)
            gather = jnp.where(slot_r == base + row_i, 1.0, 0.0).astype(BF16)
            xs = jnp.dot(gather, hb_scr[half], preferred_element_type=F32).astype(BF16)
            y = _swiglu_partial(xs, wg_ref[0, 0], wu_ref[0, 0], wd_ref[0, 0])
            scatter = jnp.where(slot_c == base + col_i, 1.0, 0.0).astype(BF16)
            acc_scr[half] += cw * jnp.dot(scatter, y.astype(BF16), preferred_element_type=F32)

        n_loop = jnp.maximum((count - MOE_BLOCKS[-1] + MOE_LOOP_BLOCK - 1) // MOE_LOOP_BLOCK, 0)

        def body(i, carry):
            run_block((i * MOE_LOOP_BLOCK).astype(F32), MOE_LOOP_BLOCK)
            return carry
        lax.fori_loop(0, n_loop, body, 0)
        done = n_loop * MOE_LOOP_BLOCK
        rest = count - done
        lo = 0
        for size in MOE_BLOCKS:
            pl.when(jnp.logical_and(rest > lo, rest <= size))(
                functools.partial(run_block, done.astype(F32), size))
            lo = size

        @pl.when(e == ne - 1)
        def _():
            g_l, g_c = _mod_rows(mod_ref, b, ctx_row, 5)
            o_ref[rows, :] = x_ref[rows, :] + jnp.where(is_ctx, g_c, g_l) * acc_scr[half]


def _moe(xa, mod, layer, g2, wr, br, wg, wu, wd, moe_layer, *, seq):
    bsz, t_all, _ = xa.shape
    tm = TOK_TILE
    nt = t_all // tm
    group = MOE_GROUP
    assert (bsz * nt) % group == 0
    ne, fe = wg.shape[1], wg.shape[3]
    kern = functools.partial(_moe_kernel, seq=seq, tm=tm, nt=nt, ctx_row=bsz, ne=ne, group=group)
    out = pl.pallas_call(
        kern, grid=(bsz * nt // group, ne),
        in_specs=[pl.BlockSpec((group * tm, D_MODEL), lambda s, e: (s, 0)),
                  pl.BlockSpec((1, MOD_ROWS, 6 * D_MODEL), lambda s, e: (layer, 0, 0)),
                  _const_spec((1, D_MODEL)), _const_spec(wr.shape), _const_spec(br.shape),
                  pl.BlockSpec((1, 1, D_MODEL, fe), lambda s, e: (moe_layer, e, 0, 0)),
                  pl.BlockSpec((1, 1, D_MODEL, fe), lambda s, e: (moe_layer, e, 0, 0)),
                  pl.BlockSpec((1, 1, fe, D_MODEL), lambda s, e: (moe_layer, e, 0, 0))],
        out_specs=pl.BlockSpec((group * tm, D_MODEL), lambda s, e: (s, 0)),
        out_shape=jax.ShapeDtypeStruct((bsz * t_all, D_MODEL), F32),
        scratch_shapes=[pltpu.VMEM((group, tm, D_MODEL), BF16), pltpu.VMEM((group, tm, D_MODEL), F32),
                        pltpu.VMEM((group, tm, LANE), F32), pltpu.VMEM((group, tm, LANE), F32),
                        pltpu.VMEM((group, LANE, tm), F32)],
        compiler_params=_cparams(("parallel", "arbitrary")), name="moe_swiglu",
    )(xa.reshape(bsz * t_all, D_MODEL), mod, g2, wr, br, wg, wu, wd)
    return out.reshape(xa.shape)


def _final_kernel(x_ref, g_ref, o_ref):
    x = x_ref[0]
    o_ref[0] = x * lax.rsqrt(jnp.mean(x * x, axis=-1, keepdims=True) + EPS) * g_ref[...]


def _final_norm(xa, g, *, seq):
    bsz = xa.shape[0]
    tf = 512
    return pl.pallas_call(
        _final_kernel, grid=(bsz, seq // tf),
        in_specs=[pl.BlockSpec((1, tf, D_MODEL), lambda b, j: (b, j, 0)), _const_spec((1, D_MODEL))],
        out_specs=pl.BlockSpec((1, tf, D_MODEL), lambda b, j: (b, j, 0)),
        out_shape=jax.ShapeDtypeStruct((bsz, seq, D_MODEL), F32),
        compiler_params=_cparams(("parallel", "parallel")), name="final_norm",
    )(xa, g.reshape(1, D_MODEL))


def _rope_tables(seq, t_all):
    pos = np.arange(seq)
    per_axis = DIFF_QK_DIM // 2
    inv = (ROPE_BASE ** (-np.arange(0, per_axis, 2, dtype=np.float32) / per_axis)).astype(np.float32)
    rowp = (pos // GRID_W).astype(np.float32)
    colp = (pos % GRID_W).astype(np.float32)
    ang = np.stack([rowp[:, None] * inv, colp[:, None] * inv], axis=1).astype(np.float64)
    cos = np.concatenate([np.cos(ang), np.ones((t_all - seq, 2, 16))], axis=0)
    sin = np.concatenate([np.sin(ang), np.zeros((t_all - seq, 2, 16))], axis=0)
    qscale = (DIFF_QK_DIM ** -0.5) * LOG2E
    ropeT = np.concatenate([cos.reshape(t_all, 32).T, sin.reshape(t_all, 32).T], axis=0) * qscale
    cos64 = np.concatenate([cos[:, 0], cos[:, 0], cos[:, 1], cos[:, 1]], axis=-1)
    sin64 = np.concatenate([-sin[:, 0], sin[:, 0], -sin[:, 1], sin[:, 1]], axis=-1)
    ropeR = np.concatenate([cos64, cos64, sin64, sin64], axis=-1)
    return jnp.asarray(ropeT, F32), jnp.asarray(ropeR, F32)


def _swap_perm():
    idx = np.arange(2 * N_DIFF_HEADS * DIFF_QK_DIM)
    return np.where((idx % 32) < 16, idx + 16, idx - 16)


def _prep_inproj_weights(w_in_l, b_gates_l):
    offs = np.cumsum((0,) + IN_SPLITS)
    col = lambda i: w_in_l[:, offs[i]:offs[i + 1]]
    mq, mk, mv, mo, gt, aq, ak, av, cv = (col(i) for i in range(9))
    nh = N_MLSTM_HEADS
    wTm = jnp.concatenate([mq, mv, mo], axis=1).T.astype(BF16)
    wTg = gt.T.astype(BF16)
    bgT = b_gates_l.reshape(4 * nh, 1).astype(F32)
    wkm = (mk * (MLSTM_HEAD_DIM ** -0.5)).astype(BF16)
    zpad = jnp.zeros((D_MODEL, LANE - 2 * nh), F32)
    wg = jnp.concatenate([gt[:, 0:2 * nh], zpad, gt[:, 2 * nh:4 * nh], zpad], axis=1).astype(BF16)
    bpad = jnp.zeros((LANE - 2 * nh,), F32)
    bg = jnp.concatenate([b_gates_l[0:2 * nh], bpad, b_gates_l[2 * nh:4 * nh], bpad]).reshape(1, 2 * LANE)
    wTaq = aq.T.astype(BF16)
    wak = jnp.concatenate([ak, ak[:, _swap_perm()]], axis=1).astype(BF16)
    wTav = av.T.astype(BF16)
    wcv = cv.astype(BF16)
    return (wTm, wTg, bgT, wkm, wg, bg, wTaq, wak, wTav, wcv)


def kernel(x, c, ctx, c_ctx, w_mod, b_mod, g_norm1, w_in, b_gates, g_mlstm, lambda_q1, lambda_k1,
           lambda_q2, lambda_k2, g_subln, w_dw, b_dw, g_conv_ln, b_conv_ln, w_out, g_norm2,
           w_ffn_gate, w_ffn_up, w_ffn_down, w_router, b_router, w_exp_gate, w_exp_up, w_exp_down,
           g_final):
    bsz, seq, _ = x.shape
    nctx = ctx.shape[1]
    t_all = seq + nctx
    depth = w_mod.shape[0]
    assert nctx == CTX_LEN == MLSTM_CHUNK and bsz + 1 <= MOD_ROWS
    assert t_all % TOK_TILE == 0 and seq % Q_TILE == 0 and seq % CONV_TILE == 0 and seq % GRID_W == 0

    xa = jnp.concatenate([x, ctx], axis=1)
    cond = jnp.concatenate([c, c_ctx[None, :], jnp.zeros((MOD_ROWS - bsz - 1, D_MODEL), F32)], axis=0)
    mod = _mod_table(cond, w_mod, b_mod)
    ropeT, ropeR = _rope_tables(seq, t_all)
    nkc = t_all // TOK_TILE
    weg, weu, wed = w_exp_gate.astype(BF16), w_exp_up.astype(BF16), w_exp_down.astype(BF16)
    wfg, wfu, wfd = w_ffn_gate.astype(BF16), w_ffn_up.astype(BF16), w_ffn_down.astype(BF16)

    for l in range(depth):
        wts = _prep_inproj_weights(w_in[l], b_gates[l])
        (qmT, km, vmT, omT, gT, g, qaT, ka, vaT, u) = _inproj(
            xa, mod, l, g_norm1[l].reshape(1, D_MODEL), ropeT, ropeR, wts, seq=seq)

        hTf, hTb = _mlstm(qmT, km, vmT, gT, g, seq=seq)

        lam_init = 0.8 - 0.6 * math.exp(-0.3 * l)
        lamv = jnp.zeros((8, LANE), F32).at[0:4, 0:DIFF_QK_DIM].set(
            jnp.stack([lambda_q1[l], lambda_k1[l], lambda_q2[l], lambda_k2[l]]).astype(F32))
        lamv = lamv.at[4, :].set(lam_init)
        gs_col = g_subln[l].reshape(DIFF_V_DIM, 1).astype(F32)
        d = _attention(lamv, gs_col, qaT, ka, vaT, None, t_all, q_tile=Q_TILE, q_blk0=0,
                       n_q=seq // Q_TILE, k_rows=t_all, k_blk0=0, v_chunks=nkc, v_chunk0=0,
                       v_cols=TOK_TILE, v_blk0=0)
        d = _attention(lamv, gs_col, qaT, ka, vaT, d, t_all, q_tile=nctx, q_blk0=seq // nctx, n_q=1,
                       k_rows=nctx, k_blk0=seq // nctx, v_chunks=1, v_chunk0=nkc - 1,
                       v_cols=nctx, v_blk0=TOK_TILE // nctx - 1)

        cx = _conv(u, w_dw[l], b_dw[l], g_conv_ln[l], b_conv_ln[l], seq=seq)

        wo = w_out[l].astype(BF16)
        xa = _mixout(xa, mod, l, hTf, hTb, omT, g_mlstm[l].reshape(MLSTM_WIDTH, 1).astype(F32), d, cx,
                     wo[0:MLSTM_WIDTH], wo[MLSTM_WIDTH:MLSTM_WIDTH + DIFF_WIDTH],
                     wo[MLSTM_WIDTH + DIFF_WIDTH:], seq=seq)

        jj = l // 2
        g2 = g_norm2[l].reshape(1, D_MODEL)
        if l % 2 == 0:
            xa = _ffn(xa, mod, l, g2, wfg, wfu, wfd, jj, seq=seq)
        else:
            wr = jnp.concatenate([w_router[jj], jnp.zeros((D_MODEL, LANE - N_EXPERTS), F32)], axis=1)
            wr_hi = wr.astype(BF16)
            wr = jnp.stack([wr_hi, (wr - wr_hi.astype(F32)).astype(BF16)])
            br = jnp.concatenate([b_router[jj], jnp.zeros((LANE - N_EXPERTS,), F32)]).reshape(1, LANE)
            xa = _moe(xa, mod, l, g2, wr, br, weg, weu, wed, jj, seq=seq)

    return _final_norm(xa, g_final, seq=seq)
```

```python
import functools
import math

import jax
import jax.numpy as jnp
import numpy as np
from jax import lax
from jax.experimental import pallas as pl
from jax.experimental.pallas import tpu as pltpu

F32 = jnp.float32
BF16 = jnp.bfloat16
HIGHEST = lax.Precision.HIGHEST

D_MODEL = 1024
DEPTH = 4
GRID_W = 64
CTX_LEN = 256
N_MLSTM_HEADS = 4
MLSTM_HEAD_DIM = 64
MLSTM_WIDTH = N_MLSTM_HEADS * MLSTM_HEAD_DIM
N_DIFF_HEADS = 4
DIFF_QK_DIM = 64
DIFF_V_DIM = 2 * DIFF_QK_DIM
DIFF_WIDTH = N_DIFF_HEADS * DIFF_V_DIM
ROPE_BASE = 10000.0
CONV_WIDTH = 256
CONV_KERNEL = 31
IN_SPLITS = (MLSTM_WIDTH, MLSTM_WIDTH, MLSTM_WIDTH, MLSTM_WIDTH, 4 * N_MLSTM_HEADS,
             2 * N_DIFF_HEADS * DIFF_QK_DIM, 2 * N_DIFF_HEADS * DIFF_QK_DIM, DIFF_WIDTH,
             2 * CONV_WIDTH)
D_FF = 2816
N_EXPERTS = 8
D_FF_EXPERT = 1408
EPS = 1e-6
M_INIT = -1e30
NEG_BIG = -1e30
LOG2E = 1.4426950408889634

LANE = 128
V7X_VMEM_LIMIT = 56 * 1024 * 1024
TOK_TILE = 768
MLSTM_CHUNK = 256
CONV_TILE = 256
CONV_HALO = 16
CONV_SHIFTS = 8
Q_TILE = 1024
MOD_ROWS = 8
VAUG = 16
MOE_BLOCKS = (128, 192, 256, 320, 384)
MOE_LOOP_BLOCK = 256
FFN_SPLIT = 2
MOE_GROUP = 2
MLSTM_BATCH = 4
ATT_KEYS = 256
ATT_COLS = 256
ATT_AHEAD = 4


def _cparams(sem):
    return pltpu.CompilerParams(dimension_semantics=sem, vmem_limit_bytes=V7X_VMEM_LIMIT)


def _sigmoid(v):
    return 1.0 / (1.0 + jnp.exp(-v))


def _log_sigmoid(v):
    return jnp.minimum(v, 0.0) - jnp.log(1.0 + jnp.exp(-jnp.abs(v)))


def _mod_rows(mod_ref, b, ctx_row, k):
    lat = mod_ref[0, pl.ds(b, 1), k * D_MODEL:(k + 1) * D_MODEL]
    ctx = mod_ref[0, ctx_row:ctx_row + 1, k * D_MODEL:(k + 1) * D_MODEL]
    return lat, ctx


def _is_ctx_rows(j, tm, seq):
    rows = j * tm + lax.broadcasted_iota(jnp.int32, (tm, 1), 0)
    return rows >= seq


def _rms_mod(x, g, mod_ref, b, ctx_row, is_ctx, k_shift, k_scale):
    y = x * lax.rsqrt(jnp.mean(x * x, axis=-1, keepdims=True) + EPS) * g
    sh_l, sh_c = _mod_rows(mod_ref, b, ctx_row, k_shift)
    sc_l, sc_c = _mod_rows(mod_ref, b, ctx_row, k_scale)
    shift = jnp.where(is_ctx, sh_c, sh_l)
    scale = jnp.where(is_ctx, sc_c, sc_l)
    return y * (1.0 + scale) + shift


def _mod_kernel(cond_ref, w_ref, b_ref, o_ref):
    c = cond_ref[...]
    s = c * _sigmoid(c)
    o_ref[0] = jnp.dot(s, w_ref[0], preferred_element_type=F32, precision=HIGHEST) + b_ref[0]


def _mod_table(cond, w_mod, b_mod):
    depth = w_mod.shape[0]
    n = w_mod.shape[2] // D_MODEL
    return pl.pallas_call(
        _mod_kernel,
        grid=(depth, n),
        in_specs=[pl.BlockSpec((MOD_ROWS, D_MODEL), lambda l, c: (0, 0)),
                  pl.BlockSpec((1, D_MODEL, D_MODEL), lambda l, c: (l, 0, c)),
                  pl.BlockSpec((1, 1, D_MODEL), lambda l, c: (l, 0, c))],
        out_specs=pl.BlockSpec((1, MOD_ROWS, D_MODEL), lambda l, c: (l, 0, c)),
        out_shape=jax.ShapeDtypeStruct((depth, MOD_ROWS, n * D_MODEL), F32),
        compiler_params=_cparams(("parallel", "parallel")),
        name="mod_table",
    )(cond, w_mod, b_mod.reshape(depth, 1, n * D_MODEL))


def _inproj_kernel(x_ref, mod_ref, g1_ref, ropeT_ref, ropeR_ref,
                   wTm_ref, wTg_ref, bgT_ref, wkm_ref, wg_ref, bg_ref,
                   wTaq_ref, wak_ref, wTav_ref, wcv_ref,
                   qmT_o, km_o, vmT_o, omT_o, gT_o, g_o, qaT_o, ka_o, vaT_o, u_o,
                   *, seq, tm, ctx_row):
    b = pl.program_id(0)
    j = pl.program_id(1)
    is_ctx = _is_ctx_rows(j, tm, seq)
    h = _rms_mod(x_ref[0], g1_ref[...], mod_ref, b, ctx_row, is_ctx, 0, 1)
    hb = h.astype(BF16)
    hT = h.T.astype(BF16)

    mT = jnp.dot(wTm_ref[...], hT, preferred_element_type=F32)
    w = MLSTM_WIDTH
    qmT_o[0] = mT[0:w].astype(BF16)
    vmT_o[0] = mT[w:2 * w].astype(BF16)
    omT_o[0] = mT[2 * w:3 * w].astype(BF16)
    gT = jnp.dot(wTg_ref[...], hT, preferred_element_type=F32) + bgT_ref[...]
    rowi = lax.broadcasted_iota(jnp.int32, gT.shape, 0)
    gT_o[0] = jnp.where((rowi % 8) >= 4, _log_sigmoid(gT), gT)
    km_o[0] = jnp.dot(hb, wkm_ref[...], preferred_element_type=F32).astype(BF16)
    g = jnp.dot(hb, wg_ref[...], preferred_element_type=F32) + bg_ref[...]
    lanei = lax.broadcasted_iota(jnp.int32, g.shape, 1) % LANE
    g_o[0] = jnp.where((lanei >= 4) & (lanei < 8), _log_sigmoid(g), g)

    qT = jnp.dot(wTaq_ref[...], hT, preferred_element_type=F32)
    for grp in range(2 * N_DIFF_HEADS * 2):
        ax = grp % 2
        cos = ropeT_ref[ax * 16:(ax + 1) * 16, :]
        sin = ropeT_ref[32 + ax * 16:32 + (ax + 1) * 16, :]
        x1 = qT[grp * 32:grp * 32 + 16]
        x2 = qT[grp * 32 + 16:grp * 32 + 32]
        qaT_o[0, grp * 32:grp * 32 + 16, :] = (x1 * cos - x2 * sin).astype(BF16)
        qaT_o[0, grp * 32 + 16:grp * 32 + 32, :] = (x2 * cos + x1 * sin).astype(BF16)
    kk = jnp.dot(hb, wak_ref[...], preferred_element_type=F32)
    cosr = ropeR_ref[:, 0:LANE]
    sinr = ropeR_ref[:, LANE:2 * LANE]
    nk = 2 * N_DIFF_HEADS * DIFF_QK_DIM
    for sl in range(nk // LANE):
        k0 = kk[:, sl * LANE:(sl + 1) * LANE]
        k1 = kk[:, nk + sl * LANE:nk + (sl + 1) * LANE]
        ka_o[0, :, sl * LANE:(sl + 1) * LANE] = (k0 * cosr + k1 * sinr).astype(BF16)
    vT = jnp.dot(wTav_ref[...], hT, preferred_element_type=F32)
    for hh in range(N_DIFF_HEADS):
        vaT_o[0, 0, hh, 0:DIFF_V_DIM, :] = vT[hh * DIFF_V_DIM:(hh + 1) * DIFF_V_DIM].astype(BF16)
        vaT_o[0, 0, hh, DIFF_V_DIM:DIFF_V_DIM + VAUG, :] = jnp.ones((VAUG, tm), BF16)
    cv = jnp.dot(hb, wcv_ref[...], preferred_element_type=F32)
    u_o[0] = (cv[:, :CONV_WIDTH] * _sigmoid(cv[:, CONV_WIDTH:])).astype(BF16)


def _const_spec(shape):
    nd = len(shape)
    return pl.BlockSpec(shape, lambda *_: (0,) * nd)


def _inproj(xa, mod, layer, g1, ropeT, ropeR, wts, *, seq):
    bsz, t_all, _ = xa.shape
    tm = TOK_TILE
    nt = t_all // tm
    kern = functools.partial(_inproj_kernel, seq=seq, tm=tm, ctx_row=bsz)
    w = MLSTM_WIDTH
    out_shapes = (
        jax.ShapeDtypeStruct((bsz, w, t_all), BF16),
        jax.ShapeDtypeStruct((bsz, t_all, w), BF16),
        jax.ShapeDtypeStruct((bsz, w, t_all), BF16),
        jax.ShapeDtypeStruct((bsz, w, t_all), BF16),
        jax.ShapeDtypeStruct((bsz, 16, t_all), F32),
        jax.ShapeDtypeStruct((bsz, t_all, 2 * LANE), F32),
        jax.ShapeDtypeStruct((bsz, DIFF_WIDTH, t_all), BF16),
        jax.ShapeDtypeStruct((bsz, t_all, DIFF_WIDTH), BF16),
        jax.ShapeDtypeStruct((bsz, nt, N_DIFF_HEADS, DIFF_V_DIM + VAUG, tm), BF16),
        jax.ShapeDtypeStruct((bsz, t_all, CONV_WIDTH), BF16),
    )
    fm = lambda rows: pl.BlockSpec((1, rows, tm), lambda b, j: (b, 0, j))
    tk = lambda cols: pl.BlockSpec((1, tm, cols), lambda b, j: (b, j, 0))
    out_specs = (fm(w), tk(w), fm(w), fm(w), fm(16), tk(2 * LANE), fm(DIFF_WIDTH), tk(DIFF_WIDTH),
                 pl.BlockSpec((1, 1, N_DIFF_HEADS, DIFF_V_DIM + VAUG, tm), lambda b, j: (b, j, 0, 0, 0)),
                 tk(CONV_WIDTH))
    in_specs = [
        pl.BlockSpec((1, tm, D_MODEL), lambda b, j: (b, j, 0)),
        pl.BlockSpec((1, MOD_ROWS, 6 * D_MODEL), lambda b, j: (layer, 0, 0)),
        _const_spec((1, D_MODEL)),
        pl.BlockSpec((64, tm), lambda b, j: (0, j)),
        pl.BlockSpec((tm, 2 * LANE), lambda b, j: (j, 0)),
    ] + [_const_spec(a.shape) for a in wts]
    return pl.pallas_call(
        kern, grid=(bsz, nt), in_specs=in_specs, out_specs=out_specs, out_shape=out_shapes,
        compiler_params=_cparams(("parallel", "parallel")), name="inproj",
    )(xa, mod, g1, ropeT, ropeR, *wts)


def _mlstm_kernel(qf_ref, kf_ref, vf_ref, gTf_ref, gf_ref, qb_ref, kb_ref, vb_ref, gTb_ref, gb_ref,
                  hf_o, hb_o, c_scr, m_scr, *, chunk, nb):
    step = pl.program_id(1)
    L = chunk
    hd = MLSTM_HEAD_DIM
    nh = N_MLSTM_HEADS

    @pl.when(step == 0)
    def _():
        c_scr[...] = jnp.zeros(c_scr.shape, F32)
        m_scr[...] = jnp.full(m_scr.shape, M_INIT, F32)

    si = lax.broadcasted_iota(jnp.int32, (L, L), 0)
    ti = lax.broadcasted_iota(jnp.int32, (L, L), 1)
    ones_rows = jnp.ones((VAUG, L), BF16)
    zero64 = jnp.zeros((hd, L), BF16)

    dirs = ((qf_ref, kf_ref, vf_ref, gTf_ref, gf_ref, hf_o), (qb_ref, kb_ref, vb_ref, gTb_ref, gb_ref, hb_o))
    masks = []
    for d in range(2):
        causal = (si <= ti) if d == 0 else (si >= ti)
        tri = jnp.where(causal, 1.0, 0.0).astype(F32)
        masks.append((causal, tri, tri.T))

    chains = []
    for bb, d in [(bb, d) for bb in range(nb) for d in range(2)]:
        qT_ref, k_ref, vT_ref, gT_ref, g_ref, h_o = dirs[d]
        causal, tri, tri_t = masks[d]
        gT = gT_ref[bb]
        gc = g_ref[bb]
        b_rows = jnp.dot(gT, tri, preferred_element_type=F32, precision=HIGHEST)
        b_cols = jnp.dot(tri_t, gc, preferred_element_type=F32, precision=HIGHEST)
        totals = jnp.sum(gT, axis=1, keepdims=True)
        for hh in range(nh):
            pair, half = hh // 2, hh % 2
            q_h = qT_ref[bb, hh * hd:(hh + 1) * hd, :]
            q_msk = jnp.concatenate([q_h, zero64] if half == 0 else [zero64, q_h], axis=0)
            k_pair = k_ref[bb, :, pair * 2 * hd:(pair + 1) * 2 * hd]
            v_aug = jnp.concatenate([vT_ref[bb, hh * hd:(hh + 1) * hd, :], ones_rows], axis=0)
            idx = (bb * 2 + d) * nh + hh
            c_st = c_scr[idx]
            sT = jnp.dot(k_pair, q_msk, preferred_element_type=F32)
            cq = jnp.dot(c_st.astype(BF16), q_msk, preferred_element_type=F32)
            chains.append(dict(idx=idx, hh=hh, bb=bb, h_o=h_o, causal=causal, k_pair=k_pair, v_aug=v_aug,
                               c_st=c_st, sT=sT, cq=cq, li_row=gT[hh:hh + 1], b_row=b_rows[4 + hh:5 + hh],
                               total=totals[4 + hh:5 + hh],
                               a_col=gc[:, hh:hh + 1] - b_cols[:, 4 + hh:5 + hh]))

    for ch in chains:
        hh, b_row = ch["hh"], ch["b_row"]
        m_st = m_scr[ch["idx"], 0:1, :]
        dmat = jnp.where(ch["causal"], b_row + ch["a_col"], -jnp.inf)
        inter = b_row + m_st
        m_t = jnp.maximum(inter, jnp.max(dmat, axis=0, keepdims=True))
        wT = jnp.exp(dmat - m_t) * ch["sT"]
        e_inter = jnp.exp(inter - m_t)
        intra = jnp.dot(ch["v_aug"], wT.astype(BF16), preferred_element_type=F32)
        cq = ch["cq"]
        num = e_inter * cq[0:hd] + intra[0:hd]
        den = e_inter * cq[hd:hd + 1] + jnp.sum(wT, axis=0, keepdims=True)
        ch["h_o"][ch["bb"], hh * hd:(hh + 1) * hd, :] = num / jnp.maximum(jnp.abs(den), jnp.exp(-m_t))

        total = ch["total"]
        g_row = total - b_row + ch["li_row"]
        m_prev = m_st[:, 0:1]
        m_new = jnp.maximum(total + m_prev, jnp.max(g_row, axis=1, keepdims=True))
        e_old = jnp.exp(total + m_prev - m_new)
        e_g = jnp.exp(g_row - m_new)
        upd = jnp.dot((ch["v_aug"].astype(F32) * e_g).astype(BF16), ch["k_pair"],
                      preferred_element_type=F32)
        c_scr[ch["idx"]] = e_old * ch["c_st"] + upd
        m_scr[ch["idx"]] = jnp.broadcast_to(m_new, m_scr.shape[1:])


def _mlstm(qmT, km, vmT, gT, g, *, seq):
    bsz, w, t_all = qmT.shape
    L = MLSTM_CHUNK
    nlat = seq // L
    nch = t_all // L
    nctx = nch - nlat
    fwd = lambda i: jnp.where(i < nctx, nlat + i, i - nctx)
    bwd = lambda i: nch - 1 - i

    nb = MLSTM_BATCH if bsz % MLSTM_BATCH == 0 else 1

    def specs(chunk_of, d):
        return [pl.BlockSpec((nb, w, L), lambda b, i: (b, 0, chunk_of(i))),
                pl.BlockSpec((nb, L, w), lambda b, i: (b, chunk_of(i), 0)),
                pl.BlockSpec((nb, w, L), lambda b, i: (b, 0, chunk_of(i))),
                pl.BlockSpec((nb, 8, L), lambda b, i: (b, d, chunk_of(i))),
                pl.BlockSpec((nb, L, LANE), lambda b, i: (b, chunk_of(i), d))]

    kern = functools.partial(_mlstm_kernel, chunk=L, nb=nb)
    out = jax.ShapeDtypeStruct((bsz, w, t_all), F32)
    nchain = 2 * nb * N_MLSTM_HEADS
    return pl.pallas_call(
        kern, grid=(bsz // nb, nch),
        in_specs=specs(fwd, 0) + specs(bwd, 1),
        out_specs=(pl.BlockSpec((nb, w, L), lambda b, i: (b, 0, fwd(i))),
                   pl.BlockSpec((nb, w, L), lambda b, i: (b, 0, bwd(i)))),
        out_shape=(out, out),
        scratch_shapes=[pltpu.VMEM((nchain, MLSTM_HEAD_DIM + VAUG, LANE), F32),
                        pltpu.VMEM((nchain, 8, L), F32)],
        compiler_params=_cparams(("parallel", "arbitrary")), name="mlstm_scan",
    )(qmT, km, vmT, gT, g, qmT, km, vmT, gT, g)


def _attn_kernel(lam_ref, gs_ref, qT_ref, k_ref, vT_ref, *rest, nch, tk):
    o_ref = rest[-1]
    tq = qT_ref.shape[2]
    qT = qT_ref[0]
    z = jnp.zeros((DIFF_QK_DIM, tq), BF16)
    rhs = jnp.concatenate([jnp.concatenate([qT[:DIFF_QK_DIM], z], axis=0),
                           jnp.concatenate([z, qT[DIFF_QK_DIM:]], axis=0)], axis=1)
    ncb = 2 * tq // ATT_COLS
    ms = [jnp.full((1, ATT_COLS), NEG_BIG, F32) for _ in range(ncb)]
    accs = [jnp.zeros((DIFF_V_DIM + VAUG, ATT_COLS), F32) for _ in range(ncb)]
    sub = tk // ATT_KEYS
    units = [(c, cb) for c in range(nch * sub) for cb in range(ncb)]

    def scores(c, cb):
        return jnp.dot(k_ref[0, c * ATT_KEYS:(c + 1) * ATT_KEYS, :],
                       rhs[:, cb * ATT_COLS:(cb + 1) * ATT_COLS], preferred_element_type=F32)

    pending = [scores(*u) for u in units[:ATT_AHEAD]]
    for i, (c, cb) in enumerate(units):
        sT = pending.pop(0)
        if i + ATT_AHEAD < len(units):
            pending.append(scores(*units[i + ATT_AHEAD]))
        vT = vT_ref[0, c // sub, 0, :, (c % sub) * ATT_KEYS:(c % sub + 1) * ATT_KEYS]
        m_new = jnp.maximum(ms[cb], jnp.max(sT, axis=0, keepdims=True))
        p = jnp.exp2((sT - m_new).astype(BF16))
        alpha = jnp.exp2(ms[cb] - m_new)
        accs[cb] = alpha * accs[cb] + jnp.dot(vT, p, preferred_element_type=F32)
        ms[cb] = m_new
    acc = jnp.concatenate(accs, axis=1)
    l = acc[DIFF_V_DIM:DIFF_V_DIM + 1]
    acc = acc[0:DIFF_V_DIM]

    lv = lam_ref[...]
    lam_init = lv[4:5, 0:1]
    lam = (jnp.exp(jnp.sum(lv[0:1] * lv[1:2], axis=1, keepdims=True))
           - jnp.exp(jnp.sum(lv[2:3] * lv[3:4], axis=1, keepdims=True)) + lam_init)
    oT = acc[:, :tq] / l[:, :tq] - lam * (acc[:, tq:] / l[:, tq:])
    oT = oT * lax.rsqrt(jnp.mean(oT * oT, axis=0, keepdims=True) + EPS) * gs_ref[...] * (1.0 - lam_init)
    o_ref[0] = oT.T.astype(BF16)


def _attention(lamv, gs_col, qaT, ka, vaT, d_prev, out_rows, *, q_tile, q_blk0, n_q, k_rows, k_blk0,
               v_chunks, v_chunk0, v_cols, v_blk0):
    bsz = qaT.shape[0]
    kern = functools.partial(_attn_kernel, nch=v_chunks, tk=v_cols)
    in_specs = [_const_spec(lamv.shape), _const_spec((DIFF_V_DIM, 1)),
                pl.BlockSpec((1, DIFF_V_DIM, q_tile), lambda b, h, i: (b, h, q_blk0 + i)),
                pl.BlockSpec((1, k_rows, DIFF_V_DIM), lambda b, h, i: (b, k_blk0, h)),
                pl.BlockSpec((1, v_chunks, 1, DIFF_V_DIM + VAUG, v_cols),
                             lambda b, h, i: (b, v_chunk0, h, 0, v_blk0))]
    args = [lamv, gs_col, qaT, ka, vaT]
    aliases = {}
    if d_prev is not None:
        in_specs.append(pl.BlockSpec(memory_space=pl.ANY))
        aliases = {len(args): 0}
        args.append(d_prev)
    return pl.pallas_call(
        kern, grid=(bsz, N_DIFF_HEADS, n_q), in_specs=in_specs,
        out_specs=pl.BlockSpec((1, q_tile, DIFF_V_DIM), lambda b, h, i: (b, q_blk0 + i, h)),
        out_shape=jax.ShapeDtypeStruct((bsz, out_rows, DIFF_WIDTH), BF16),
        input_output_aliases=aliases,
        compiler_params=_cparams(("parallel", "parallel", "arbitrary")), name="diff_attn",
    )(*args)


def _conv_kernel(l_ref, c_ref, r_ref, w_ref, b_ref, g_ref, bb_ref, o_ref, buf, shifted, *, seq, tc, nt):
    j = pl.program_id(1)
    start = j * tc
    lvalid = jnp.logical_and(j > 0, start != seq)
    rvalid = jnp.logical_and(j < nt - 1, start + tc != seq)
    hl = CONV_HALO
    buf[0:hl, :] = jnp.where(lvalid, l_ref[0].astype(F32), 0.0)
    buf[hl:hl + tc, :] = c_ref[0].astype(F32)
    buf[hl + tc:2 * hl + tc, :] = jnp.where(rvalid, r_ref[0].astype(F32), 0.0)
    pad = CONV_KERNEL // 2
    sub = 128
    sl = CONV_SHIFTS
    for r in range(sl):
        shifted[r] = buf[pl.ds(r, shifted.shape[1]), :]
    for r0 in range(0, tc, sub):
        accs = [jnp.zeros((sub, LANE), F32) for _ in range(CONV_WIDTH // LANE)]
        for t in range(CONV_KERNEL):
            q, r = divmod(hl - pad + t, sl)
            for cb in range(CONV_WIDTH // LANE):
                cols = slice(cb * LANE, (cb + 1) * LANE)
                accs[cb] = accs[cb] + w_ref[t:t + 1, cols] * shifted[r, pl.ds(r0 + sl * q, sub), cols]
        y = jnp.concatenate(accs, axis=1) + b_ref[...]
        mu = jnp.mean(y, axis=-1, keepdims=True)
        var = jnp.mean(jnp.square(y - mu), axis=-1, keepdims=True)
        z = (y - mu) * lax.rsqrt(var + EPS) * g_ref[...] + bb_ref[...]
        o_ref[0, r0:r0 + sub, :] = (z * _sigmoid(z)).astype(BF16)


def _conv(u, w_dw, b_dw, g_ln, b_ln, *, seq):
    bsz, t_all, cw = u.shape
    tc = CONV_TILE
    nt = t_all // tc
    r = tc // CONV_HALO
    nhalo = t_all // CONV_HALO
    kern = functools.partial(_conv_kernel, seq=seq, tc=tc, nt=nt)
    row = lambda a: a.reshape(1, cw)
    return pl.pallas_call(
        kern, grid=(bsz, nt),
        in_specs=[pl.BlockSpec((1, CONV_HALO, cw), lambda b, j: (b, jnp.maximum(j * r - 1, 0), 0)),
                  pl.BlockSpec((1, tc, cw), lambda b, j: (b, j, 0)),
                  pl.BlockSpec((1, CONV_HALO, cw), lambda b, j: (b, jnp.minimum((j + 1) * r, nhalo - 1), 0)),
                  _const_spec((CONV_KERNEL, cw)), _const_spec((1, cw)), _const_spec((1, cw)),
                  _const_spec((1, cw))],
        out_specs=pl.BlockSpec((1, tc, cw), lambda b, j: (b, j, 0)),
        out_shape=jax.ShapeDtypeStruct((bsz, t_all, cw), BF16),
        scratch_shapes=[pltpu.VMEM((tc + 2 * CONV_HALO, cw), F32),
                        pltpu.VMEM((CONV_SHIFTS, tc + 2 * CONV_HALO - CONV_SHIFTS, cw), F32)],
        compiler_params=_cparams(("parallel", "parallel")), name="conv_mixer",
    )(u, u, u, w_dw, row(b_dw), row(g_ln), row(b_ln))


def _mixout_kernel(x_ref, mod_ref, hf_ref, hb_ref, omT_ref, gm_ref, d_ref, c_ref, wm_ref, wd_ref, wc_ref, o_ref,
                   *, seq, tm, ctx_row):
    b = pl.program_id(0)
    j = pl.program_id(1)
    is_ctx = _is_ctx_rows(j, tm, seq)
    hT = hf_ref[0] + hb_ref[0]
    h4 = hT.reshape(N_MLSTM_HEADS, MLSTM_HEAD_DIM, tm)
    mu = jnp.mean(h4, axis=1, keepdims=True)
    var = jnp.mean(jnp.square(h4 - mu), axis=1, keepdims=True)
    hn = ((h4 - mu) * lax.rsqrt(var + EPS)).reshape(MLSTM_WIDTH, tm)
    mT = _sigmoid(omT_ref[0].astype(F32)) * hn * gm_ref[...]
    m = mT.T.astype(BF16)
    y = (jnp.dot(m, wm_ref[...], preferred_element_type=F32)
         + jnp.dot(d_ref[0], wd_ref[...], preferred_element_type=F32)
         + jnp.dot(c_ref[0], wc_ref[...], preferred_element_type=F32))
    g_l, g_c = _mod_rows(mod_ref, b, ctx_row, 2)
    o_ref[0] = x_ref[0] + jnp.where(is_ctx, g_c, g_l) * y


def _mixout(xa, mod, layer, hTf, hTb, omT, gm_col, d, cx, wm, wd, wc, *, seq):
    bsz, t_all, _ = xa.shape
    tm = TOK_TILE
    nt = t_all // tm
    kern = functools.partial(_mixout_kernel, seq=seq, tm=tm, ctx_row=bsz)
    w = MLSTM_WIDTH
    return pl.pallas_call(
        kern, grid=(bsz, nt),
        in_specs=[pl.BlockSpec((1, tm, D_MODEL), lambda b, j: (b, j, 0)),
                  pl.BlockSpec((1, MOD_ROWS, 6 * D_MODEL), lambda b, j: (layer, 0, 0)),
                  pl.BlockSpec((1, w, tm), lambda b, j: (b, 0, j)),
                  pl.BlockSpec((1, w, tm), lambda b, j: (b, 0, j)),
                  pl.BlockSpec((1, w, tm), lambda b, j: (b, 0, j)),
                  _const_spec((w, 1)),
                  pl.BlockSpec((1, tm, DIFF_WIDTH), lambda b, j: (b, j, 0)),
                  pl.BlockSpec((1, tm, CONV_WIDTH), lambda b, j: (b, j, 0)),
                  _const_spec(wm.shape), _const_spec(wd.shape), _const_spec(wc.shape)],
        out_specs=pl.BlockSpec((1, tm, D_MODEL), lambda b, j: (b, j, 0)),
        out_shape=jax.ShapeDtypeStruct(xa.shape, F32),
        compiler_params=_cparams(("parallel", "parallel")), name="mix_out",
    )(xa, mod, hTf, hTb, omT, gm_col, d, cx, wm, wd, wc)


def _swiglu_partial(hb, wg, wu, wd):
    a = jnp.dot(hb, wg, preferred_element_type=F32)
    u = jnp.dot(hb, wu, preferred_element_type=F32)
    t = (a * _sigmoid(a) * u).astype(BF16)
    return jnp.dot(t, wd, preferred_element_type=F32)


def _ffn_kernel(x_ref, mod_ref, g2_ref, wg_ref, wu_ref, wd_ref, o_ref, hb_scr, acc_scr,
                *, seq, tm, ctx_row, nf):
    b = pl.program_id(0)
    j = pl.program_id(1)
    f = pl.program_id(2)
    is_ctx = _is_ctx_rows(j, tm, seq)

    @pl.when(f == 0)
    def _():
        h = _rms_mod(x_ref[0], g2_ref[...], mod_ref, b, ctx_row, is_ctx, 3, 4)
        hb_scr[...] = h.astype(BF16)
        acc_scr[...] = jnp.zeros(acc_scr.shape, F32)

    acc_scr[...] += _swiglu_partial(hb_scr[...], wg_ref[0], wu_ref[0], wd_ref[0])

    @pl.when(f == nf - 1)
    def _():
        g_l, g_c = _mod_rows(mod_ref, b, ctx_row, 5)
        o_ref[0] = x_ref[0] + jnp.where(is_ctx, g_c, g_l) * acc_scr[...]


def _ffn(xa, mod, layer, g2, wg, wu, wd, ffn_layer, *, seq):
    bsz, t_all, _ = xa.shape
    tm = TOK_TILE
    nt = t_all // tm
    nf = FFN_SPLIT
    tf = wg.shape[2] // nf
    assert tf % LANE == 0 and tf * nf == wg.shape[2]
    kern = functools.partial(_ffn_kernel, seq=seq, tm=tm, ctx_row=bsz, nf=nf)
    return pl.pallas_call(
        kern, grid=(bsz, nt, nf),
        in_specs=[pl.BlockSpec((1, tm, D_MODEL), lambda b, j, f: (b, j, 0)),
                  pl.BlockSpec((1, MOD_ROWS, 6 * D_MODEL), lambda b, j, f: (layer, 0, 0)),
                  _const_spec((1, D_MODEL)),
                  pl.BlockSpec((1, D_MODEL, tf), lambda b, j, f: (ffn_layer, 0, f)),
                  pl.BlockSpec((1, D_MODEL, tf), lambda b, j, f: (ffn_layer, 0, f)),
                  pl.BlockSpec((1, tf, D_MODEL), lambda b, j, f: (ffn_layer, f, 0))],
        out_specs=pl.BlockSpec((1, tm, D_MODEL), lambda b, j, f: (b, j, 0)),
        out_shape=jax.ShapeDtypeStruct(xa.shape, F32),
        scratch_shapes=[pltpu.VMEM((tm, D_MODEL), BF16), pltpu.VMEM((tm, D_MODEL), F32)],
        compiler_params=_cparams(("parallel", "parallel", "arbitrary")), name="ffn_swiglu",
    )(xa, mod, g2, wg, wu, wd)


def _moe_kernel(x_ref, mod_ref, g2_ref, wr_ref, br_ref, wg_ref, wu_ref, wd_ref, o_ref,
                hb_scr, comb_scr, slot_scr, slotT_scr, *, seq, tm, nt, ctx_row, ne, group):
    step = pl.program_id(0)
    e = pl.program_id(1)
    lane = lax.broadcasted_iota(jnp.int32, (tm, LANE), 1)

    for half in range(group):
        tile = step * group + half
        b = tile // nt
        is_ctx = _is_ctx_rows(tile % nt, tm, seq)
        rows = slice(half * tm, (half + 1) * tm)

        @pl.when(e == 0)
        def _():
            h = _rms_mod(x_ref[rows, :], g2_ref[...], mod_ref, b, ctx_row, is_ctx, 3, 4)
            h_hi = h.astype(BF16)
            hb_scr[half] = h_hi
            o_ref[rows, :] = jnp.zeros((tm, D_MODEL), F32)
            h_lo = (h - h_hi.astype(F32)).astype(BF16)
            logits = (jnp.dot(h_hi, wr_ref[0], preferred_element_type=F32)
                      + jnp.dot(h_lo, wr_ref[0], preferred_element_type=F32)
                      + jnp.dot(h_hi, wr_ref[1], preferred_element_type=F32)) + br_ref[...]
            logits = jnp.where(lane < ne, logits, -jnp.inf)
            ex = jnp.exp(logits - jnp.max(logits, axis=-1, keepdims=True))
            probs = ex / jnp.sum(ex, axis=-1, keepdims=True)
            v1 = jnp.max(probs, axis=-1, keepdims=True)
            i1 = jnp.min(jnp.where(probs == v1, lane, LANE), axis=-1, keepdims=True)
            rest = jnp.where(lane == i1, -1.0, probs)
            v2 = jnp.max(rest, axis=-1, keepdims=True)
            i2 = jnp.min(jnp.where(rest == v2, lane, LANE), axis=-1, keepdims=True)
            tot = v1 + v2
            comb_scr[half] = jnp.where(lane == i1, v1 / tot, 0.0) + jnp.where(lane == i2, v2 / tot, 0.0)
            sel = jnp.logical_or(lane == i1, lane == i2)
            ri = lax.broadcasted_iota(jnp.int32, (tm, tm), 0)
            ci = lax.broadcasted_iota(jnp.int32, (tm, tm), 1)
            before = jnp.where(ci < ri, 1.0, 0.0).astype(BF16)
            rank = jnp.dot(before, jnp.where(sel, 1.0, 0.0).astype(BF16), preferred_element_type=F32)
            slot = jnp.where(sel, rank, -1.0)
            slot_scr[half] = slot
            slotT_scr[half] = slot.T

        onlane = lane == e
        cw = jnp.sum(jnp.where(onlane, comb_scr[half], 0.0), axis=-1, keepdims=True)
        slot_c = jnp.max(jnp.where(onlane, slot_scr[half], -1.0), axis=-1, keepdims=True)
        slot_r = slotT_scr[half, pl.ds(e, 1), :]
        count = (jnp.max(slot_r) + 1.0).astype(jnp.int32)

        def run_block(base, size):
            row_i = lax.broadcasted_iota(jnp.int32, (size, 1), 0).astype(F32)
            col_i = lax.broadcasted_iota(jnp.int32, (1, size), 1).astype(F32)
            gather = jnp.where(slot_r == base + row_i, 1.0, 0.0).astype(BF16)
            xs = jnp.dot(gather, hb_scr[half], preferred_element_type=F32).astype(BF16)
            y = _swiglu_partial(xs, wg_ref[0, 0], wu_ref[0, 0], wd_ref[0, 0])
            scatter = jnp.where(slot_c == base + col_i, 1.0, 0.0).astype(BF16)
            o_ref[rows, :] += cw * jnp.dot(scatter, y.astype(BF16), preferred_element_type=F32)

        n_loop = jnp.maximum((count - MOE_BLOCKS[-1] + MOE_LOOP_BLOCK - 1) // MOE_LOOP_BLOCK, 0)

        def body(i, carry):
            run_block((i * MOE_LOOP_BLOCK).astype(F32), MOE_LOOP_BLOCK)
            return carry
        lax.fori_loop(0, n_loop, body, 0)
        done = n_loop * MOE_LOOP_BLOCK
        rest = count - done
        lo = 0
        for size in MOE_BLOCKS:
            pl.when(jnp.logical_and(rest > lo, rest <= size))(
                functools.partial(run_block, done.astype(F32), size))
            lo = size

        @pl.when(e == ne - 1)
        def _():
            g_l, g_c = _mod_rows(mod_ref, b, ctx_row, 5)
            o_ref[rows, :] = x_ref[rows, :] + jnp.where(is_ctx, g_c, g_l) * o_ref[rows, :]


def _moe(xa, mod, layer, g2, wr, br, wg, wu, wd, moe_layer, *, seq):
    bsz, t_all, _ = xa.shape
    tm = TOK_TILE
    nt = t_all // tm
    group = MOE_GROUP
    assert (bsz * nt) % group == 0
    ne, fe = wg.shape[1], wg.shape[3]
    kern = functools.partial(_moe_kernel, seq=seq, tm=tm, nt=nt, ctx_row=bsz, ne=ne, group=group)
    out = pl.pallas_call(
        kern, grid=(bsz * nt // group, ne),
        in_specs=[pl.BlockSpec((group * tm, D_MODEL), lambda s, e: (s, 0), pipeline_mode=pl.Buffered(1)),
                  pl.BlockSpec((1, MOD_ROWS, 6 * D_MODEL), lambda s, e: (layer, 0, 0)),
                  _const_spec((1, D_MODEL)), _const_spec(wr.shape), _const_spec(br.shape),
                  pl.BlockSpec((1, 1, D_MODEL, fe), lambda s, e: (moe_layer, e, 0, 0)),
                  pl.BlockSpec((1, 1, D_MODEL, fe), lambda s, e: (moe_layer, e, 0, 0)),
                  pl.BlockSpec((1, 1, fe, D_MODEL), lambda s, e: (moe_layer, e, 0, 0))],
        out_specs=pl.BlockSpec((group * tm, D_MODEL), lambda s, e: (s, 0)),
        out_shape=jax.ShapeDtypeStruct((bsz * t_all, D_MODEL), F32),
        scratch_shapes=[pltpu.VMEM((group, tm, D_MODEL), BF16),
                        pltpu.VMEM((group, tm, LANE), F32), pltpu.VMEM((group, tm, LANE), F32),
                        pltpu.VMEM((group, LANE, tm), F32)],
        compiler_params=_cparams(("parallel", "arbitrary")), name="moe_swiglu",
    )(xa.reshape(bsz * t_all, D_MODEL), mod, g2, wr, br, wg, wu, wd)
    return out.reshape(xa.shape)


def _final_kernel(x_ref, g_ref, o_ref):
    x = x_ref[0]
    o_ref[0] = x * lax.rsqrt(jnp.mean(x * x, axis=-1, keepdims=True) + EPS) * g_ref[...]


def _final_norm(xa, g, *, seq):
    bsz = xa.shape[0]
    tf = 512
    return pl.pallas_call(
        _final_kernel, grid=(bsz, seq // tf),
        in_specs=[pl.BlockSpec((1, tf, D_MODEL), lambda b, j: (b, j, 0)), _const_spec((1, D_MODEL))],
        out_specs=pl.BlockSpec((1, tf, D_MODEL), lambda b, j: (b, j, 0)),
        out_shape=jax.ShapeDtypeStruct((bsz, seq, D_MODEL), F32),
        compiler_params=_cparams(("parallel", "parallel")), name="final_norm",
    )(xa, g.reshape(1, D_MODEL))


def _rope_tables(seq, t_all):
    pos = np.arange(seq)
    per_axis = DIFF_QK_DIM // 2
    inv = (ROPE_BASE ** (-np.arange(0, per_axis, 2, dtype=np.float32) / per_axis)).astype(np.float32)
    rowp = (pos // GRID_W).astype(np.float32)
    colp = (pos % GRID_W).astype(np.float32)
    ang = np.stack([rowp[:, None] * inv, colp[:, None] * inv], axis=1).astype(np.float64)
    cos = np.concatenate([np.cos(ang), np.ones((t_all - seq, 2, 16))], axis=0)
    sin = np.concatenate([np.sin(ang), np.zeros((t_all - seq, 2, 16))], axis=0)
    qscale = (DIFF_QK_DIM ** -0.5) * LOG2E
    ropeT = np.concatenate([cos.reshape(t_all, 32).T, sin.reshape(t_all, 32).T], axis=0) * qscale
    cos64 = np.concatenate([cos[:, 0], cos[:, 0], cos[:, 1], cos[:, 1]], axis=-1)
    sin64 = np.concatenate([-sin[:, 0], sin[:, 0], -sin[:, 1], sin[:, 1]], axis=-1)
    ropeR = np.concatenate([cos64, cos64, sin64, sin64], axis=-1)
    return jnp.asarray(ropeT, F32), jnp.asarray(ropeR, F32)


def _swap_perm():
    idx = np.arange(2 * N_DIFF_HEADS * DIFF_QK_DIM)
    return np.where((idx % 32) < 16, idx + 16, idx - 16)


def _prep_inproj_weights(w_in_l, b_gates_l):
    offs = np.cumsum((0,) + IN_SPLITS)
    col = lambda i: w_in_l[:, offs[i]:offs[i + 1]]
    mq, mk, mv, mo, gt, aq, ak, av, cv = (col(i) for i in range(9))
    nh = N_MLSTM_HEADS
    wTm = jnp.concatenate([mq, mv, mo], axis=1).T.astype(BF16)
    wTg = gt.T.astype(BF16)
    bgT = b_gates_l.reshape(4 * nh, 1).astype(F32)
    wkm = (mk * (MLSTM_HEAD_DIM ** -0.5)).astype(BF16)
    zpad = jnp.zeros((D_MODEL, LANE - 2 * nh), F32)
    wg = jnp.concatenate([gt[:, 0:2 * nh], zpad, gt[:, 2 * nh:4 * nh], zpad], axis=1).astype(BF16)
    bpad = jnp.zeros((LANE - 2 * nh,), F32)
    bg = jnp.concatenate([b_gates_l[0:2 * nh], bpad, b_gates_l[2 * nh:4 * nh], bpad]).reshape(1, 2 * LANE)
    wTaq = aq.T.astype(BF16)
    wak = jnp.concatenate([ak, ak[:, _swap_perm()]], axis=1).astype(BF16)
    wTav = av.T.astype(BF16)
    wcv = cv.astype(BF16)
    return (wTm, wTg, bgT, wkm, wg, bg, wTaq, wak, wTav, wcv)


def kernel(x, c, ctx, c_ctx, w_mod, b_mod, g_norm1, w_in, b_gates, g_mlstm, lambda_q1, lambda_k1,
           lambda_q2, lambda_k2, g_subln, w_dw, b_dw, g_conv_ln, b_conv_ln, w_out, g_norm2,
           w_ffn_gate, w_ffn_up, w_ffn_down, w_router, b_router, w_exp_gate, w_exp_up, w_exp_down,
           g_final):
    bsz, seq, _ = x.shape
    nctx = ctx.shape[1]
    t_all = seq + nctx
    depth = w_mod.shape[0]
    assert nctx == CTX_LEN == MLSTM_CHUNK and bsz + 1 <= MOD_ROWS
    assert t_all % TOK_TILE == 0 and seq % Q_TILE == 0 and seq % CONV_TILE == 0 and seq % GRID_W == 0

    xa = jnp.concatenate([x, ctx], axis=1)
    cond = jnp.concatenate([c, c_ctx[None, :], jnp.zeros((MOD_ROWS - bsz - 1, D_MODEL), F32)], axis=0)
    mod = _mod_table(cond, w_mod, b_mod)
    ropeT, ropeR = _rope_tables(seq, t_all)
    nkc = t_all // TOK_TILE
    weg, weu, wed = w_exp_gate.astype(BF16), w_exp_up.astype(BF16), w_exp_down.astype(BF16)
    wfg, wfu, wfd = w_ffn_gate.astype(BF16), w_ffn_up.astype(BF16), w_ffn_down.astype(BF16)

    for l in range(depth):
        wts = _prep_inproj_weights(w_in[l], b_gates[l])
        (qmT, km, vmT, omT, gT, g, qaT, ka, vaT, u) = _inproj(
            xa, mod, l, g_norm1[l].reshape(1, D_MODEL), ropeT, ropeR, wts, seq=seq)

        hTf, hTb = _mlstm(qmT, km, vmT, gT, g, seq=seq)

        lam_init = 0.8 - 0.6 * math.exp(-0.3 * l)
        lamv = jnp.zeros((8, LANE), F32).at[0:4, 0:DIFF_QK_DIM].set(
            jnp.stack([lambda_q1[l], lambda_k1[l], lambda_q2[l], lambda_k2[l]]).astype(F32))
        lamv = lamv.at[4, :].set(lam_init)
        gs_col = g_subln[l].reshape(DIFF_V_DIM, 1).astype(F32)
        d = _attention(lamv, gs_col, qaT, ka, vaT, None, t_all, q_tile=Q_TILE, q_blk0=0,
                       n_q=seq // Q_TILE, k_rows=t_all, k_blk0=0, v_chunks=nkc, v_chunk0=0,
                       v_cols=TOK_TILE, v_blk0=0)
        d = _attention(lamv, gs_col, qaT, ka, vaT, d, t_all, q_tile=nctx, q_blk0=seq // nctx, n_q=1,
                       k_rows=nctx, k_blk0=seq // nctx, v_chunks=1, v_chunk0=nkc - 1,
                       v_cols=nctx, v_blk0=TOK_TILE // nctx - 1)

        cx = _conv(u, w_dw[l], b_dw[l], g_conv_ln[l], b_conv_ln[l], seq=seq)

        wo = w_out[l].astype(BF16)
        xa = _mixout(xa, mod, l, hTf, hTb, omT, g_mlstm[l].reshape(MLSTM_WIDTH, 1).astype(F32), d, cx,
                     wo[0:MLSTM_WIDTH], wo[MLSTM_WIDTH:MLSTM_WIDTH + DIFF_WIDTH],
                     wo[MLSTM_WIDTH + DIFF_WIDTH:], seq=seq)

        jj = l // 2
        g2 = g_norm2[l].reshape(1, D_MODEL)
        if l % 2 == 0:
            xa = _ffn(xa, mod, l, g2, wfg, wfu, wfd, jj, seq=seq)
        else:
            wr = jnp.concatenate([w_router[jj], jnp.zeros((D_MODEL, LANE - N_EXPERTS), F32)], axis=1)
            wr_hi = wr.astype(BF16)
            wr = jnp.stack([wr_hi, (wr - wr_hi.astype(F32)).astype(BF16)])
            br = jnp.concatenate([b_router[jj], jnp.zeros((LANE - N_EXPERTS,), F32)]).reshape(1, LANE)
            xa = _moe(xa, mod, l, g2, wr, br, weg, weu, wed, jj, seq=seq)

    return _final_norm(xa, g_final, seq=seq)
```

```python
import functools
import math

import jax
import jax.numpy as jnp
import numpy as np
from jax import lax
from jax.experimental import pallas as pl
from jax.experimental.pallas import tpu as pltpu

F32 = jnp.float32
BF16 = jnp.bfloat16
HIGHEST = lax.Precision.HIGHEST

D_MODEL = 1024
DEPTH = 4
GRID_W = 64
CTX_LEN = 256
N_MLSTM_HEADS = 4
MLSTM_HEAD_DIM = 64
MLSTM_WIDTH = N_MLSTM_HEADS * MLSTM_HEAD_DIM
N_DIFF_HEADS = 4
DIFF_QK_DIM = 64
DIFF_V_DIM = 2 * DIFF_QK_DIM
DIFF_WIDTH = N_DIFF_HEADS * DIFF_V_DIM
ROPE_BASE = 10000.0
CONV_WIDTH = 256
CONV_KERNEL = 31
IN_SPLITS = (MLSTM_WIDTH, MLSTM_WIDTH, MLSTM_WIDTH, MLSTM_WIDTH, 4 * N_MLSTM_HEADS,
             2 * N_DIFF_HEADS * DIFF_QK_DIM, 2 * N_DIFF_HEADS * DIFF_QK_DIM, DIFF_WIDTH,
             2 * CONV_WIDTH)
D_FF = 2816
N_EXPERTS = 8
D_FF_EXPERT = 1408
EPS = 1e-6
M_INIT = -1e30
NEG_BIG = -1e30
LOG2E = 1.4426950408889634

LANE = 128
V7X_VMEM_LIMIT = 56 * 1024 * 1024
TOK_TILE = 768
MLSTM_CHUNK = 256
CONV_TILE = 256
CONV_HALO = 16
CONV_SHIFTS = 8
Q_TILE = 1024
MOD_ROWS = 8
VAUG = 16
MOE_BLOCKS = (128, 192, 256, 320, 384)
MOE_LOOP_BLOCK = 256
FFN_SPLIT = 2
MOE_GROUP = 1
MLSTM_BATCH = 4
ATT_KEYS = 256
ATT_COLS = 256
ATT_AHEAD = 4


def _cparams(sem):
    return pltpu.CompilerParams(dimension_semantics=sem, vmem_limit_bytes=V7X_VMEM_LIMIT)


def _sigmoid(v):
    return 1.0 / (1.0 + jnp.exp(-v))


def _log_sigmoid(v):
    return jnp.minimum(v, 0.0) - jnp.log(1.0 + jnp.exp(-jnp.abs(v)))


def _mod_rows(mod_ref, b, ctx_row, k):
    lat = mod_ref[0, pl.ds(b, 1), k * D_MODEL:(k + 1) * D_MODEL]
    ctx = mod_ref[0, ctx_row:ctx_row + 1, k * D_MODEL:(k + 1) * D_MODEL]
    return lat, ctx


def _is_ctx_rows(j, tm, seq):
    rows = j * tm + lax.broadcasted_iota(jnp.int32, (tm, 1), 0)
    return rows >= seq


def _rms_mod(x, g, mod_ref, b, ctx_row, is_ctx, k_shift, k_scale):
    y = x * lax.rsqrt(jnp.mean(x * x, axis=-1, keepdims=True) + EPS) * g
    sh_l, sh_c = _mod_rows(mod_ref, b, ctx_row, k_shift)
    sc_l, sc_c = _mod_rows(mod_ref, b, ctx_row, k_scale)
    shift = jnp.where(is_ctx, sh_c, sh_l)
    scale = jnp.where(is_ctx, sc_c, sc_l)
    return y * (1.0 + scale) + shift


def _mod_kernel(cond_ref, w_ref, b_ref, o_ref):
    c = cond_ref[...]
    s = c * _sigmoid(c)
    o_ref[0] = jnp.dot(s, w_ref[0], preferred_element_type=F32, precision=HIGHEST) + b_ref[0]


def _mod_table(cond, w_mod, b_mod):
    depth = w_mod.shape[0]
    n = w_mod.shape[2] // D_MODEL
    return pl.pallas_call(
        _mod_kernel,
        grid=(depth, n),
        in_specs=[pl.BlockSpec((MOD_ROWS, D_MODEL), lambda l, c: (0, 0)),
                  pl.BlockSpec((1, D_MODEL, D_MODEL), lambda l, c: (l, 0, c)),
                  pl.BlockSpec((1, 1, D_MODEL), lambda l, c: (l, 0, c))],
        out_specs=pl.BlockSpec((1, MOD_ROWS, D_MODEL), lambda l, c: (l, 0, c)),
        out_shape=jax.ShapeDtypeStruct((depth, MOD_ROWS, n * D_MODEL), F32),
        compiler_params=_cparams(("parallel", "parallel")),
        name="mod_table",
    )(cond, w_mod, b_mod.reshape(depth, 1, n * D_MODEL))


def _inproj_kernel(x_ref, mod_ref, g1_ref, ropeT_ref, ropeR_ref,
                   wTm_ref, wTg_ref, bgT_ref, wkm_ref, wg_ref, bg_ref,
                   wTaq_ref, wak_ref, wTav_ref, wcv_ref,
                   qmT_o, km_o, vmT_o, omT_o, gT_o, g_o, qaT_o, ka_o, vaT_o, u_o,
                   *, seq, tm, ctx_row):
    b = pl.program_id(0)
    j = pl.program_id(1)
    is_ctx = _is_ctx_rows(j, tm, seq)
    h = _rms_mod(x_ref[0], g1_ref[...], mod_ref, b, ctx_row, is_ctx, 0, 1)
    hb = h.astype(BF16)
    hT = h.T.astype(BF16)

    mT = jnp.dot(wTm_ref[...], hT, preferred_element_type=F32)
    w = MLSTM_WIDTH
    qmT_o[0] = mT[0:w].astype(BF16)
    vmT_o[0] = mT[w:2 * w].astype(BF16)
    omT_o[0] = mT[2 * w:3 * w].astype(BF16)
    gT = jnp.dot(wTg_ref[...], hT, preferred_element_type=F32) + bgT_ref[...]
    rowi = lax.broadcasted_iota(jnp.int32, gT.shape, 0)
    gT_o[0] = jnp.where((rowi % 8) >= 4, _log_sigmoid(gT), gT)
    km_o[0] = jnp.dot(hb, wkm_ref[...], preferred_element_type=F32).astype(BF16)
    g = jnp.dot(hb, wg_ref[...], preferred_element_type=F32) + bg_ref[...]
    lanei = lax.broadcasted_iota(jnp.int32, g.shape, 1) % LANE
    g_o[0] = jnp.where((lanei >= 4) & (lanei < 8), _log_sigmoid(g), g)

    qT = jnp.dot(wTaq_ref[...], hT, preferred_element_type=F32)
    for grp in range(2 * N_DIFF_HEADS * 2):
        ax = grp % 2
        cos = ropeT_ref[ax * 16:(ax + 1) * 16, :]
        sin = ropeT_ref[32 + ax * 16:32 + (ax + 1) * 16, :]
        x1 = qT[grp * 32:grp * 32 + 16]
        x2 = qT[grp * 32 + 16:grp * 32 + 32]
        qaT_o[0, grp * 32:grp * 32 + 16, :] = (x1 * cos - x2 * sin).astype(BF16)
        qaT_o[0, grp * 32 + 16:grp * 32 + 32, :] = (x2 * cos + x1 * sin).astype(BF16)
    kk = jnp.dot(hb, wak_ref[...], preferred_element_type=F32)
    cosr = ropeR_ref[:, 0:LANE]
    sinr = ropeR_ref[:, LANE:2 * LANE]
    half = DIFF_QK_DIM // 4
    first_half = (lax.broadcasted_iota(jnp.int32, (tm, LANE), 1) % (2 * half)) < half
    for sl in range(2 * N_DIFF_HEADS * DIFF_QK_DIM // LANE):
        k0 = kk[:, sl * LANE:(sl + 1) * LANE]
        k1 = jnp.where(first_half, pltpu.roll(k0, LANE - half, axis=1), pltpu.roll(k0, half, axis=1))
        ka_o[0, :, sl * LANE:(sl + 1) * LANE] = (k0 * cosr + k1 * sinr).astype(BF16)
    vT = jnp.dot(wTav_ref[...], hT, preferred_element_type=F32)
    for hh in range(N_DIFF_HEADS):
        vaT_o[0, 0, hh, 0:DIFF_V_DIM, :] = vT[hh * DIFF_V_DIM:(hh + 1) * DIFF_V_DIM].astype(BF16)
        vaT_o[0, 0, hh, DIFF_V_DIM:DIFF_V_DIM + VAUG, :] = jnp.ones((VAUG, tm), BF16)
    cv = jnp.dot(hb, wcv_ref[...], preferred_element_type=F32)
    u_o[0] = (cv[:, :CONV_WIDTH] * _sigmoid(cv[:, CONV_WIDTH:])).astype(BF16)


def _const_spec(shape):
    nd = len(shape)
    return pl.BlockSpec(shape, lambda *_: (0,) * nd)


def _inproj(xa, mod, layer, g1, ropeT, ropeR, wts, *, seq):
    bsz, t_all, _ = xa.shape
    tm = TOK_TILE
    nt = t_all // tm
    kern = functools.partial(_inproj_kernel, seq=seq, tm=tm, ctx_row=bsz)
    w = MLSTM_WIDTH
    out_shapes = (
        jax.ShapeDtypeStruct((bsz, w, t_all), BF16),
        jax.ShapeDtypeStruct((bsz, t_all, w), BF16),
        jax.ShapeDtypeStruct((bsz, w, t_all), BF16),
        jax.ShapeDtypeStruct((bsz, w, t_all), BF16),
        jax.ShapeDtypeStruct((bsz, 16, t_all), F32),
        jax.ShapeDtypeStruct((bsz, t_all, 2 * LANE), F32),
        jax.ShapeDtypeStruct((bsz, DIFF_WIDTH, t_all), BF16),
        jax.ShapeDtypeStruct((bsz, t_all, DIFF_WIDTH), BF16),
        jax.ShapeDtypeStruct((bsz, nt, N_DIFF_HEADS, DIFF_V_DIM + VAUG, tm), BF16),
        jax.ShapeDtypeStruct((bsz, t_all, CONV_WIDTH), BF16),
    )
    fm = lambda rows: pl.BlockSpec((1, rows, tm), lambda b, j: (b, 0, j))
    tk = lambda cols: pl.BlockSpec((1, tm, cols), lambda b, j: (b, j, 0))
    out_specs = (fm(w), tk(w), fm(w), fm(w), fm(16), tk(2 * LANE), fm(DIFF_WIDTH), tk(DIFF_WIDTH),
                 pl.BlockSpec((1, 1, N_DIFF_HEADS, DIFF_V_DIM + VAUG, tm), lambda b, j: (b, j, 0, 0, 0)),
                 tk(CONV_WIDTH))
    in_specs = [
        pl.BlockSpec((1, tm, D_MODEL), lambda b, j: (b, j, 0)),
        pl.BlockSpec((1, MOD_ROWS, 6 * D_MODEL), lambda b, j: (layer, 0, 0)),
        _const_spec((1, D_MODEL)),
        pl.BlockSpec((64, tm), lambda b, j: (0, j)),
        pl.BlockSpec((tm, 2 * LANE), lambda b, j: (j, 0)),
    ] + [_const_spec(a.shape) for a in wts]
    return pl.pallas_call(
        kern, grid=(bsz, nt), in_specs=in_specs, out_specs=out_specs, out_shape=out_shapes,
        compiler_params=_cparams(("parallel", "parallel")), name="inproj",
    )(xa, mod, g1, ropeT, ropeR, *wts)


def _mlstm_kernel(qf_ref, kf_ref, vf_ref, gTf_ref, gf_ref, qb_ref, kb_ref, vb_ref, gTb_ref, gb_ref,
                  hf_o, hb_o, c_scr, m_scr, *, chunk, nb):
    step = pl.program_id(1)
    L = chunk
    hd = MLSTM_HEAD_DIM
    nh = N_MLSTM_HEADS

    @pl.when(step == 0)
    def _():
        c_scr[...] = jnp.zeros(c_scr.shape, F32)
        m_scr[...] = jnp.full(m_scr.shape, M_INIT, F32)

    si = lax.broadcasted_iota(jnp.int32, (L, L), 0)
    ti = lax.broadcasted_iota(jnp.int32, (L, L), 1)
    ones_rows = jnp.ones((VAUG, L), BF16)
    zero64 = jnp.zeros((hd, L), BF16)

    dirs = ((qf_ref, kf_ref, vf_ref, gTf_ref, gf_ref, hf_o), (qb_ref, kb_ref, vb_ref, gTb_ref, gb_ref, hb_o))
    masks = []
    for d in range(2):
        causal = (si <= ti) if d == 0 else (si >= ti)
        tri = jnp.where(causal, 1.0, 0.0).astype(F32)
        masks.append((causal, tri, tri.T))

    chains = []
    for bb, d in [(bb, d) for bb in range(nb) for d in range(2)]:
        qT_ref, k_ref, vT_ref, gT_ref, g_ref, h_o = dirs[d]
        causal, tri, tri_t = masks[d]
        gT = gT_ref[bb]
        gc = g_ref[bb]
        b_rows = jnp.dot(gT, tri, preferred_element_type=F32, precision=HIGHEST)
        b_cols = jnp.dot(tri_t, gc, preferred_element_type=F32, precision=HIGHEST)
        totals = jnp.sum(gT, axis=1, keepdims=True)
        for hh in range(nh):
            pair, half = hh // 2, hh % 2
            q_h = qT_ref[bb, hh * hd:(hh + 1) * hd, :]
            q_msk = jnp.concatenate([q_h, zero64] if half == 0 else [zero64, q_h], axis=0)
            k_pair = k_ref[bb, :, pair * 2 * hd:(pair + 1) * 2 * hd]
            v_aug = jnp.concatenate([vT_ref[bb, hh * hd:(hh + 1) * hd, :], ones_rows], axis=0)
            idx = (bb * 2 + d) * nh + hh
            c_st = c_scr[idx]
            sT = jnp.dot(k_pair, q_msk, preferred_element_type=F32)
            cq = jnp.dot(c_st.astype(BF16), q_msk, preferred_element_type=F32)
            chains.append(dict(idx=idx, hh=hh, bb=bb, h_o=h_o, causal=causal, k_pair=k_pair, v_aug=v_aug,
                               c_st=c_st, sT=sT, cq=cq, li_row=gT[hh:hh + 1], b_row=b_rows[4 + hh:5 + hh],
                               total=totals[4 + hh:5 + hh],
                               a_col=gc[:, hh:hh + 1] - b_cols[:, 4 + hh:5 + hh]))

    for ch in chains:
        hh, b_row = ch["hh"], ch["b_row"]
        m_st = m_scr[ch["idx"], 0:1, :]
        dmat = jnp.where(ch["causal"], b_row + ch["a_col"], -jnp.inf)
        inter = b_row + m_st
        m_t = jnp.maximum(inter, jnp.max(dmat, axis=0, keepdims=True))
        wT = jnp.exp(dmat - m_t) * ch["sT"]
        e_inter = jnp.exp(inter - m_t)
        intra = jnp.dot(ch["v_aug"], wT.astype(BF16), preferred_element_type=F32)
        cq = ch["cq"]
        num = e_inter * cq[0:hd] + intra[0:hd]
        den = e_inter * cq[hd:hd + 1] + jnp.sum(wT, axis=0, keepdims=True)
        ch["h_o"][ch["bb"], hh * hd:(hh + 1) * hd, :] = num / jnp.maximum(jnp.abs(den), jnp.exp(-m_t))

        total = ch["total"]
        g_row = total - b_row + ch["li_row"]
        m_prev = m_st[:, 0:1]
        m_new = jnp.maximum(total + m_prev, jnp.max(g_row, axis=1, keepdims=True))
        e_old = jnp.exp(total + m_prev - m_new)
        e_g = jnp.exp(g_row - m_new)
        upd = jnp.dot((ch["v_aug"].astype(F32) * e_g).astype(BF16), ch["k_pair"],
                      preferred_element_type=F32)
        c_scr[ch["idx"]] = e_old * ch["c_st"] + upd
        m_scr[ch["idx"]] = jnp.broadcast_to(m_new, m_scr.shape[1:])


def _mlstm(qmT, km, vmT, gT, g, *, seq):
    bsz, w, t_all = qmT.shape
    L = MLSTM_CHUNK
    nlat = seq // L
    nch = t_all // L
    nctx = nch - nlat
    fwd = lambda i: jnp.where(i < nctx, nlat + i, i - nctx)
    bwd = lambda i: nch - 1 - i

    nb = MLSTM_BATCH if bsz % MLSTM_BATCH == 0 else 1

    def specs(chunk_of, d):
        return [pl.BlockSpec((nb, w, L), lambda b, i: (b, 0, chunk_of(i))),
                pl.BlockSpec((nb, L, w), lambda b, i: (b, chunk_of(i), 0)),
                pl.BlockSpec((nb, w, L), lambda b, i: (b, 0, chunk_of(i))),
                pl.BlockSpec((nb, 8, L), lambda b, i: (b, d, chunk_of(i))),
                pl.BlockSpec((nb, L, LANE), lambda b, i: (b, chunk_of(i), d))]

    kern = functools.partial(_mlstm_kernel, chunk=L, nb=nb)
    out = jax.ShapeDtypeStruct((bsz, w, t_all), F32)
    nchain = 2 * nb * N_MLSTM_HEADS
    return pl.pallas_call(
        kern, grid=(bsz // nb, nch),
        in_specs=specs(fwd, 0) + specs(bwd, 1),
        out_specs=(pl.BlockSpec((nb, w, L), lambda b, i: (b, 0, fwd(i))),
                   pl.BlockSpec((nb, w, L), lambda b, i: (b, 0, bwd(i)))),
        out_shape=(out, out),
        scratch_shapes=[pltpu.VMEM((nchain, MLSTM_HEAD_DIM + VAUG, LANE), F32),
                        pltpu.VMEM((nchain, 8, L), F32)],
        compiler_params=_cparams(("parallel", "arbitrary")), name="mlstm_scan",
    )(qmT, km, vmT, gT, g, qmT, km, vmT, gT, g)


def _attn_kernel(lam_ref, gs_ref, qT_ref, k_ref, vT_ref, *rest, nch, tk):
    o_ref = rest[-1]
    tq = qT_ref.shape[2]
    qT = qT_ref[0]
    z = jnp.zeros((DIFF_QK_DIM, tq), BF16)
    rhs = jnp.concatenate([jnp.concatenate([qT[:DIFF_QK_DIM], z], axis=0),
                           jnp.concatenate([z, qT[DIFF_QK_DIM:]], axis=0)], axis=1)
    ncb = 2 * tq // ATT_COLS
    ms = [jnp.full((1, ATT_COLS), NEG_BIG, F32) for _ in range(ncb)]
    accs = [jnp.zeros((DIFF_V_DIM + VAUG, ATT_COLS), F32) for _ in range(ncb)]
    sub = tk // ATT_KEYS
    units = [(c, cb) for c in range(nch * sub) for cb in range(ncb)]

    def scores(c, cb):
        return jnp.dot(k_ref[0, c * ATT_KEYS:(c + 1) * ATT_KEYS, :],
                       rhs[:, cb * ATT_COLS:(cb + 1) * ATT_COLS], preferred_element_type=F32)

    pending = [scores(*u) for u in units[:ATT_AHEAD]]
    for i, (c, cb) in enumerate(units):
        sT = pending.pop(0)
        if i + ATT_AHEAD < len(units):
            pending.append(scores(*units[i + ATT_AHEAD]))
        vT = vT_ref[0, c // sub, 0, :, (c % sub) * ATT_KEYS:(c % sub + 1) * ATT_KEYS]
        m_new = jnp.maximum(ms[cb], jnp.max(sT, axis=0, keepdims=True))
        p = jnp.exp2((sT - m_new).astype(BF16))
        alpha = jnp.exp2(ms[cb] - m_new)
        accs[cb] = alpha * accs[cb] + jnp.dot(vT, p, preferred_element_type=F32)
        ms[cb] = m_new
    acc = jnp.concatenate(accs, axis=1)
    l = acc[DIFF_V_DIM:DIFF_V_DIM + 1]
    acc = acc[0:DIFF_V_DIM]

    lv = lam_ref[...]
    lam_init = lv[4:5, 0:1]
    lam = (jnp.exp(jnp.sum(lv[0:1] * lv[1:2], axis=1, keepdims=True))
           - jnp.exp(jnp.sum(lv[2:3] * lv[3:4], axis=1, keepdims=True)) + lam_init)
    oT = acc[:, :tq] / l[:, :tq] - lam * (acc[:, tq:] / l[:, tq:])
    oT = oT * lax.rsqrt(jnp.mean(oT * oT, axis=0, keepdims=True) + EPS) * gs_ref[...] * (1.0 - lam_init)
    o_ref[0] = oT.T.astype(BF16)


def _attention(lamv, gs_col, qaT, ka, vaT, d_prev, out_rows, *, q_tile, q_blk0, n_q, k_rows, k_blk0,
               v_chunks, v_chunk0, v_cols, v_blk0):
    bsz = qaT.shape[0]
    kern = functools.partial(_attn_kernel, nch=v_chunks, tk=v_cols)
    in_specs = [_const_spec(lamv.shape), _const_spec((DIFF_V_DIM, 1)),
                pl.BlockSpec((1, DIFF_V_DIM, q_tile), lambda b, h, i: (b, h, q_blk0 + i)),
                pl.BlockSpec((1, k_rows, DIFF_V_DIM), lambda b, h, i: (b, k_blk0, h)),
                pl.BlockSpec((1, v_chunks, 1, DIFF_V_DIM + VAUG, v_cols),
                             lambda b, h, i: (b, v_chunk0, h, 0, v_blk0))]
    args = [lamv, gs_col, qaT, ka, vaT]
    aliases = {}
    if d_prev is not None:
        in_specs.append(pl.BlockSpec(memory_space=pl.ANY))
        aliases = {len(args): 0}
        args.append(d_prev)
    return pl.pallas_call(
        kern, grid=(bsz, N_DIFF_HEADS, n_q), in_specs=in_specs,
        out_specs=pl.BlockSpec((1, q_tile, DIFF_V_DIM), lambda b, h, i: (b, q_blk0 + i, h)),
        out_shape=jax.ShapeDtypeStruct((bsz, out_rows, DIFF_WIDTH), BF16),
        input_output_aliases=aliases,
        compiler_params=_cparams(("parallel", "parallel", "arbitrary")), name="diff_attn",
    )(*args)


def _conv_kernel(l_ref, c_ref, r_ref, w_ref, b_ref, g_ref, bb_ref, o_ref, buf, shifted, *, seq, tc, nt):
    j = pl.program_id(1)
    start = j * tc
    lvalid = jnp.logical_and(j > 0, start != seq)
    rvalid = jnp.logical_and(j < nt - 1, start + tc != seq)
    hl = CONV_HALO
    buf[0:hl, :] = jnp.where(lvalid, l_ref[0].astype(F32), 0.0)
    buf[hl:hl + tc, :] = c_ref[0].astype(F32)
    buf[hl + tc:2 * hl + tc, :] = jnp.where(rvalid, r_ref[0].astype(F32), 0.0)
    pad = CONV_KERNEL // 2
    sub = 128
    sl = CONV_SHIFTS
    for r in range(sl):
        shifted[r] = buf[pl.ds(r, shifted.shape[1]), :]
    for r0 in range(0, tc, sub):
        accs = [jnp.zeros((sub, LANE), F32) for _ in range(CONV_WIDTH // LANE)]
        for t in range(CONV_KERNEL):
            q, r = divmod(hl - pad + t, sl)
            for cb in range(CONV_WIDTH // LANE):
                cols = slice(cb * LANE, (cb + 1) * LANE)
                accs[cb] = accs[cb] + w_ref[t:t + 1, cols] * shifted[r, pl.ds(r0 + sl * q, sub), cols]
        y = jnp.concatenate(accs, axis=1) + b_ref[...]
        mu = jnp.mean(y, axis=-1, keepdims=True)
        var = jnp.mean(jnp.square(y - mu), axis=-1, keepdims=True)
        z = (y - mu) * lax.rsqrt(var + EPS) * g_ref[...] + bb_ref[...]
        o_ref[0, r0:r0 + sub, :] = (z * _sigmoid(z)).astype(BF16)


def _conv(u, w_dw, b_dw, g_ln, b_ln, *, seq):
    bsz, t_all, cw = u.shape
    tc = CONV_TILE
    nt = t_all // tc
    r = tc // CONV_HALO
    nhalo = t_all // CONV_HALO
    kern = functools.partial(_conv_kernel, seq=seq, tc=tc, nt=nt)
    row = lambda a: a.reshape(1, cw)
    return pl.pallas_call(
        kern, grid=(bsz, nt),
        in_specs=[pl.BlockSpec((1, CONV_HALO, cw), lambda b, j: (b, jnp.maximum(j * r - 1, 0), 0)),
                  pl.BlockSpec((1, tc, cw), lambda b, j: (b, j, 0)),
                  pl.BlockSpec((1, CONV_HALO, cw), lambda b, j: (b, jnp.minimum((j + 1) * r, nhalo - 1), 0)),
                  _const_spec((CONV_KERNEL, cw)), _const_spec((1, cw)), _const_spec((1, cw)),
                  _const_spec((1, cw))],
        out_specs=pl.BlockSpec((1, tc, cw), lambda b, j: (b, j, 0)),
        out_shape=jax.ShapeDtypeStruct((bsz, t_all, cw), BF16),
        scratch_shapes=[pltpu.VMEM((tc + 2 * CONV_HALO, cw), F32),
                        pltpu.VMEM((CONV_SHIFTS, tc + 2 * CONV_HALO - CONV_SHIFTS, cw), F32)],
        compiler_params=_cparams(("parallel", "parallel")), name="conv_mixer",
    )(u, u, u, w_dw, row(b_dw), row(g_ln), row(b_ln))


def _mixout_kernel(x_ref, mod_ref, hf_ref, hb_ref, omT_ref, gm_ref, d_ref, c_ref, wm_ref, wd_ref, wc_ref, o_ref,
                   *, seq, tm, ctx_row):
    b = pl.program_id(0)
    j = pl.program_id(1)
    is_ctx = _is_ctx_rows(j, tm, seq)
    hT = hf_ref[0] + hb_ref[0]
    h4 = hT.reshape(N_MLSTM_HEADS, MLSTM_HEAD_DIM, tm)
    mu = jnp.mean(h4, axis=1, keepdims=True)
    var = jnp.mean(jnp.square(h4 - mu), axis=1, keepdims=True)
    hn = ((h4 - mu) * lax.rsqrt(var + EPS)).reshape(MLSTM_WIDTH, tm)
    mT = _sigmoid(omT_ref[0].astype(F32)) * hn * gm_ref[...]
    m = mT.T.astype(BF16)
    y = (jnp.dot(m, wm_ref[...], preferred_element_type=F32)
         + jnp.dot(d_ref[0], wd_ref[...], preferred_element_type=F32)
         + jnp.dot(c_ref[0], wc_ref[...], preferred_element_type=F32))
    g_l, g_c = _mod_rows(mod_ref, b, ctx_row, 2)
    o_ref[0] = x_ref[0] + jnp.where(is_ctx, g_c, g_l) * y


def _mixout(xa, mod, layer, hTf, hTb, omT, gm_col, d, cx, wm, wd, wc, *, seq):
    bsz, t_all, _ = xa.shape
    tm = TOK_TILE
    nt = t_all // tm
    kern = functools.partial(_mixout_kernel, seq=seq, tm=tm, ctx_row=bsz)
    w = MLSTM_WIDTH
    return pl.pallas_call(
        kern, grid=(bsz, nt),
        in_specs=[pl.BlockSpec((1, tm, D_MODEL), lambda b, j: (b, j, 0)),
                  pl.BlockSpec((1, MOD_ROWS, 6 * D_MODEL), lambda b, j: (layer, 0, 0)),
                  pl.BlockSpec((1, w, tm), lambda b, j: (b, 0, j)),
                  pl.BlockSpec((1, w, tm), lambda b, j: (b, 0, j)),
                  pl.BlockSpec((1, w, tm), lambda b, j: (b, 0, j)),
                  _const_spec((w, 1)),
                  pl.BlockSpec((1, tm, DIFF_WIDTH), lambda b, j: (b, j, 0)),
                  pl.BlockSpec((1, tm, CONV_WIDTH), lambda b, j: (b, j, 0)),
                  _const_spec(wm.shape), _const_spec(wd.shape), _const_spec(wc.shape)],
        out_specs=pl.BlockSpec((1, tm, D_MODEL), lambda b, j: (b, j, 0)),
        out_shape=jax.ShapeDtypeStruct(xa.shape, F32),
        compiler_params=_cparams(("parallel", "parallel")), name="mix_out",
    )(xa, mod, hTf, hTb, omT, gm_col, d, cx, wm, wd, wc)


def _swiglu_partial(hb, wg, wu, wd):
    a = jnp.dot(hb, wg, preferred_element_type=F32)
    u = jnp.dot(hb, wu, preferred_element_type=F32)
    t = (a * _sigmoid(a) * u).astype(BF16)
    return jnp.dot(t, wd, preferred_element_type=F32)


def _ffn_kernel(x_ref, mod_ref, g2_ref, wg_ref, wu_ref, wd_ref, o_ref, hb_scr, acc_scr,
                *, seq, tm, ctx_row, nf):
    b = pl.program_id(0)
    j = pl.program_id(1)
    f = pl.program_id(2)
    is_ctx = _is_ctx_rows(j, tm, seq)

    @pl.when(f == 0)
    def _():
        h = _rms_mod(x_ref[0], g2_ref[...], mod_ref, b, ctx_row, is_ctx, 3, 4)
        hb_scr[...] = h.astype(BF16)
        acc_scr[...] = jnp.zeros(acc_scr.shape, F32)

    acc_scr[...] += _swiglu_partial(hb_scr[...], wg_ref[0], wu_ref[0], wd_ref[0])

    @pl.when(f == nf - 1)
    def _():
        g_l, g_c = _mod_rows(mod_ref, b, ctx_row, 5)
        o_ref[0] = x_ref[0] + jnp.where(is_ctx, g_c, g_l) * acc_scr[...]


def _ffn(xa, mod, layer, g2, wg, wu, wd, ffn_layer, *, seq):
    bsz, t_all, _ = xa.shape
    tm = TOK_TILE
    nt = t_all // tm
    nf = FFN_SPLIT
    tf = wg.shape[2] // nf
    assert tf % LANE == 0 and tf * nf == wg.shape[2]
    kern = functools.partial(_ffn_kernel, seq=seq, tm=tm, ctx_row=bsz, nf=nf)
    return pl.pallas_call(
        kern, grid=(bsz, nt, nf),
        in_specs=[pl.BlockSpec((1, tm, D_MODEL), lambda b, j, f: (b, j, 0)),
                  pl.BlockSpec((1, MOD_ROWS, 6 * D_MODEL), lambda b, j, f: (layer, 0, 0)),
                  _const_spec((1, D_MODEL)),
                  pl.BlockSpec((1, D_MODEL, tf), lambda b, j, f: (ffn_layer, 0, f)),
                  pl.BlockSpec((1, D_MODEL, tf), lambda b, j, f: (ffn_layer, 0, f)),
                  pl.BlockSpec((1, tf, D_MODEL), lambda b, j, f: (ffn_layer, f, 0))],
        out_specs=pl.BlockSpec((1, tm, D_MODEL), lambda b, j, f: (b, j, 0)),
        out_shape=jax.ShapeDtypeStruct(xa.shape, F32),
        scratch_shapes=[pltpu.VMEM((tm, D_MODEL), BF16), pltpu.VMEM((tm, D_MODEL), F32)],
        compiler_params=_cparams(("parallel", "parallel", "arbitrary")), name="ffn_swiglu",
    )(xa, mod, g2, wg, wu, wd)


def _moe_kernel(x_ref, mod_ref, g2_ref, wr_ref, br_ref, wg_ref, wu_ref, wd_ref, o_ref,
                hb_scr, acc_scr, comb_scr, slot_scr, slotT_scr, *, seq, tm, nt, ctx_row, ne, group):
    step = pl.program_id(0)
    e = pl.program_id(1)
    lane = lax.broadcasted_iota(jnp.int32, (tm, LANE), 1)

    for half in range(group):
        tile = step * group + half
        b = tile // nt
        is_ctx = _is_ctx_rows(tile % nt, tm, seq)
        rows = slice(half * tm, (half + 1) * tm)

        @pl.when(e == 0)
        def _():
            h = _rms_mod(x_ref[rows, :], g2_ref[...], mod_ref, b, ctx_row, is_ctx, 3, 4)
            h_hi = h.astype(BF16)
            hb_scr[half] = h_hi
            acc_scr[half] = jnp.zeros((tm, D_MODEL), F32)
            h_lo = (h - h_hi.astype(F32)).astype(BF16)
            logits = (jnp.dot(h_hi, wr_ref[0], preferred_element_type=F32)
                      + jnp.dot(h_lo, wr_ref[0], preferred_element_type=F32)
                      + jnp.dot(h_hi, wr_ref[1], preferred_element_type=F32)) + br_ref[...]
            logits = jnp.where(lane < ne, logits, -jnp.inf)
            ex = jnp.exp(logits - jnp.max(logits, axis=-1, keepdims=True))
            probs = ex / jnp.sum(ex, axis=-1, keepdims=True)
            v1 = jnp.max(probs, axis=-1, keepdims=True)
            i1 = jnp.min(jnp.where(probs == v1, lane, LANE), axis=-1, keepdims=True)
            rest = jnp.where(lane == i1, -1.0, probs)
            v2 = jnp.max(rest, axis=-1, keepdims=True)
            i2 = jnp.min(jnp.where(rest == v2, lane, LANE), axis=-1, keepdims=True)
            tot = v1 + v2
            comb_scr[half] = jnp.where(lane == i1, v1 / tot, 0.0) + jnp.where(lane == i2, v2 / tot, 0.0)
            sel = jnp.logical_or(lane == i1, lane == i2)
            ri = lax.broadcasted_iota(jnp.int32, (tm, tm), 0)
            ci = lax.broadcasted_iota(jnp.int32, (tm, tm), 1)
            before = jnp.where(ci < ri, 1.0, 0.0).astype(BF16)
            rank = jnp.dot(before, jnp.where(sel, 1.0, 0.0).astype(BF16), preferred_element_type=F32)
            slot = jnp.where(sel, rank, -1.0)
            slot_scr[half] = slot
            slotT_scr[half] = slot.T

        onlane = lane == e
        cw = jnp.sum(jnp.where(onlane, comb_scr[half], 0.0), axis=-1, keepdims=True)
        slot_c = jnp.max(jnp.where(onlane, slot_scr[half], -1.0), axis=-1, keepdims=True)
        slot_r = slotT_scr[half, pl.ds(e, 1), :]
        count = (jnp.max(slot_r) + 1.0).astype(jnp.int32)

        def run_block(base, size):
            row_i = lax.broadcasted_iota(jnp.int32, (size, 1), 0).astype(F32)
            col_i = lax.broadcasted_iota(jnp.int32, (1, size), 1).astype(F32)
            gather = jnp.where(slot_r == base + row_i, 1.0, 0.0).astype(BF16)
            xs = jnp.dot(gather, hb_scr[half], preferred_element_type=F32).astype(BF16)
            y = _swiglu_partial(xs, wg_ref[0, 0], wu_ref[0, 0], wd_ref[0, 0])
            scatter = jnp.where(slot_c == base + col_i, 1.0, 0.0).astype(BF16)
            acc_scr[half] += cw * jnp.dot(scatter, y.astype(BF16), preferred_element_type=F32)

        n_loop = jnp.maximum((count - MOE_BLOCKS[-1] + MOE_LOOP_BLOCK - 1) // MOE_LOOP_BLOCK, 0)

        def body(i, carry):
            run_block((i * MOE_LOOP_BLOCK).astype(F32), MOE_LOOP_BLOCK)
            return carry
        lax.fori_loop(0, n_loop, body, 0)
        done = n_loop * MOE_LOOP_BLOCK
        rest = count - done
        lo = 0
        for size in MOE_BLOCKS:
            pl.when(jnp.logical_and(rest > lo, rest <= size))(
                functools.partial(run_block, done.astype(F32), size))
            lo = size

        @pl.when(e == ne - 1)
        def _():
            g_l, g_c = _mod_rows(mod_ref, b, ctx_row, 5)
            o_ref[rows, :] = x_ref[rows, :] + jnp.where(is_ctx, g_c, g_l) * acc_scr[half]


def _moe(xa, mod, layer, g2, wr, br, wg, wu, wd, moe_layer, *, seq):
    bsz, t_all, _ = xa.shape
    tm = TOK_TILE
    nt = t_all // tm
    group = MOE_GROUP
    assert (bsz * nt) % group == 0
    ne, fe = wg.shape[1], wg.shape[3]
    kern = functools.partial(_moe_kernel, seq=seq, tm=tm, nt=nt, ctx_row=bsz, ne=ne, group=group)
    out = pl.pallas_call(
        kern, grid=(bsz * nt // group, ne),
        in_specs=[pl.BlockSpec((group * tm, D_MODEL), lambda s, e: (s, 0)),
                  pl.BlockSpec((1, MOD_ROWS, 6 * D_MODEL), lambda s, e: (layer, 0, 0)),
                  _const_spec((1, D_MODEL)), _const_spec(wr.shape), _const_spec(br.shape),
                  pl.BlockSpec((1, 1, D_MODEL, fe), lambda s, e: (moe_layer, e, 0, 0)),
                  pl.BlockSpec((1, 1, D_MODEL, fe), lambda s, e: (moe_layer, e, 0, 0)),
                  pl.BlockSpec((1, 1, fe, D_MODEL), lambda s, e: (moe_layer, e, 0, 0))],
        out_specs=pl.BlockSpec((group * tm, D_MODEL), lambda s, e: (s, 0)),
        out_shape=jax.ShapeDtypeStruct((bsz * t_all, D_MODEL), F32),
        scratch_shapes=[pltpu.VMEM((group, tm, D_MODEL), BF16), pltpu.VMEM((group, tm, D_MODEL), F32),
                        pltpu.VMEM((group, tm, LANE), F32), pltpu.VMEM((group, tm, LANE), F32),
                        pltpu.VMEM((group, LANE, tm), F32)],
        compiler_params=_cparams(("parallel", "arbitrary")), name="moe_swiglu",
    )(xa.reshape(bsz * t_all, D_MODEL), mod, g2, wr, br, wg, wu, wd)
    return out.reshape(xa.shape)


def _final_kernel(x_ref, g_ref, o_ref):
    x = x_ref[0]
    o_ref[0] = x * lax.rsqrt(jnp.mean(x * x, axis=-1, keepdims=True) + EPS) * g_ref[...]


def _final_norm(xa, g, *, seq):
    bsz = xa.shape[0]
    tf = 512
    return pl.pallas_call(
        _final_kernel, grid=(bsz, seq // tf),
        in_specs=[pl.BlockSpec((1, tf, D_MODEL), lambda b, j: (b, j, 0)), _const_spec((1, D_MODEL))],
        out_specs=pl.BlockSpec((1, tf, D_MODEL), lambda b, j: (b, j, 0)),
        out_shape=jax.ShapeDtypeStruct((bsz, seq, D_MODEL), F32),
        compiler_params=_cparams(("parallel", "parallel")), name="final_norm",
    )(xa, g.reshape(1, D_MODEL))


def _rope_tables(seq, t_all):
    pos = np.arange(seq)
    per_axis = DIFF_QK_DIM // 2
    inv = (ROPE_BASE ** (-np.arange(0, per_axis, 2, dtype=np.float32) / per_axis)).astype(np.float32)
    rowp = (pos // GRID_W).astype(np.float32)
    colp = (pos % GRID_W).astype(np.float32)
    ang = np.stack([rowp[:, None] * inv, colp[:, None] * inv], axis=1).astype(np.float64)
    cos = np.concatenate([np.cos(ang), np.ones((t_all - seq, 2, 16))], axis=0)
    sin = np.concatenate([np.sin(ang), np.zeros((t_all - seq, 2, 16))], axis=0)
    qscale = (DIFF_QK_DIM ** -0.5) * LOG2E
    ropeT = np.concatenate([cos.reshape(t_all, 32).T, sin.reshape(t_all, 32).T], axis=0) * qscale
    cos64 = np.concatenate([cos[:, 0], cos[:, 0], cos[:, 1], cos[:, 1]], axis=-1)
    sin64 = np.concatenate([-sin[:, 0], sin[:, 0], -sin[:, 1], sin[:, 1]], axis=-1)
    ropeR = np.concatenate([cos64, cos64, sin64, sin64], axis=-1)
    return jnp.asarray(ropeT, F32), jnp.asarray(ropeR, F32)


def _prep_inproj_weights(w_in_l, b_gates_l):
    offs = np.cumsum((0,) + IN_SPLITS)
    col = lambda i: w_in_l[:, offs[i]:offs[i + 1]]
    mq, mk, mv, mo, gt, aq, ak, av, cv = (col(i) for i in range(9))
    nh = N_MLSTM_HEADS
    wTm = jnp.concatenate([mq, mv, mo], axis=1).T.astype(BF16)
    wTg = gt.T.astype(BF16)
    bgT = b_gates_l.reshape(4 * nh, 1).astype(F32)
    wkm = (mk * (MLSTM_HEAD_DIM ** -0.5)).astype(BF16)
    zpad = jnp.zeros((D_MODEL, LANE - 2 * nh), F32)
    wg = jnp.concatenate([gt[:, 0:2 * nh], zpad, gt[:, 2 * nh:4 * nh], zpad], axis=1).astype(BF16)
    bpad = jnp.zeros((LANE - 2 * nh,), F32)
    bg = jnp.concatenate([b_gates_l[0:2 * nh], bpad, b_gates_l[2 * nh:4 * nh], bpad]).reshape(1, 2 * LANE)
    wTaq = aq.T.astype(BF16)
    wak = ak.astype(BF16)
    wTav = av.T.astype(BF16)
    wcv = cv.astype(BF16)
    return (wTm, wTg, bgT, wkm, wg, bg, wTaq, wak, wTav, wcv)


def kernel(x, c, ctx, c_ctx, w_mod, b_mod, g_norm1, w_in, b_gates, g_mlstm, lambda_q1, lambda_k1,
           lambda_q2, lambda_k2, g_subln, w_dw, b_dw, g_conv_ln, b_conv_ln, w_out, g_norm2,
           w_ffn_gate, w_ffn_up, w_ffn_down, w_router, b_router, w_exp_gate, w_exp_up, w_exp_down,
           g_final):
    bsz, seq, _ = x.shape
    nctx = ctx.shape[1]
    t_all = seq + nctx
    depth = w_mod.shape[0]
    assert nctx == CTX_LEN == MLSTM_CHUNK and bsz + 1 <= MOD_ROWS
    assert t_all % TOK_TILE == 0 and seq % Q_TILE == 0 and seq % CONV_TILE == 0 and seq % GRID_W == 0

    xa = jnp.concatenate([x, ctx], axis=1)
    cond = jnp.concatenate([c, c_ctx[None, :], jnp.zeros((MOD_ROWS - bsz - 1, D_MODEL), F32)], axis=0)
    mod = _mod_table(cond, w_mod, b_mod)
    ropeT, ropeR = _rope_tables(seq, t_all)
    nkc = t_all // TOK_TILE
    weg, weu, wed = w_exp_gate.astype(BF16), w_exp_up.astype(BF16), w_exp_down.astype(BF16)
    wfg, wfu, wfd = w_ffn_gate.astype(BF16), w_ffn_up.astype(BF16), w_ffn_down.astype(BF16)

    for l in range(depth):
        wts = _prep_inproj_weights(w_in[l], b_gates[l])
        (qmT, km, vmT, omT, gT, g, qaT, ka, vaT, u) = _inproj(
            xa, mod, l, g_norm1[l].reshape(1, D_MODEL), ropeT, ropeR, wts, seq=seq)

        hTf, hTb = _mlstm(qmT, km, vmT, gT, g, seq=seq)

        lam_init = 0.8 - 0.6 * math.exp(-0.3 * l)
        lamv = jnp.zeros((8, LANE), F32).at[0:4, 0:DIFF_QK_DIM].set(
            jnp.stack([lambda_q1[l], lambda_k1[l], lambda_q2[l], lambda_k2[l]]).astype(F32))
        lamv = lamv.at[4, :].set(lam_init)
        gs_col = g_subln[l].reshape(DIFF_V_DIM, 1).astype(F32)
        d = _attention(lamv, gs_col, qaT, ka, vaT, None, t_all, q_tile=Q_TILE, q_blk0=0,
                       n_q=seq // Q_TILE, k_rows=t_all, k_blk0=0, v_chunks=nkc, v_chunk0=0,
                       v_cols=TOK_TILE, v_blk0=0)
        d = _attention(lamv, gs_col, qaT, ka, vaT, d, t_all, q_tile=nctx, q_blk0=seq // nctx, n_q=1,
                       k_rows=nctx, k_blk0=seq // nctx, v_chunks=1, v_chunk0=nkc - 1,
                       v_cols=nctx, v_blk0=TOK_TILE // nctx - 1)

        cx = _conv(u, w_dw[l], b_dw[l], g_conv_ln[l], b_conv_ln[l], seq=seq)

        wo = w_out[l].astype(BF16)
        xa = _mixout(xa, mod, l, hTf, hTb, omT, g_mlstm[l].reshape(MLSTM_WIDTH, 1).astype(F32), d, cx,
                     wo[0:MLSTM_WIDTH], wo[MLSTM_WIDTH:MLSTM_WIDTH + DIFF_WIDTH],
                     wo[MLSTM_WIDTH + DIFF_WIDTH:], seq=seq)

        jj = l // 2
        g2 = g_norm2[l].reshape(1, D_MODEL)
        if l % 2 == 0:
            xa = _ffn(xa, mod, l, g2, wfg, wfu, wfd, jj, seq=seq)
        else:
            wr = jnp.concatenate([w_router[jj], jnp.zeros((D_MODEL, LANE - N_EXPERTS), F32)], axis=1)
            wr_hi = wr.astype(BF16)
            wr = jnp.stack([wr_hi, (wr - wr_hi.astype(F32)).astype(BF16)])
            br = jnp.concatenate([b_router[jj], jnp.zeros((LANE - N_EXPERTS,), F32)]).reshape(1, LANE)
            xa = _moe(xa, mod, l, g2, wr, br, weg, weu, wed, jj, seq=seq)

    return _final_norm(xa, g_final, seq=seq)
```

```python
import functools
import math

import jax
import jax.numpy as jnp
import numpy as np
from jax import lax
from jax.experimental import pallas as pl
from jax.experimental.pallas import tpu as pltpu

F32 = jnp.float32
BF16 = jnp.bfloat16
HIGHEST = lax.Precision.HIGHEST

D_MODEL = 1024
DEPTH = 4
GRID_W = 64
CTX_LEN = 256
N_MLSTM_HEADS = 4
MLSTM_HEAD_DIM = 64
MLSTM_WIDTH = N_MLSTM_HEADS * MLSTM_HEAD_DIM
N_DIFF_HEADS = 4
DIFF_QK_DIM = 64
DIFF_V_DIM = 2 * DIFF_QK_DIM
DIFF_WIDTH = N_DIFF_HEADS * DIFF_V_DIM
ROPE_BASE = 10000.0
CONV_WIDTH = 256
CONV_KERNEL = 31
IN_SPLITS = (MLSTM_WIDTH, MLSTM_WIDTH, MLSTM_WIDTH, MLSTM_WIDTH, 4 * N_MLSTM_HEADS,
             2 * N_DIFF_HEADS * DIFF_QK_DIM, 2 * N_DIFF_HEADS * DIFF_QK_DIM, DIFF_WIDTH,
             2 * CONV_WIDTH)
D_FF = 2816
N_EXPERTS = 8
D_FF_EXPERT = 1408
EPS = 1e-6
M_INIT = -1e30
NEG_BIG = -1e30
LOG2E = 1.4426950408889634

LANE = 128
V7X_VMEM_LIMIT = 56 * 1024 * 1024
TOK_TILE = 768
MLSTM_CHUNK = 256
CONV_TILE = 256
CONV_HALO = 16
CONV_SHIFTS = 8
Q_TILE = 1024
MOD_ROWS = 8
VAUG = 16
MOE_BLOCKS = (128, 192, 256, 320, 384)
MOE_LOOP_BLOCK = 256
FFN_SPLIT = 2
MOE_GROUP = 1
MLSTM_BATCH = 4
ATT_KEYS = 256
ATT_COLS = 256
ATT_AHEAD = 4


def _cparams(sem):
    return pltpu.CompilerParams(dimension_semantics=sem, vmem_limit_bytes=V7X_VMEM_LIMIT)


def _sigmoid(v):
    return 1.0 / (1.0 + jnp.exp(-v))


def _log_sigmoid(v):
    return jnp.minimum(v, 0.0) - jnp.log(1.0 + jnp.exp(-jnp.abs(v)))


def _mod_rows(mod_ref, b, ctx_row, k):
    lat = mod_ref[0, pl.ds(b, 1), k * D_MODEL:(k + 1) * D_MODEL]
    ctx = mod_ref[0, ctx_row:ctx_row + 1, k * D_MODEL:(k + 1) * D_MODEL]
    return lat, ctx


def _is_ctx_rows(j, tm, seq):
    rows = j * tm + lax.broadcasted_iota(jnp.int32, (tm, 1), 0)
    return rows >= seq


def _rms_mod(x, g, mod_ref, b, ctx_row, is_ctx, k_shift, k_scale):
    y = x * lax.rsqrt(jnp.mean(x * x, axis=-1, keepdims=True) + EPS) * g
    sh_l, sh_c = _mod_rows(mod_ref, b, ctx_row, k_shift)
    sc_l, sc_c = _mod_rows(mod_ref, b, ctx_row, k_scale)
    shift = jnp.where(is_ctx, sh_c, sh_l)
    scale = jnp.where(is_ctx, sc_c, sc_l)
    return y * (1.0 + scale) + shift


def _mod_kernel(cond_ref, w_ref, b_ref, o_ref):
    c = cond_ref[...]
    s = c * _sigmoid(c)
    o_ref[0] = jnp.dot(s, w_ref[0], preferred_element_type=F32, precision=HIGHEST) + b_ref[0]


def _mod_table(cond, w_mod, b_mod):
    depth = w_mod.shape[0]
    n = w_mod.shape[2] // D_MODEL
    return pl.pallas_call(
        _mod_kernel,
        grid=(depth, n),
        in_specs=[pl.BlockSpec((MOD_ROWS, D_MODEL), lambda l, c: (0, 0)),
                  pl.BlockSpec((1, D_MODEL, D_MODEL), lambda l, c: (l, 0, c)),
                  pl.BlockSpec((1, 1, D_MODEL), lambda l, c: (l, 0, c))],
        out_specs=pl.BlockSpec((1, MOD_ROWS, D_MODEL), lambda l, c: (l, 0, c)),
        out_shape=jax.ShapeDtypeStruct((depth, MOD_ROWS, n * D_MODEL), F32),
        compiler_params=_cparams(("parallel", "parallel")),
        name="mod_table",
    )(cond, w_mod, b_mod.reshape(depth, 1, n * D_MODEL))


def _inproj_kernel(x_ref, mod_ref, g1_ref, ropeT_ref, ropeR_ref,
                   wTm_ref, wTg_ref, bgT_ref, wkm_ref,
                   wTaq_ref, wak_ref, wTav_ref, wcv_ref,
                   qmT_o, km_o, vmT_o, omT_o, gT_o, qaT_o, ka_o, vaT_o, u_o,
                   *, seq, tm, ctx_row):
    b = pl.program_id(0)
    j = pl.program_id(1)
    is_ctx = _is_ctx_rows(j, tm, seq)
    h = _rms_mod(x_ref[0], g1_ref[...], mod_ref, b, ctx_row, is_ctx, 0, 1)
    hb = h.astype(BF16)
    hT = h.T.astype(BF16)

    mT = jnp.dot(wTm_ref[...], hT, preferred_element_type=F32)
    w = MLSTM_WIDTH
    qmT_o[0] = mT[0:w].astype(BF16)
    vmT_o[0] = mT[w:2 * w].astype(BF16)
    omT_o[0] = mT[2 * w:3 * w].astype(BF16)
    gT = jnp.dot(wTg_ref[...], hT, preferred_element_type=F32) + bgT_ref[...]
    rowi = lax.broadcasted_iota(jnp.int32, gT.shape, 0)
    gT_o[0] = jnp.where((rowi % 8) >= 4, _log_sigmoid(gT), gT)
    km_o[0] = jnp.dot(hb, wkm_ref[...], preferred_element_type=F32).astype(BF16)

    qT = jnp.dot(wTaq_ref[...], hT, preferred_element_type=F32)
    for grp in range(2 * N_DIFF_HEADS * 2):
        ax = grp % 2
        cos = ropeT_ref[ax * 16:(ax + 1) * 16, :]
        sin = ropeT_ref[32 + ax * 16:32 + (ax + 1) * 16, :]
        x1 = qT[grp * 32:grp * 32 + 16]
        x2 = qT[grp * 32 + 16:grp * 32 + 32]
        qaT_o[0, grp * 32:grp * 32 + 16, :] = (x1 * cos - x2 * sin).astype(BF16)
        qaT_o[0, grp * 32 + 16:grp * 32 + 32, :] = (x2 * cos + x1 * sin).astype(BF16)
    kk = jnp.dot(hb, wak_ref[...], preferred_element_type=F32)
    cosr = ropeR_ref[:, 0:LANE]
    sinr = ropeR_ref[:, LANE:2 * LANE]
    half = DIFF_QK_DIM // 4
    first_half = (lax.broadcasted_iota(jnp.int32, (tm, LANE), 1) % (2 * half)) < half
    for sl in range(2 * N_DIFF_HEADS * DIFF_QK_DIM // LANE):
        k0 = kk[:, sl * LANE:(sl + 1) * LANE]
        k1 = jnp.where(first_half, pltpu.roll(k0, LANE - half, axis=1), pltpu.roll(k0, half, axis=1))
        ka_o[0, :, sl * LANE:(sl + 1) * LANE] = (k0 * cosr + k1 * sinr).astype(BF16)
    vT = jnp.dot(wTav_ref[...], hT, preferred_element_type=F32)
    for hh in range(N_DIFF_HEADS):
        vaT_o[0, 0, hh, 0:DIFF_V_DIM, :] = vT[hh * DIFF_V_DIM:(hh + 1) * DIFF_V_DIM].astype(BF16)
        vaT_o[0, 0, hh, DIFF_V_DIM:DIFF_V_DIM + VAUG, :] = jnp.ones((VAUG, tm), BF16)
    cv = jnp.dot(hb, wcv_ref[...], preferred_element_type=F32)
    u_o[0] = (cv[:, :CONV_WIDTH] * _sigmoid(cv[:, CONV_WIDTH:])).astype(BF16)


def _const_spec(shape):
    nd = len(shape)
    return pl.BlockSpec(shape, lambda *_: (0,) * nd)


def _inproj(xa, mod, layer, g1, ropeT, ropeR, wts, *, seq):
    bsz, t_all, _ = xa.shape
    tm = TOK_TILE
    nt = t_all // tm
    kern = functools.partial(_inproj_kernel, seq=seq, tm=tm, ctx_row=bsz)
    w = MLSTM_WIDTH
    out_shapes = (
        jax.ShapeDtypeStruct((bsz, w, t_all), BF16),
        jax.ShapeDtypeStruct((bsz, t_all, w), BF16),
        jax.ShapeDtypeStruct((bsz, w, t_all), BF16),
        jax.ShapeDtypeStruct((bsz, w, t_all), BF16),
        jax.ShapeDtypeStruct((bsz, 16, t_all), F32),
        jax.ShapeDtypeStruct((bsz, DIFF_WIDTH, t_all), BF16),
        jax.ShapeDtypeStruct((bsz, t_all, DIFF_WIDTH), BF16),
        jax.ShapeDtypeStruct((bsz, nt, N_DIFF_HEADS, DIFF_V_DIM + VAUG, tm), BF16),
        jax.ShapeDtypeStruct((bsz, t_all, CONV_WIDTH), BF16),
    )
    fm = lambda rows: pl.BlockSpec((1, rows, tm), lambda b, j: (b, 0, j))
    tk = lambda cols: pl.BlockSpec((1, tm, cols), lambda b, j: (b, j, 0))
    out_specs = (fm(w), tk(w), fm(w), fm(w), fm(16), fm(DIFF_WIDTH), tk(DIFF_WIDTH),
                 pl.BlockSpec((1, 1, N_DIFF_HEADS, DIFF_V_DIM + VAUG, tm), lambda b, j: (b, j, 0, 0, 0)),
                 tk(CONV_WIDTH))
    in_specs = [
        pl.BlockSpec((1, tm, D_MODEL), lambda b, j: (b, j, 0)),
        pl.BlockSpec((1, MOD_ROWS, 6 * D_MODEL), lambda b, j: (layer, 0, 0)),
        _const_spec((1, D_MODEL)),
        pl.BlockSpec((64, tm), lambda b, j: (0, j)),
        pl.BlockSpec((tm, 2 * LANE), lambda b, j: (j, 0)),
    ] + [_const_spec(a.shape) for a in wts]
    return pl.pallas_call(
        kern, grid=(bsz, nt), in_specs=in_specs, out_specs=out_specs, out_shape=out_shapes,
        compiler_params=_cparams(("parallel", "parallel")), name="inproj",
    )(xa, mod, g1, ropeT, ropeR, *wts)


def _mlstm_kernel(qf_ref, kf_ref, vf_ref, gTf_ref, qb_ref, kb_ref, vb_ref, gTb_ref,
                  hf_o, hb_o, c_scr, m_scr, *, chunk, nb):
    step = pl.program_id(1)
    L = chunk
    hd = MLSTM_HEAD_DIM
    nh = N_MLSTM_HEADS

    @pl.when(step == 0)
    def _():
        c_scr[...] = jnp.zeros(c_scr.shape, F32)
        m_scr[...] = jnp.full(m_scr.shape, M_INIT, F32)

    si = lax.broadcasted_iota(jnp.int32, (L, L), 0)
    ti = lax.broadcasted_iota(jnp.int32, (L, L), 1)
    ones_rows = jnp.ones((VAUG, L), BF16)
    zero64 = jnp.zeros((hd, L), BF16)

    dirs = ((qf_ref, kf_ref, vf_ref, gTf_ref, hf_o), (qb_ref, kb_ref, vb_ref, gTb_ref, hb_o))
    masks = []
    for d in range(2):
        causal = (si <= ti) if d == 0 else (si >= ti)
        masks.append((causal, jnp.where(causal, 1.0, 0.0).astype(F32)))

    chains = []
    for bb, d in [(bb, d) for bb in range(nb) for d in range(2)]:
        qT_ref, k_ref, vT_ref, gT_ref, h_o = dirs[d]
        causal, tri = masks[d]
        gT = gT_ref[bb]
        b_rows = jnp.dot(gT, tri, preferred_element_type=F32, precision=HIGHEST)
        totals = jnp.sum(gT, axis=1, keepdims=True)
        a_rows = gT - pltpu.roll(b_rows, nh, axis=0)
        a_cols = jnp.concatenate([a_rows, jnp.zeros((LANE - 2 * nh, L), F32)], axis=0).T
        for hh in range(nh):
            pair, half = hh // 2, hh % 2
            q_h = qT_ref[bb, hh * hd:(hh + 1) * hd, :]
            q_msk = jnp.concatenate([q_h, zero64] if half == 0 else [zero64, q_h], axis=0)
            k_pair = k_ref[bb, :, pair * 2 * hd:(pair + 1) * 2 * hd]
            v_aug = jnp.concatenate([vT_ref[bb, hh * hd:(hh + 1) * hd, :], ones_rows], axis=0)
            idx = (bb * 2 + d) * nh + hh
            c_st = c_scr[idx]
            sT = jnp.dot(k_pair, q_msk, preferred_element_type=F32)
            cq = jnp.dot(c_st.astype(BF16), q_msk, preferred_element_type=F32)
            chains.append(dict(idx=idx, hh=hh, bb=bb, h_o=h_o, causal=causal, k_pair=k_pair, v_aug=v_aug,
                               c_st=c_st, sT=sT, cq=cq, li_row=gT[hh:hh + 1], b_row=b_rows[4 + hh:5 + hh],
                               total=totals[4 + hh:5 + hh],
                               a_col=a_cols[:, hh:hh + 1]))

    for ch in chains:
        hh, b_row = ch["hh"], ch["b_row"]
        m_st = m_scr[ch["idx"], 0:1, :]
        dmat = jnp.where(ch["causal"], b_row + ch["a_col"], -jnp.inf)
        inter = b_row + m_st
        m_t = jnp.maximum(inter, jnp.max(dmat, axis=0, keepdims=True))
        wT = jnp.exp(dmat - m_t) * ch["sT"]
        e_inter = jnp.exp(inter - m_t)
        intra = jnp.dot(ch["v_aug"], wT.astype(BF16), preferred_element_type=F32)
        cq = ch["cq"]
        num = e_inter * cq[0:hd] + intra[0:hd]
        den = e_inter * cq[hd:hd + 1] + jnp.sum(wT, axis=0, keepdims=True)
        ch["h_o"][ch["bb"], hh * hd:(hh + 1) * hd, :] = num / jnp.maximum(jnp.abs(den), jnp.exp(-m_t))

        total = ch["total"]
        g_row = total - b_row + ch["li_row"]
        m_prev = m_st[:, 0:1]
        m_new = jnp.maximum(total + m_prev, jnp.max(g_row, axis=1, keepdims=True))
        e_old = jnp.exp(total + m_prev - m_new)
        e_g = jnp.exp(g_row - m_new)
        upd = jnp.dot((ch["v_aug"].astype(F32) * e_g).astype(BF16), ch["k_pair"],
                      preferred_element_type=F32)
        c_scr[ch["idx"]] = e_old * ch["c_st"] + upd
        m_scr[ch["idx"]] = jnp.broadcast_to(m_new, m_scr.shape[1:])


def _mlstm(qmT, km, vmT, gT, *, seq):
    bsz, w, t_all = qmT.shape
    L = MLSTM_CHUNK
    nlat = seq // L
    nch = t_all // L
    nctx = nch - nlat
    fwd = lambda i: jnp.where(i < nctx, nlat + i, i - nctx)
    bwd = lambda i: nch - 1 - i

    nb = MLSTM_BATCH if bsz % MLSTM_BATCH == 0 else 1

    def specs(chunk_of, d):
        return [pl.BlockSpec((nb, w, L), lambda b, i: (b, 0, chunk_of(i))),
                pl.BlockSpec((nb, L, w), lambda b, i: (b, chunk_of(i), 0)),
                pl.BlockSpec((nb, w, L), lambda b, i: (b, 0, chunk_of(i))),
                pl.BlockSpec((nb, 8, L), lambda b, i: (b, d, chunk_of(i)))]

    kern = functools.partial(_mlstm_kernel, chunk=L, nb=nb)
    out = jax.ShapeDtypeStruct((bsz, w, t_all), F32)
    nchain = 2 * nb * N_MLSTM_HEADS
    return pl.pallas_call(
        kern, grid=(bsz // nb, nch),
        in_specs=specs(fwd, 0) + specs(bwd, 1),
        out_specs=(pl.BlockSpec((nb, w, L), lambda b, i: (b, 0, fwd(i))),
                   pl.BlockSpec((nb, w, L), lambda b, i: (b, 0, bwd(i)))),
        out_shape=(out, out),
        scratch_shapes=[pltpu.VMEM((nchain, MLSTM_HEAD_DIM + VAUG, LANE), F32),
                        pltpu.VMEM((nchain, 8, L), F32)],
        compiler_params=_cparams(("parallel", "arbitrary")), name="mlstm_scan",
    )(qmT, km, vmT, gT, qmT, km, vmT, gT)


def _attn_kernel(lam_ref, gs_ref, qT_ref, k_ref, vT_ref, *rest, nch, tk):
    o_ref = rest[-1]
    tq = qT_ref.shape[2]
    qT = qT_ref[0]
    z = jnp.zeros((DIFF_QK_DIM, tq), BF16)
    rhs = jnp.concatenate([jnp.concatenate([qT[:DIFF_QK_DIM], z], axis=0),
                           jnp.concatenate([z, qT[DIFF_QK_DIM:]], axis=0)], axis=1)
    ncb = 2 * tq // ATT_COLS
    ms = [jnp.full((1, ATT_COLS), NEG_BIG, F32) for _ in range(ncb)]
    accs = [jnp.zeros((DIFF_V_DIM + VAUG, ATT_COLS), F32) for _ in range(ncb)]
    sub = tk // ATT_KEYS
    units = [(c, cb) for c in range(nch * sub) for cb in range(ncb)]

    def scores(c, cb):
        return jnp.dot(k_ref[0, c * ATT_KEYS:(c + 1) * ATT_KEYS, :],
                       rhs[:, cb * ATT_COLS:(cb + 1) * ATT_COLS], preferred_element_type=F32)

    pending = [scores(*u) for u in units[:ATT_AHEAD]]
    for i, (c, cb) in enumerate(units):
        sT = pending.pop(0)
        if i + ATT_AHEAD < len(units):
            pending.append(scores(*units[i + ATT_AHEAD]))
        vT = vT_ref[0, c // sub, 0, :, (c % sub) * ATT_KEYS:(c % sub + 1) * ATT_KEYS]
        m_new = jnp.maximum(ms[cb], jnp.max(sT, axis=0, keepdims=True))
        p = jnp.exp2((sT - m_new).astype(BF16))
        alpha = jnp.exp2(ms[cb] - m_new)
        accs[cb] = alpha * accs[cb] + jnp.dot(vT, p, preferred_element_type=F32)
        ms[cb] = m_new
    acc = jnp.concatenate(accs, axis=1)
    l = acc[DIFF_V_DIM:DIFF_V_DIM + 1]
    acc = acc[0:DIFF_V_DIM]

    lv = lam_ref[...]
    lam_init = lv[4:5, 0:1]
    lam = (jnp.exp(jnp.sum(lv[0:1] * lv[1:2], axis=1, keepdims=True))
           - jnp.exp(jnp.sum(lv[2:3] * lv[3:4], axis=1, keepdims=True)) + lam_init)
    oT = acc[:, :tq] / l[:, :tq] - lam * (acc[:, tq:] / l[:, tq:])
    oT = oT * lax.rsqrt(jnp.mean(oT * oT, axis=0, keepdims=True) + EPS) * gs_ref[...] * (1.0 - lam_init)
    o_ref[0] = oT.T.astype(BF16)


def _attention(lamv, gs_col, qaT, ka, vaT, d_prev, out_rows, *, q_tile, q_blk0, n_q, k_rows, k_blk0,
               v_chunks, v_chunk0, v_cols, v_blk0):
    bsz = qaT.shape[0]
    kern = functools.partial(_attn_kernel, nch=v_chunks, tk=v_cols)
    in_specs = [_const_spec(lamv.shape), _const_spec((DIFF_V_DIM, 1)),
                pl.BlockSpec((1, DIFF_V_DIM, q_tile), lambda b, h, i: (b, h, q_blk0 + i)),
                pl.BlockSpec((1, k_rows, DIFF_V_DIM), lambda b, h, i: (b, k_blk0, h)),
                pl.BlockSpec((1, v_chunks, 1, DIFF_V_DIM + VAUG, v_cols),
                             lambda b, h, i: (b, v_chunk0, h, 0, v_blk0))]
    args = [lamv, gs_col, qaT, ka, vaT]
    aliases = {}
    if d_prev is not None:
        in_specs.append(pl.BlockSpec(memory_space=pl.ANY))
        aliases = {len(args): 0}
        args.append(d_prev)
    return pl.pallas_call(
        kern, grid=(bsz, N_DIFF_HEADS, n_q), in_specs=in_specs,
        out_specs=pl.BlockSpec((1, q_tile, DIFF_V_DIM), lambda b, h, i: (b, q_blk0 + i, h)),
        out_shape=jax.ShapeDtypeStruct((bsz, out_rows, DIFF_WIDTH), BF16),
        input_output_aliases=aliases,
        compiler_params=_cparams(("parallel", "parallel", "arbitrary")), name="diff_attn",
    )(*args)


def _conv_kernel(l_ref, c_ref, r_ref, w_ref, b_ref, g_ref, bb_ref, o_ref, buf, shifted, *, seq, tc, nt):
    j = pl.program_id(1)
    start = j * tc
    lvalid = jnp.logical_and(j > 0, start != seq)
    rvalid = jnp.logical_and(j < nt - 1, start + tc != seq)
    hl = CONV_HALO
    buf[0:hl, :] = jnp.where(lvalid, l_ref[0].astype(F32), 0.0)
    buf[hl:hl + tc, :] = c_ref[0].astype(F32)
    buf[hl + tc:2 * hl + tc, :] = jnp.where(rvalid, r_ref[0].astype(F32), 0.0)
    pad = CONV_KERNEL // 2
    sub = 128
    sl = CONV_SHIFTS
    for r in range(sl):
        shifted[r] = buf[pl.ds(r, shifted.shape[1]), :]
    for r0 in range(0, tc, sub):
        accs = [jnp.zeros((sub, LANE), F32) for _ in range(CONV_WIDTH // LANE)]
        for t in range(CONV_KERNEL):
            q, r = divmod(hl - pad + t, sl)
            for cb in range(CONV_WIDTH // LANE):
                cols = slice(cb * LANE, (cb + 1) * LANE)
                accs[cb] = accs[cb] + w_ref[t:t + 1, cols] * shifted[r, pl.ds(r0 + sl * q, sub), cols]
        y = jnp.concatenate(accs, axis=1) + b_ref[...]
        mu = jnp.mean(y, axis=-1, keepdims=True)
        var = jnp.mean(jnp.square(y - mu), axis=-1, keepdims=True)
        z = (y - mu) * lax.rsqrt(var + EPS) * g_ref[...] + bb_ref[...]
        o_ref[0, r0:r0 + sub, :] = (z * _sigmoid(z)).astype(BF16)


def _conv(u, w_dw, b_dw, g_ln, b_ln, *, seq):
    bsz, t_all, cw = u.shape
    tc = CONV_TILE
    nt = t_all // tc
    r = tc // CONV_HALO
    nhalo = t_all // CONV_HALO
    kern = functools.partial(_conv_kernel, seq=seq, tc=tc, nt=nt)
    row = lambda a: a.reshape(1, cw)
    return pl.pallas_call(
        kern, grid=(bsz, nt),
        in_specs=[pl.BlockSpec((1, CONV_HALO, cw), lambda b, j: (b, jnp.maximum(j * r - 1, 0), 0)),
                  pl.BlockSpec((1, tc, cw), lambda b, j: (b, j, 0)),
                  pl.BlockSpec((1, CONV_HALO, cw), lambda b, j: (b, jnp.minimum((j + 1) * r, nhalo - 1), 0)),
                  _const_spec((CONV_KERNEL, cw)), _const_spec((1, cw)), _const_spec((1, cw)),
                  _const_spec((1, cw))],
        out_specs=pl.BlockSpec((1, tc, cw), lambda b, j: (b, j, 0)),
        out_shape=jax.ShapeDtypeStruct((bsz, t_all, cw), BF16),
        scratch_shapes=[pltpu.VMEM((tc + 2 * CONV_HALO, cw), F32),
                        pltpu.VMEM((CONV_SHIFTS, tc + 2 * CONV_HALO - CONV_SHIFTS, cw), F32)],
        compiler_params=_cparams(("parallel", "parallel")), name="conv_mixer",
    )(u, u, u, w_dw, row(b_dw), row(g_ln), row(b_ln))


def _mixout_kernel(x_ref, mod_ref, hf_ref, hb_ref, omT_ref, gm_ref, d_ref, c_ref, wm_ref, wd_ref, wc_ref, o_ref,
                   *, seq, tm, ctx_row):
    b = pl.program_id(0)
    j = pl.program_id(1)
    is_ctx = _is_ctx_rows(j, tm, seq)
    hT = hf_ref[0] + hb_ref[0]
    h4 = hT.reshape(N_MLSTM_HEADS, MLSTM_HEAD_DIM, tm)
    mu = jnp.mean(h4, axis=1, keepdims=True)
    var = jnp.mean(jnp.square(h4 - mu), axis=1, keepdims=True)
    hn = ((h4 - mu) * lax.rsqrt(var + EPS)).reshape(MLSTM_WIDTH, tm)
    mT = _sigmoid(omT_ref[0].astype(F32)) * hn * gm_ref[...]
    m = mT.T.astype(BF16)
    y = (jnp.dot(m, wm_ref[...], preferred_element_type=F32)
         + jnp.dot(d_ref[0], wd_ref[...], preferred_element_type=F32)
         + jnp.dot(c_ref[0], wc_ref[...], preferred_element_type=F32))
    g_l, g_c = _mod_rows(mod_ref, b, ctx_row, 2)
    o_ref[0] = x_ref[0] + jnp.where(is_ctx, g_c, g_l) * y


def _mixout(xa, mod, layer, hTf, hTb, omT, gm_col, d, cx, wm, wd, wc, *, seq):
    bsz, t_all, _ = xa.shape
    tm = TOK_TILE
    nt = t_all // tm
    kern = functools.partial(_mixout_kernel, seq=seq, tm=tm, ctx_row=bsz)
    w = MLSTM_WIDTH
    return pl.pallas_call(
        kern, grid=(bsz, nt),
        in_specs=[pl.BlockSpec((1, tm, D_MODEL), lambda b, j: (b, j, 0)),
                  pl.BlockSpec((1, MOD_ROWS, 6 * D_MODEL), lambda b, j: (layer, 0, 0)),
                  pl.BlockSpec((1, w, tm), lambda b, j: (b, 0, j)),
                  pl.BlockSpec((1, w, tm), lambda b, j: (b, 0, j)),
                  pl.BlockSpec((1, w, tm), lambda b, j: (b, 0, j)),
                  _const_spec((w, 1)),
                  pl.BlockSpec((1, tm, DIFF_WIDTH), lambda b, j: (b, j, 0)),
                  pl.BlockSpec((1, tm, CONV_WIDTH), lambda b, j: (b, j, 0)),
                  _const_spec(wm.shape), _const_spec(wd.shape), _const_spec(wc.shape)],
        out_specs=pl.BlockSpec((1, tm, D_MODEL), lambda b, j: (b, j, 0)),
        out_shape=jax.ShapeDtypeStruct(xa.shape, F32),
        compiler_params=_cparams(("parallel", "parallel")), name="mix_out",
    )(xa, mod, hTf, hTb, omT, gm_col, d, cx, wm, wd, wc)


def _swiglu_partial(hb, wg, wu, wd):
    a = jnp.dot(hb, wg, preferred_element_type=F32)
    u = jnp.dot(hb, wu, preferred_element_type=F32)
    t = (a * _sigmoid(a) * u).astype(BF16)
    return jnp.dot(t, wd, preferred_element_type=F32)


def _ffn_kernel(x_ref, mod_ref, g2_ref, wg_ref, wu_ref, wd_ref, o_ref, hb_scr, acc_scr,
                *, seq, tm, ctx_row, nf):
    b = pl.program_id(0)
    j = pl.program_id(1)
    f = pl.program_id(2)
    is_ctx = _is_ctx_rows(j, tm, seq)

    @pl.when(f == 0)
    def _():
        h = _rms_mod(x_ref[0], g2_ref[...], mod_ref, b, ctx_row, is_ctx, 3, 4)
        hb_scr[...] = h.astype(BF16)
        acc_scr[...] = jnp.zeros(acc_scr.shape, F32)

    acc_scr[...] += _swiglu_partial(hb_scr[...], wg_ref[0], wu_ref[0], wd_ref[0])

    @pl.when(f == nf - 1)
    def _():
        g_l, g_c = _mod_rows(mod_ref, b, ctx_row, 5)
        o_ref[0] = x_ref[0] + jnp.where(is_ctx, g_c, g_l) * acc_scr[...]


def _ffn(xa, mod, layer, g2, wg, wu, wd, ffn_layer, *, seq):
    bsz, t_all, _ = xa.shape
    tm = TOK_TILE
    nt = t_all // tm
    nf = FFN_SPLIT
    tf = wg.shape[2] // nf
    assert tf % LANE == 0 and tf * nf == wg.shape[2]
    kern = functools.partial(_ffn_kernel, seq=seq, tm=tm, ctx_row=bsz, nf=nf)
    return pl.pallas_call(
        kern, grid=(bsz, nt, nf),
        in_specs=[pl.BlockSpec((1, tm, D_MODEL), lambda b, j, f: (b, j, 0)),
                  pl.BlockSpec((1, MOD_ROWS, 6 * D_MODEL), lambda b, j, f: (layer, 0, 0)),
                  _const_spec((1, D_MODEL)),
                  pl.BlockSpec((1, D_MODEL, tf), lambda b, j, f: (ffn_layer, 0, f)),
                  pl.BlockSpec((1, D_MODEL, tf), lambda b, j, f: (ffn_layer, 0, f)),
                  pl.BlockSpec((1, tf, D_MODEL), lambda b, j, f: (ffn_layer, f, 0))],
        out_specs=pl.BlockSpec((1, tm, D_MODEL), lambda b, j, f: (b, j, 0)),
        out_shape=jax.ShapeDtypeStruct(xa.shape, F32),
        scratch_shapes=[pltpu.VMEM((tm, D_MODEL), BF16), pltpu.VMEM((tm, D_MODEL), F32)],
        compiler_params=_cparams(("parallel", "parallel", "arbitrary")), name="ffn_swiglu",
    )(xa, mod, g2, wg, wu, wd)


def _moe_kernel(x_ref, mod_ref, g2_ref, wr_ref, br_ref, wg_ref, wu_ref, wd_ref, o_ref,
                hb_scr, acc_scr, comb_scr, slot_scr, slotT_scr, *, seq, tm, nt, ctx_row, ne, group):
    step = pl.program_id(0)
    e = pl.program_id(1)
    lane = lax.broadcasted_iota(jnp.int32, (tm, LANE), 1)

    for half in range(group):
        tile = step * group + half
        b = tile // nt
        is_ctx = _is_ctx_rows(tile % nt, tm, seq)
        rows = slice(half * tm, (half + 1) * tm)

        @pl.when(e == 0)
        def _():
            h = _rms_mod(x_ref[rows, :], g2_ref[...], mod_ref, b, ctx_row, is_ctx, 3, 4)
            h_hi = h.astype(BF16)
            hb_scr[half] = h_hi
            acc_scr[half] = jnp.zeros((tm, D_MODEL), F32)
            h_lo = (h - h_hi.astype(F32)).astype(BF16)
            logits = (jnp.dot(h_hi, wr_ref[0], preferred_element_type=F32)
                      + jnp.dot(h_lo, wr_ref[0], preferred_element_type=F32)
                      + jnp.dot(h_hi, wr_ref[1], preferred_element_type=F32)) + br_ref[...]
            logits = jnp.where(lane < ne, logits, -jnp.inf)
            ex = jnp.exp(logits - jnp.max(logits, axis=-1, keepdims=True))
            probs = ex / jnp.sum(ex, axis=-1, keepdims=True)
            v1 = jnp.max(probs, axis=-1, keepdims=True)
            i1 = jnp.min(jnp.where(probs == v1, lane, LANE), axis=-1, keepdims=True)
            rest = jnp.where(lane == i1, -1.0, probs)
            v2 = jnp.max(rest, axis=-1, keepdims=True)
            i2 = jnp.min(jnp.where(rest == v2, lane, LANE), axis=-1, keepdims=True)
            tot = v1 + v2
            comb_scr[half] = jnp.where(lane == i1, v1 / tot, 0.0) + jnp.where(lane == i2, v2 / tot, 0.0)
            sel = jnp.logical_or(lane == i1, lane == i2)
            ri = lax.broadcasted_iota(jnp.int32, (tm, tm), 0)
            ci = lax.broadcasted_iota(jnp.int32, (tm, tm), 1)
            before = jnp.where(ci < ri, 1.0, 0.0).astype(BF16)
            rank = jnp.dot(before, jnp.where(sel, 1.0, 0.0).astype(BF16), preferred_element_type=F32)
            slot = jnp.where(sel, rank, -1.0)
            slot_scr[half] = slot
            slotT_scr[half] = slot.T

        onlane = lane == e
        cw = jnp.sum(jnp.where(onlane, comb_scr[half], 0.0), axis=-1, keepdims=True)
        slot_c = jnp.max(jnp.where(onlane, slot_scr[half], -1.0), axis=-1, keepdims=True)
        slot_r = slotT_scr[half, pl.ds(e, 1), :]
        count = (jnp.max(slot_r) + 1.0).astype(jnp.int32)

        def run_block(base, size):
            row_i = lax.broadcasted_iota(jnp.int32, (size, 1), 0).astype(F32)
            col_i = lax.broadcasted_iota(jnp.int32, (1, size), 1).astype(F32)
            gather = jnp.where(slot_r == base + row_i, 1.0, 0.0).astype(BF16)
            xs = jnp.dot(gather, hb_scr[half], preferred_element_type=F32).astype(BF16)
            y = _swiglu_partial(xs, wg_ref[0, 0], wu_ref[0, 0], wd_ref[0, 0])
            scatter = jnp.where(slot_c == base + col_i, 1.0, 0.0).astype(BF16)
            acc_scr[half] += cw * jnp.dot(scatter, y.astype(BF16), preferred_element_type=F32)

        n_loop = jnp.maximum((count - MOE_BLOCKS[-1] + MOE_LOOP_BLOCK - 1) // MOE_LOOP_BLOCK, 0)

        def body(i, carry):
            run_block((i * MOE_LOOP_BLOCK).astype(F32), MOE_LOOP_BLOCK)
            return carry
        lax.fori_loop(0, n_loop, body, 0)
        done = n_loop * MOE_LOOP_BLOCK
        rest = count - done
        lo = 0
        for size in MOE_BLOCKS:
            pl.when(jnp.logical_and(rest > lo, rest <= size))(
                functools.partial(run_block, done.astype(F32), size))
            lo = size

        @pl.when(e == ne - 1)
        def _():
            g_l, g_c = _mod_rows(mod_ref, b, ctx_row, 5)
            o_ref[rows, :] = x_ref[rows, :] + jnp.where(is_ctx, g_c, g_l) * acc_scr[half]


def _moe(xa, mod, layer, g2, wr, br, wg, wu, wd, moe_layer, *, seq):
    bsz, t_all, _ = xa.shape
    tm = TOK_TILE
    nt = t_all // tm
    group = MOE_GROUP
    assert (bsz * nt) % group == 0
    ne, fe = wg.shape[1], wg.shape[3]
    kern = functools.partial(_moe_kernel, seq=seq, tm=tm, nt=nt, ctx_row=bsz, ne=ne, group=group)
    out = pl.pallas_call(
        kern, grid=(bsz * nt // group, ne),
        in_specs=[pl.BlockSpec((group * tm, D_MODEL), lambda s, e: (s, 0)),
                  pl.BlockSpec((1, MOD_ROWS, 6 * D_MODEL), lambda s, e: (layer, 0, 0)),
                  _const_spec((1, D_MODEL)), _const_spec(wr.shape), _const_spec(br.shape),
                  pl.BlockSpec((1, 1, D_MODEL, fe), lambda s, e: (moe_layer, e, 0, 0)),
                  pl.BlockSpec((1, 1, D_MODEL, fe), lambda s, e: (moe_layer, e, 0, 0)),
                  pl.BlockSpec((1, 1, fe, D_MODEL), lambda s, e: (moe_layer, e, 0, 0))],
        out_specs=pl.BlockSpec((group * tm, D_MODEL), lambda s, e: (s, 0)),
        out_shape=jax.ShapeDtypeStruct((bsz * t_all, D_MODEL), F32),
        scratch_shapes=[pltpu.VMEM((group, tm, D_MODEL), BF16), pltpu.VMEM((group, tm, D_MODEL), F32),
                        pltpu.VMEM((group, tm, LANE), F32), pltpu.VMEM((group, tm, LANE), F32),
                        pltpu.VMEM((group, LANE, tm), F32)],
        compiler_params=_cparams(("parallel", "arbitrary")), name="moe_swiglu",
    )(xa.reshape(bsz * t_all, D_MODEL), mod, g2, wr, br, wg, wu, wd)
    return out.reshape(xa.shape)


def _final_kernel(x_ref, g_ref, o_ref):
    x = x_ref[0]
    o_ref[0] = x * lax.rsqrt(jnp.mean(x * x, axis=-1, keepdims=True) + EPS) * g_ref[...]


def _final_norm(xa, g, *, seq):
    bsz = xa.shape[0]
    tf = 512
    return pl.pallas_call(
        _final_kernel, grid=(bsz, seq // tf),
        in_specs=[pl.BlockSpec((1, tf, D_MODEL), lambda b, j: (b, j, 0)), _const_spec((1, D_MODEL))],
        out_specs=pl.BlockSpec((1, tf, D_MODEL), lambda b, j: (b, j, 0)),
        out_shape=jax.ShapeDtypeStruct((bsz, seq, D_MODEL), F32),
        compiler_params=_cparams(("parallel", "parallel")), name="final_norm",
    )(xa, g.reshape(1, D_MODEL))


def _rope_tables(seq, t_all):
    pos = np.arange(seq)
    per_axis = DIFF_QK_DIM // 2
    inv = (ROPE_BASE ** (-np.arange(0, per_axis, 2, dtype=np.float32) / per_axis)).astype(np.float32)
    rowp = (pos // GRID_W).astype(np.float32)
    colp = (pos % GRID_W).astype(np.float32)
    ang = np.stack([rowp[:, None] * inv, colp[:, None] * inv], axis=1).astype(np.float64)
    cos = np.concatenate([np.cos(ang), np.ones((t_all - seq, 2, 16))], axis=0)
    sin = np.concatenate([np.sin(ang), np.zeros((t_all - seq, 2, 16))], axis=0)
    qscale = (DIFF_QK_DIM ** -0.5) * LOG2E
    ropeT = np.concatenate([cos.reshape(t_all, 32).T, sin.reshape(t_all, 32).T], axis=0) * qscale
    cos64 = np.concatenate([cos[:, 0], cos[:, 0], cos[:, 1], cos[:, 1]], axis=-1)
    sin64 = np.concatenate([-sin[:, 0], sin[:, 0], -sin[:, 1], sin[:, 1]], axis=-1)
    ropeR = np.concatenate([cos64, cos64, sin64, sin64], axis=-1)
    return jnp.asarray(ropeT, F32), jnp.asarray(ropeR, F32)


def _prep_inproj_weights(w_in_l, b_gates_l):
    offs = np.cumsum((0,) + IN_SPLITS)
    col = lambda i: w_in_l[:, offs[i]:offs[i + 1]]
    mq, mk, mv, mo, gt, aq, ak, av, cv = (col(i) for i in range(9))
    nh = N_MLSTM_HEADS
    wTm = jnp.concatenate([mq, mv, mo], axis=1).T.astype(BF16)
    wTg = gt.T.astype(BF16)
    bgT = b_gates_l.reshape(4 * nh, 1).astype(F32)
    wkm = (mk * (MLSTM_HEAD_DIM ** -0.5)).astype(BF16)
    wTaq = aq.T.astype(BF16)
    wak = ak.astype(BF16)
    wTav = av.T.astype(BF16)
    wcv = cv.astype(BF16)
    return (wTm, wTg, bgT, wkm, wTaq, wak, wTav, wcv)


def kernel(x, c, ctx, c_ctx, w_mod, b_mod, g_norm1, w_in, b_gates, g_mlstm, lambda_q1, lambda_k1,
           lambda_q2, lambda_k2, g_subln, w_dw, b_dw, g_conv_ln, b_conv_ln, w_out, g_norm2,
           w_ffn_gate, w_ffn_up, w_ffn_down, w_router, b_router, w_exp_gate, w_exp_up, w_exp_down,
           g_final):
    bsz, seq, _ = x.shape
    nctx = ctx.shape[1]
    t_all = seq + nctx
    depth = w_mod.shape[0]
    assert nctx == CTX_LEN == MLSTM_CHUNK and bsz + 1 <= MOD_ROWS
    assert t_all % TOK_TILE == 0 and seq % Q_TILE == 0 and seq % CONV_TILE == 0 and seq % GRID_W == 0

    xa = jnp.concatenate([x, ctx], axis=1)
    cond = jnp.concatenate([c, c_ctx[None, :], jnp.zeros((MOD_ROWS - bsz - 1, D_MODEL), F32)], axis=0)
    mod = _mod_table(cond, w_mod, b_mod)
    ropeT, ropeR = _rope_tables(seq, t_all)
    nkc = t_all // TOK_TILE
    weg, weu, wed = w_exp_gate.astype(BF16), w_exp_up.astype(BF16), w_exp_down.astype(BF16)
    wfg, wfu, wfd = w_ffn_gate.astype(BF16), w_ffn_up.astype(BF16), w_ffn_down.astype(BF16)

    for l in range(depth):
        wts = _prep_inproj_weights(w_in[l], b_gates[l])
        (qmT, km, vmT, omT, gT, qaT, ka, vaT, u) = _inproj(
            xa, mod, l, g_norm1[l].reshape(1, D_MODEL), ropeT, ropeR, wts, seq=seq)

        hTf, hTb = _mlstm(qmT, km, vmT, gT, seq=seq)

        lam_init = 0.8 - 0.6 * math.exp(-0.3 * l)
        lamv = jnp.zeros((8, LANE), F32).at[0:4, 0:DIFF_QK_DIM].set(
            jnp.stack([lambda_q1[l], lambda_k1[l], lambda_q2[l], lambda_k2[l]]).astype(F32))
        lamv = lamv.at[4, :].set(lam_init)
        gs_col = g_subln[l].reshape(DIFF_V_DIM, 1).astype(F32)
        d = _attention(lamv, gs_col, qaT, ka, vaT, None, t_all, q_tile=Q_TILE, q_blk0=0,
                       n_q=seq // Q_TILE, k_rows=t_all, k_blk0=0, v_chunks=nkc, v_chunk0=0,
                       v_cols=TOK_TILE, v_blk0=0)
        d = _attention(lamv, gs_col, qaT, ka, vaT, d, t_all, q_tile=nctx, q_blk0=seq // nctx, n_q=1,
                       k_rows=nctx, k_blk0=seq // nctx, v_chunks=1, v_chunk0=nkc - 1,
                       v_cols=nctx, v_blk0=TOK_TILE // nctx - 1)

        cx = _conv(u, w_dw[l], b_dw[l], g_conv_ln[l], b_conv_ln[l], seq=seq)

        wo = w_out[l].astype(BF16)
        xa = _mixout(xa, mod, l, hTf, hTb, omT, g_mlstm[l].reshape(MLSTM_WIDTH, 1).astype(F32), d, cx,
                     wo[0:MLSTM_WIDTH], wo[MLSTM_WIDTH:MLSTM_WIDTH + DIFF_WIDTH],
                     wo[MLSTM_WIDTH + DIFF_WIDTH:], seq=seq)

        jj = l // 2
        g2 = g_norm2[l].reshape(1, D_MODEL)
        if l % 2 == 0:
            xa = _ffn(xa, mod, l, g2, wfg, wfu, wfd, jj, seq=seq)
        else:
            wr = jnp.concatenate([w_router[jj], jnp.zeros((D_MODEL, LANE - N_EXPERTS), F32)], axis=1)
            wr_hi = wr.astype(BF16)
            wr = jnp.stack([wr_hi, (wr - wr_hi.astype(F32)).astype(BF16)])
            br = jnp.concatenate([b_router[jj], jnp.zeros((LANE - N_EXPERTS,), F32)]).reshape(1, LANE)
            xa = _moe(xa, mod, l, g2, wr, br, weg, weu, wed, jj, seq=seq)

    return _final_norm(xa, g_final, seq=seq)
```

```python
import functools
import math

import jax
import jax.numpy as jnp
import numpy as np
from jax import lax
from jax.experimental import pallas as pl
from jax.experimental.pallas import tpu as pltpu

F32 = jnp.float32
BF16 = jnp.bfloat16
HIGHEST = lax.Precision.HIGHEST

D_MODEL = 1024
DEPTH = 4
GRID_W = 64
CTX_LEN = 256
N_MLSTM_HEADS = 4
MLSTM_HEAD_DIM = 64
MLSTM_WIDTH = N_MLSTM_HEADS * MLSTM_HEAD_DIM
N_DIFF_HEADS = 4
DIFF_QK_DIM = 64
DIFF_V_DIM = 2 * DIFF_QK_DIM
DIFF_WIDTH = N_DIFF_HEADS * DIFF_V_DIM
ROPE_BASE = 10000.0
CONV_WIDTH = 256
CONV_KERNEL = 31
IN_SPLITS = (MLSTM_WIDTH, MLSTM_WIDTH, MLSTM_WIDTH, MLSTM_WIDTH, 4 * N_MLSTM_HEADS,
             2 * N_DIFF_HEADS * DIFF_QK_DIM, 2 * N_DIFF_HEADS * DIFF_QK_DIM, DIFF_WIDTH,
             2 * CONV_WIDTH)
D_FF = 2816
N_EXPERTS = 8
D_FF_EXPERT = 1408
EPS = 1e-6
M_INIT = -1e30
NEG_BIG = -1e30
LOG2E = 1.4426950408889634

LANE = 128
V7X_VMEM_LIMIT = 56 * 1024 * 1024
TOK_TILE = 768
MLSTM_CHUNK = 256
CONV_TILE = 256
CONV_HALO = 16
CONV_SHIFTS = 8
Q_TILE = 1024
MOD_ROWS = 8
VAUG = 16
MOE_BLOCKS = (128, 160, 192, 224, 256, 320, 384)
MOE_LOOP_BLOCK = 256
FFN_SPLIT = 2
MOE_GROUP = 1
MLSTM_BATCH = 4
ATT_KEYS = 256
ATT_COLS = 256
ATT_AHEAD = 4


def _cparams(sem):
    return pltpu.CompilerParams(dimension_semantics=sem, vmem_limit_bytes=V7X_VMEM_LIMIT)


def _sigmoid(v):
    return 1.0 / (1.0 + jnp.exp(-v))


def _log_sigmoid(v):
    return jnp.minimum(v, 0.0) - jnp.log(1.0 + jnp.exp(-jnp.abs(v)))


def _mod_rows(mod_ref, b, ctx_row, k):
    lat = mod_ref[0, pl.ds(b, 1), k * D_MODEL:(k + 1) * D_MODEL]
    ctx = mod_ref[0, ctx_row:ctx_row + 1, k * D_MODEL:(k + 1) * D_MODEL]
    return lat, ctx


def _is_ctx_rows(j, tm, seq):
    rows = j * tm + lax.broadcasted_iota(jnp.int32, (tm, 1), 0)
    return rows >= seq


def _rms_mod(x, g, mod_ref, b, ctx_row, is_ctx, k_shift, k_scale):
    y = x * lax.rsqrt(jnp.mean(x * x, axis=-1, keepdims=True) + EPS) * g
    sh_l, sh_c = _mod_rows(mod_ref, b, ctx_row, k_shift)
    sc_l, sc_c = _mod_rows(mod_ref, b, ctx_row, k_scale)
    shift = jnp.where(is_ctx, sh_c, sh_l)
    scale = jnp.where(is_ctx, sc_c, sc_l)
    return y * (1.0 + scale) + shift


def _mod_kernel(cond_ref, w_ref, b_ref, o_ref):
    c = cond_ref[...]
    s = c * _sigmoid(c)
    o_ref[0] = jnp.dot(s, w_ref[0], preferred_element_type=F32, precision=HIGHEST) + b_ref[0]


def _mod_table(cond, w_mod, b_mod):
    depth = w_mod.shape[0]
    n = w_mod.shape[2] // D_MODEL
    return pl.pallas_call(
        _mod_kernel,
        grid=(depth, n),
        in_specs=[pl.BlockSpec((MOD_ROWS, D_MODEL), lambda l, c: (0, 0)),
                  pl.BlockSpec((1, D_MODEL, D_MODEL), lambda l, c: (l, 0, c)),
                  pl.BlockSpec((1, 1, D_MODEL), lambda l, c: (l, 0, c))],
        out_specs=pl.BlockSpec((1, MOD_ROWS, D_MODEL), lambda l, c: (l, 0, c)),
        out_shape=jax.ShapeDtypeStruct((depth, MOD_ROWS, n * D_MODEL), F32),
        compiler_params=_cparams(("parallel", "parallel")),
        name="mod_table",
    )(cond, w_mod, b_mod.reshape(depth, 1, n * D_MODEL))


def _inproj_kernel(x_ref, mod_ref, g1_ref, ropeT_ref, ropeR_ref,
                   wTm_ref, wTg_ref, bgT_ref, wkm_ref,
                   wTaq_ref, wak_ref, wTav_ref, wcv_ref,
                   qmT_o, km_o, vmT_o, omT_o, gT_o, qaT_o, ka_o, vaT_o, u_o,
                   *, seq, tm, ctx_row):
    b = pl.program_id(0)
    j = pl.program_id(1)
    is_ctx = _is_ctx_rows(j, tm, seq)
    h = _rms_mod(x_ref[0], g1_ref[...], mod_ref, b, ctx_row, is_ctx, 0, 1)
    hb = h.astype(BF16)
    hT = h.T.astype(BF16)

    mT = jnp.dot(wTm_ref[...], hT, preferred_element_type=F32)
    w = MLSTM_WIDTH
    qmT_o[0] = mT[0:w].astype(BF16)
    vmT_o[0] = mT[w:2 * w].astype(BF16)
    omT_o[0] = mT[2 * w:3 * w].astype(BF16)
    gT = jnp.dot(wTg_ref[...], hT, preferred_element_type=F32) + bgT_ref[...]
    rowi = lax.broadcasted_iota(jnp.int32, gT.shape, 0)
    gT_o[0] = jnp.where((rowi % 8) >= 4, _log_sigmoid(gT), gT)
    km_o[0] = jnp.dot(hb, wkm_ref[...], preferred_element_type=F32).astype(BF16)

    qT = jnp.dot(wTaq_ref[...], hT, preferred_element_type=F32)
    for grp in range(2 * N_DIFF_HEADS * 2):
        ax = grp % 2
        cos = ropeT_ref[ax * 16:(ax + 1) * 16, :]
        sin = ropeT_ref[32 + ax * 16:32 + (ax + 1) * 16, :]
        x1 = qT[grp * 32:grp * 32 + 16]
        x2 = qT[grp * 32 + 16:grp * 32 + 32]
        qaT_o[0, grp * 32:grp * 32 + 16, :] = (x1 * cos - x2 * sin).astype(BF16)
        qaT_o[0, grp * 32 + 16:grp * 32 + 32, :] = (x2 * cos + x1 * sin).astype(BF16)
    kk = jnp.dot(hb, wak_ref[...], preferred_element_type=F32)
    cosr = ropeR_ref[:, 0:LANE]
    sinr = ropeR_ref[:, LANE:2 * LANE]
    half = DIFF_QK_DIM // 4
    first_half = (lax.broadcasted_iota(jnp.int32, (tm, LANE), 1) % (2 * half)) < half
    for sl in range(2 * N_DIFF_HEADS * DIFF_QK_DIM // LANE):
        k0 = kk[:, sl * LANE:(sl + 1) * LANE]
        k1 = jnp.where(first_half, pltpu.roll(k0, LANE - half, axis=1), pltpu.roll(k0, half, axis=1))
        ka_o[0, :, sl * LANE:(sl + 1) * LANE] = (k0 * cosr + k1 * sinr).astype(BF16)
    vT = jnp.dot(wTav_ref[...], hT, preferred_element_type=F32)
    for hh in range(N_DIFF_HEADS):
        vaT_o[0, 0, hh, 0:DIFF_V_DIM, :] = vT[hh * DIFF_V_DIM:(hh + 1) * DIFF_V_DIM].astype(BF16)
        vaT_o[0, 0, hh, DIFF_V_DIM:DIFF_V_DIM + VAUG, :] = jnp.ones((VAUG, tm), BF16)
    cv = jnp.dot(hb, wcv_ref[...], preferred_element_type=F32)
    u_o[0] = (cv[:, :CONV_WIDTH] * _sigmoid(cv[:, CONV_WIDTH:])).astype(BF16)


def _const_spec(shape):
    nd = len(shape)
    return pl.BlockSpec(shape, lambda *_: (0,) * nd)


def _inproj(xa, mod, layer, g1, ropeT, ropeR, wts, *, seq):
    bsz, t_all, _ = xa.shape
    tm = TOK_TILE
    nt = t_all // tm
    kern = functools.partial(_inproj_kernel, seq=seq, tm=tm, ctx_row=bsz)
    w = MLSTM_WIDTH
    out_shapes = (
        jax.ShapeDtypeStruct((bsz, w, t_all), BF16),
        jax.ShapeDtypeStruct((bsz, t_all, w), BF16),
        jax.ShapeDtypeStruct((bsz, w, t_all), BF16),
        jax.ShapeDtypeStruct((bsz, w, t_all), BF16),
        jax.ShapeDtypeStruct((bsz, 16, t_all), F32),
        jax.ShapeDtypeStruct((bsz, DIFF_WIDTH, t_all), BF16),
        jax.ShapeDtypeStruct((bsz, t_all, DIFF_WIDTH), BF16),
        jax.ShapeDtypeStruct((bsz, nt, N_DIFF_HEADS, DIFF_V_DIM + VAUG, tm), BF16),
        jax.ShapeDtypeStruct((bsz, t_all, CONV_WIDTH), BF16),
    )
    fm = lambda rows: pl.BlockSpec((1, rows, tm), lambda b, j: (b, 0, j))
    tk = lambda cols: pl.BlockSpec((1, tm, cols), lambda b, j: (b, j, 0))
    out_specs = (fm(w), tk(w), fm(w), fm(w), fm(16), fm(DIFF_WIDTH), tk(DIFF_WIDTH),
                 pl.BlockSpec((1, 1, N_DIFF_HEADS, DIFF_V_DIM + VAUG, tm), lambda b, j: (b, j, 0, 0, 0)),
                 tk(CONV_WIDTH))
    in_specs = [
        pl.BlockSpec((1, tm, D_MODEL), lambda b, j: (b, j, 0)),
        pl.BlockSpec((1, MOD_ROWS, 6 * D_MODEL), lambda b, j: (layer, 0, 0)),
        _const_spec((1, D_MODEL)),
        pl.BlockSpec((64, tm), lambda b, j: (0, j)),
        pl.BlockSpec((tm, 2 * LANE), lambda b, j: (j, 0)),
    ] + [_const_spec(a.shape) for a in wts]
    return pl.pallas_call(
        kern, grid=(bsz, nt), in_specs=in_specs, out_specs=out_specs, out_shape=out_shapes,
        compiler_params=_cparams(("parallel", "parallel")), name="inproj",
    )(xa, mod, g1, ropeT, ropeR, *wts)


def _mlstm_kernel(qf_ref, kf_ref, vf_ref, gTf_ref, qb_ref, kb_ref, vb_ref, gTb_ref,
                  hf_o, hb_o, c_scr, m_scr, *, chunk, nb):
    step = pl.program_id(1)
    L = chunk
    hd = MLSTM_HEAD_DIM
    nh = N_MLSTM_HEADS

    @pl.when(step == 0)
    def _():
        c_scr[...] = jnp.zeros(c_scr.shape, F32)
        m_scr[...] = jnp.full(m_scr.shape, M_INIT, F32)

    si = lax.broadcasted_iota(jnp.int32, (L, L), 0)
    ti = lax.broadcasted_iota(jnp.int32, (L, L), 1)
    ones_rows = jnp.ones((VAUG, L), BF16)
    zero64 = jnp.zeros((hd, L), BF16)

    dirs = ((qf_ref, kf_ref, vf_ref, gTf_ref, hf_o), (qb_ref, kb_ref, vb_ref, gTb_ref, hb_o))
    masks = []
    for d in range(2):
        causal = (si <= ti) if d == 0 else (si >= ti)
        masks.append((causal, jnp.where(causal, 1.0, 0.0).astype(F32)))

    chains = []
    for bb, d in [(bb, d) for bb in range(nb) for d in range(2)]:
        qT_ref, k_ref, vT_ref, gT_ref, h_o = dirs[d]
        causal, tri = masks[d]
        gT = gT_ref[bb]
        b_rows = jnp.dot(gT, tri, preferred_element_type=F32, precision=HIGHEST)
        totals = jnp.sum(gT, axis=1, keepdims=True)
        a_rows = gT - pltpu.roll(b_rows, nh, axis=0)
        a_cols = jnp.concatenate([a_rows, jnp.zeros((LANE - 2 * nh, L), F32)], axis=0).T
        for hh in range(nh):
            pair, half = hh // 2, hh % 2
            q_h = qT_ref[bb, hh * hd:(hh + 1) * hd, :]
            q_msk = jnp.concatenate([q_h, zero64] if half == 0 else [zero64, q_h], axis=0)
            k_pair = k_ref[bb, :, pair * 2 * hd:(pair + 1) * 2 * hd]
            v_aug = jnp.concatenate([vT_ref[bb, hh * hd:(hh + 1) * hd, :], ones_rows], axis=0)
            idx = (bb * 2 + d) * nh + hh
            c_st = c_scr[idx]
            sT = jnp.dot(k_pair, q_msk, preferred_element_type=F32)
            cq = jnp.dot(c_st.astype(BF16), q_msk, preferred_element_type=F32)
            chains.append(dict(idx=idx, hh=hh, bb=bb, h_o=h_o, causal=causal, k_pair=k_pair, v_aug=v_aug,
                               c_st=c_st, sT=sT, cq=cq, li_row=gT[hh:hh + 1], b_row=b_rows[4 + hh:5 + hh],
                               total=totals[4 + hh:5 + hh],
                               a_col=a_cols[:, hh:hh + 1]))

    for ch in chains:
        hh, b_row = ch["hh"], ch["b_row"]
        m_st = m_scr[ch["idx"], 0:1, :]
        dmat = jnp.where(ch["causal"], b_row + ch["a_col"], -jnp.inf)
        inter = b_row + m_st
        m_t = jnp.maximum(inter, jnp.max(dmat, axis=0, keepdims=True))
        wT = jnp.exp(dmat - m_t) * ch["sT"]
        e_inter = jnp.exp(inter - m_t)
        intra = jnp.dot(ch["v_aug"], wT.astype(BF16), preferred_element_type=F32)
        cq = ch["cq"]
        num = e_inter * cq[0:hd] + intra[0:hd]
        den = e_inter * cq[hd:hd + 1] + jnp.sum(wT, axis=0, keepdims=True)
        ch["h_o"][ch["bb"], hh * hd:(hh + 1) * hd, :] = num / jnp.maximum(jnp.abs(den), jnp.exp(-m_t))

        total = ch["total"]
        g_row = total - b_row + ch["li_row"]
        m_prev = m_st[:, 0:1]
        m_new = jnp.maximum(total + m_prev, jnp.max(g_row, axis=1, keepdims=True))
        e_old = jnp.exp(total + m_prev - m_new)
        e_g = jnp.exp(g_row - m_new)
        upd = jnp.dot((ch["v_aug"].astype(F32) * e_g).astype(BF16), ch["k_pair"],
                      preferred_element_type=F32)
        c_scr[ch["idx"]] = e_old * ch["c_st"] + upd
        m_scr[ch["idx"]] = jnp.broadcast_to(m_new, m_scr.shape[1:])


def _mlstm(qmT, km, vmT, gT, *, seq):
    bsz, w, t_all = qmT.shape
    L = MLSTM_CHUNK
    nlat = seq // L
    nch = t_all // L
    nctx = nch - nlat
    fwd = lambda i: jnp.where(i < nctx, nlat + i, i - nctx)
    bwd = lambda i: nch - 1 - i

    nb = MLSTM_BATCH if bsz % MLSTM_BATCH == 0 else 1

    def specs(chunk_of, d):
        return [pl.BlockSpec((nb, w, L), lambda b, i: (b, 0, chunk_of(i))),
                pl.BlockSpec((nb, L, w), lambda b, i: (b, chunk_of(i), 0)),
                pl.BlockSpec((nb, w, L), lambda b, i: (b, 0, chunk_of(i))),
                pl.BlockSpec((nb, 8, L), lambda b, i: (b, d, chunk_of(i)))]

    kern = functools.partial(_mlstm_kernel, chunk=L, nb=nb)
    out = jax.ShapeDtypeStruct((bsz, w, t_all), F32)
    nchain = 2 * nb * N_MLSTM_HEADS
    return pl.pallas_call(
        kern, grid=(bsz // nb, nch),
        in_specs=specs(fwd, 0) + specs(bwd, 1),
        out_specs=(pl.BlockSpec((nb, w, L), lambda b, i: (b, 0, fwd(i))),
                   pl.BlockSpec((nb, w, L), lambda b, i: (b, 0, bwd(i)))),
        out_shape=(out, out),
        scratch_shapes=[pltpu.VMEM((nchain, MLSTM_HEAD_DIM + VAUG, LANE), F32),
                        pltpu.VMEM((nchain, 8, L), F32)],
        compiler_params=_cparams(("parallel", "arbitrary")), name="mlstm_scan",
    )(qmT, km, vmT, gT, qmT, km, vmT, gT)


def _attn_kernel(lam_ref, gs_ref, qT_ref, k_ref, vT_ref, *rest, nch, tk):
    o_ref = rest[-1]
    tq = qT_ref.shape[2]
    qT = qT_ref[0]
    z = jnp.zeros((DIFF_QK_DIM, tq), BF16)
    rhs = jnp.concatenate([jnp.concatenate([qT[:DIFF_QK_DIM], z], axis=0),
                           jnp.concatenate([z, qT[DIFF_QK_DIM:]], axis=0)], axis=1)
    ncb = 2 * tq // ATT_COLS
    ms = [jnp.full((1, ATT_COLS), NEG_BIG, F32) for _ in range(ncb)]
    accs = [jnp.zeros((DIFF_V_DIM + VAUG, ATT_COLS), F32) for _ in range(ncb)]
    sub = tk // ATT_KEYS
    units = [(c, cb) for c in range(nch * sub) for cb in range(ncb)]

    def scores(c, cb):
        return jnp.dot(k_ref[0, c * ATT_KEYS:(c + 1) * ATT_KEYS, :],
                       rhs[:, cb * ATT_COLS:(cb + 1) * ATT_COLS], preferred_element_type=F32)

    pending = [scores(*u) for u in units[:ATT_AHEAD]]
    for i, (c, cb) in enumerate(units):
        sT = pending.pop(0)
        if i + ATT_AHEAD < len(units):
            pending.append(scores(*units[i + ATT_AHEAD]))
        vT = vT_ref[0, c // sub, 0, :, (c % sub) * ATT_KEYS:(c % sub + 1) * ATT_KEYS]
        m_new = jnp.maximum(ms[cb], jnp.max(sT, axis=0, keepdims=True))
        p = jnp.exp2(sT - m_new).astype(BF16)
        alpha = jnp.exp2(ms[cb] - m_new)
        accs[cb] = alpha * accs[cb] + jnp.dot(vT, p, preferred_element_type=F32)
        ms[cb] = m_new
    acc = jnp.concatenate(accs, axis=1)
    l = acc[DIFF_V_DIM:DIFF_V_DIM + 1]
    acc = acc[0:DIFF_V_DIM]

    lv = lam_ref[...]
    lam_init = lv[4:5, 0:1]
    lam = (jnp.exp(jnp.sum(lv[0:1] * lv[1:2], axis=1, keepdims=True))
           - jnp.exp(jnp.sum(lv[2:3] * lv[3:4], axis=1, keepdims=True)) + lam_init)
    oT = acc[:, :tq] / l[:, :tq] - lam * (acc[:, tq:] / l[:, tq:])
    oT = oT * lax.rsqrt(jnp.mean(oT * oT, axis=0, keepdims=True) + EPS) * gs_ref[...] * (1.0 - lam_init)
    o_ref[0] = oT.T.astype(BF16)


def _attention(lamv, gs_col, qaT, ka, vaT, d_prev, out_rows, *, q_tile, q_blk0, n_q, k_rows, k_blk0,
               v_chunks, v_chunk0, v_cols, v_blk0):
    bsz = qaT.shape[0]
    kern = functools.partial(_attn_kernel, nch=v_chunks, tk=v_cols)
    in_specs = [_const_spec(lamv.shape), _const_spec((DIFF_V_DIM, 1)),
                pl.BlockSpec((1, DIFF_V_DIM, q_tile), lambda b, h, i: (b, h, q_blk0 + i)),
                pl.BlockSpec((1, k_rows, DIFF_V_DIM), lambda b, h, i: (b, k_blk0, h)),
                pl.BlockSpec((1, v_chunks, 1, DIFF_V_DIM + VAUG, v_cols),
                             lambda b, h, i: (b, v_chunk0, h, 0, v_blk0))]
    args = [lamv, gs_col, qaT, ka, vaT]
    aliases = {}
    if d_prev is not None:
        in_specs.append(pl.BlockSpec(memory_space=pl.ANY))
        aliases = {len(args): 0}
        args.append(d_prev)
    return pl.pallas_call(
        kern, grid=(bsz, N_DIFF_HEADS, n_q), in_specs=in_specs,
        out_specs=pl.BlockSpec((1, q_tile, DIFF_V_DIM), lambda b, h, i: (b, q_blk0 + i, h)),
        out_shape=jax.ShapeDtypeStruct((bsz, out_rows, DIFF_WIDTH), BF16),
        input_output_aliases=aliases,
        compiler_params=_cparams(("parallel", "parallel", "arbitrary")), name="diff_attn",
    )(*args)


def _conv_kernel(l_ref, c_ref, r_ref, w_ref, b_ref, g_ref, bb_ref, o_ref, buf, shifted, *, seq, tc, nt):
    j = pl.program_id(1)
    start = j * tc
    lvalid = jnp.logical_and(j > 0, start != seq)
    rvalid = jnp.logical_and(j < nt - 1, start + tc != seq)
    hl = CONV_HALO
    buf[0:hl, :] = jnp.where(lvalid, l_ref[0].astype(F32), 0.0)
    buf[hl:hl + tc, :] = c_ref[0].astype(F32)
    buf[hl + tc:2 * hl + tc, :] = jnp.where(rvalid, r_ref[0].astype(F32), 0.0)
    pad = CONV_KERNEL // 2
    sub = 128
    sl = CONV_SHIFTS
    for r in range(sl):
        shifted[r] = buf[pl.ds(r, shifted.shape[1]), :]
    for r0 in range(0, tc, sub):
        accs = [jnp.zeros((sub, LANE), F32) for _ in range(CONV_WIDTH // LANE)]
        for t in range(CONV_KERNEL):
            q, r = divmod(hl - pad + t, sl)
            for cb in range(CONV_WIDTH // LANE):
                cols = slice(cb * LANE, (cb + 1) * LANE)
                accs[cb] = accs[cb] + w_ref[t:t + 1, cols] * shifted[r, pl.ds(r0 + sl * q, sub), cols]
        y = jnp.concatenate(accs, axis=1) + b_ref[...]
        mu = jnp.mean(y, axis=-1, keepdims=True)
        var = jnp.mean(jnp.square(y - mu), axis=-1, keepdims=True)
        z = (y - mu) * lax.rsqrt(var + EPS) * g_ref[...] + bb_ref[...]
        o_ref[0, r0:r0 + sub, :] = (z * _sigmoid(z)).astype(BF16)


def _conv(u, w_dw, b_dw, g_ln, b_ln, *, seq):
    bsz, t_all, cw = u.shape
    tc = CONV_TILE
    nt = t_all // tc
    r = tc // CONV_HALO
    nhalo = t_all // CONV_HALO
    kern = functools.partial(_conv_kernel, seq=seq, tc=tc, nt=nt)
    row = lambda a: a.reshape(1, cw)
    return pl.pallas_call(
        kern, grid=(bsz, nt),
        in_specs=[pl.BlockSpec((1, CONV_HALO, cw), lambda b, j: (b, jnp.maximum(j * r - 1, 0), 0)),
                  pl.BlockSpec((1, tc, cw), lambda b, j: (b, j, 0)),
                  pl.BlockSpec((1, CONV_HALO, cw), lambda b, j: (b, jnp.minimum((j + 1) * r, nhalo - 1), 0)),
                  _const_spec((CONV_KERNEL, cw)), _const_spec((1, cw)), _const_spec((1, cw)),
                  _const_spec((1, cw))],
        out_specs=pl.BlockSpec((1, tc, cw), lambda b, j: (b, j, 0)),
        out_shape=jax.ShapeDtypeStruct((bsz, t_all, cw), BF16),
        scratch_shapes=[pltpu.VMEM((tc + 2 * CONV_HALO, cw), F32),
                        pltpu.VMEM((CONV_SHIFTS, tc + 2 * CONV_HALO - CONV_SHIFTS, cw), F32)],
        compiler_params=_cparams(("parallel", "parallel")), name="conv_mixer",
    )(u, u, u, w_dw, row(b_dw), row(g_ln), row(b_ln))


def _mixout_kernel(x_ref, mod_ref, hf_ref, hb_ref, omT_ref, gm_ref, d_ref, c_ref, wm_ref, wd_ref, wc_ref, o_ref,
                   *, seq, tm, ctx_row):
    b = pl.program_id(0)
    j = pl.program_id(1)
    is_ctx = _is_ctx_rows(j, tm, seq)
    hT = hf_ref[0] + hb_ref[0]
    h4 = hT.reshape(N_MLSTM_HEADS, MLSTM_HEAD_DIM, tm)
    mu = jnp.mean(h4, axis=1, keepdims=True)
    var = jnp.mean(jnp.square(h4 - mu), axis=1, keepdims=True)
    hn = ((h4 - mu) * lax.rsqrt(var + EPS)).reshape(MLSTM_WIDTH, tm)
    mT = _sigmoid(omT_ref[0].astype(F32)) * hn * gm_ref[...]
    m = mT.T.astype(BF16)
    y = (jnp.dot(m, wm_ref[...], preferred_element_type=F32)
         + jnp.dot(d_ref[0], wd_ref[...], preferred_element_type=F32)
         + jnp.dot(c_ref[0], wc_ref[...], preferred_element_type=F32))
    g_l, g_c = _mod_rows(mod_ref, b, ctx_row, 2)
    o_ref[0] = x_ref[0] + jnp.where(is_ctx, g_c, g_l) * y


def _mixout(xa, mod, layer, hTf, hTb, omT, gm_col, d, cx, wm, wd, wc, *, seq):
    bsz, t_all, _ = xa.shape
    tm = TOK_TILE
    nt = t_all // tm
    kern = functools.partial(_mixout_kernel, seq=seq, tm=tm, ctx_row=bsz)
    w = MLSTM_WIDTH
    return pl.pallas_call(
        kern, grid=(bsz, nt),
        in_specs=[pl.BlockSpec((1, tm, D_MODEL), lambda b, j: (b, j, 0)),
                  pl.BlockSpec((1, MOD_ROWS, 6 * D_MODEL), lambda b, j: (layer, 0, 0)),
                  pl.BlockSpec((1, w, tm), lambda b, j: (b, 0, j)),
                  pl.BlockSpec((1, w, tm), lambda b, j: (b, 0, j)),
                  pl.BlockSpec((1, w, tm), lambda b, j: (b, 0, j)),
                  _const_spec((w, 1)),
                  pl.BlockSpec((1, tm, DIFF_WIDTH), lambda b, j: (b, j, 0)),
                  pl.BlockSpec((1, tm, CONV_WIDTH), lambda b, j: (b, j, 0)),
                  _const_spec(wm.shape), _const_spec(wd.shape), _const_spec(wc.shape)],
        out_specs=pl.BlockSpec((1, tm, D_MODEL), lambda b, j: (b, j, 0)),
        out_shape=jax.ShapeDtypeStruct(xa.shape, F32),
        compiler_params=_cparams(("parallel", "parallel")), name="mix_out",
    )(xa, mod, hTf, hTb, omT, gm_col, d, cx, wm, wd, wc)


def _swiglu_partial(hb, wg, wu, wd):
    a = jnp.dot(hb, wg, preferred_element_type=F32)
    u = jnp.dot(hb, wu, preferred_element_type=F32)
    t = (a * _sigmoid(a) * u).astype(BF16)
    return jnp.dot(t, wd, preferred_element_type=F32)


def _ffn_kernel(x_ref, mod_ref, g2_ref, wg_ref, wu_ref, wd_ref, o_ref, hb_scr, acc_scr,
                *, seq, tm, ctx_row, nf):
    b = pl.program_id(0)
    j = pl.program_id(1)
    f = pl.program_id(2)
    is_ctx = _is_ctx_rows(j, tm, seq)

    @pl.when(f == 0)
    def _():
        h = _rms_mod(x_ref[0], g2_ref[...], mod_ref, b, ctx_row, is_ctx, 3, 4)
        hb_scr[...] = h.astype(BF16)
        acc_scr[...] = jnp.zeros(acc_scr.shape, F32)

    acc_scr[...] += _swiglu_partial(hb_scr[...], wg_ref[0], wu_ref[0], wd_ref[0])

    @pl.when(f == nf - 1)
    def _():
        g_l, g_c = _mod_rows(mod_ref, b, ctx_row, 5)
        o_ref[0] = x_ref[0] + jnp.where(is_ctx, g_c, g_l) * acc_scr[...]


def _ffn(xa, mod, layer, g2, wg, wu, wd, ffn_layer, *, seq):
    bsz, t_all, _ = xa.shape
    tm = TOK_TILE
    nt = t_all // tm
    nf = FFN_SPLIT
    tf = wg.shape[2] // nf
    assert tf % LANE == 0 and tf * nf == wg.shape[2]
    kern = functools.partial(_ffn_kernel, seq=seq, tm=tm, ctx_row=bsz, nf=nf)
    return pl.pallas_call(
        kern, grid=(bsz, nt, nf),
        in_specs=[pl.BlockSpec((1, tm, D_MODEL), lambda b, j, f: (b, j, 0)),
                  pl.BlockSpec((1, MOD_ROWS, 6 * D_MODEL), lambda b, j, f: (layer, 0, 0)),
                  _const_spec((1, D_MODEL)),
                  pl.BlockSpec((1, D_MODEL, tf), lambda b, j, f: (ffn_layer, 0, f)),
                  pl.BlockSpec((1, D_MODEL, tf), lambda b, j, f: (ffn_layer, 0, f)),
                  pl.BlockSpec((1, tf, D_MODEL), lambda b, j, f: (ffn_layer, f, 0))],
        out_specs=pl.BlockSpec((1, tm, D_MODEL), lambda b, j, f: (b, j, 0)),
        out_shape=jax.ShapeDtypeStruct(xa.shape, F32),
        scratch_shapes=[pltpu.VMEM((tm, D_MODEL), BF16), pltpu.VMEM((tm, D_MODEL), F32)],
        compiler_params=_cparams(("parallel", "parallel", "arbitrary")), name="ffn_swiglu",
    )(xa, mod, g2, wg, wu, wd)


def _moe_kernel(x_ref, mod_ref, g2_ref, wr_ref, br_ref, wg_ref, wu_ref, wd_ref, o_ref,
                hb_scr, acc_scr, comb_scr, slot_scr, slotT_scr, *, seq, tm, nt, ctx_row, ne, group):
    step = pl.program_id(0)
    e = pl.program_id(1)
    lane = lax.broadcasted_iota(jnp.int32, (tm, LANE), 1)

    for half in range(group):
        tile = step * group + half
        b = tile // nt
        is_ctx = _is_ctx_rows(tile % nt, tm, seq)
        rows = slice(half * tm, (half + 1) * tm)

        @pl.when(e == 0)
        def _():
            h = _rms_mod(x_ref[rows, :], g2_ref[...], mod_ref, b, ctx_row, is_ctx, 3, 4)
            h_hi = h.astype(BF16)
            hb_scr[half] = h_hi
            acc_scr[half] = jnp.zeros((tm, D_MODEL), F32)
            h_lo = (h - h_hi.astype(F32)).astype(BF16)
            logits = (jnp.dot(h_hi, wr_ref[0], preferred_element_type=F32)
                      + jnp.dot(h_lo, wr_ref[0], preferred_element_type=F32)
                      + jnp.dot(h_hi, wr_ref[1], preferred_element_type=F32)) + br_ref[...]
            logits = jnp.where(lane < ne, logits, -jnp.inf)
            ex = jnp.exp(logits - jnp.max(logits, axis=-1, keepdims=True))
            probs = ex / jnp.sum(ex, axis=-1, keepdims=True)
            v1 = jnp.max(probs, axis=-1, keepdims=True)
            i1 = jnp.min(jnp.where(probs == v1, lane, LANE), axis=-1, keepdims=True)
            rest = jnp.where(lane == i1, -1.0, probs)
            v2 = jnp.max(rest, axis=-1, keepdims=True)
            i2 = jnp.min(jnp.where(rest == v2, lane, LANE), axis=-1, keepdims=True)
            tot = v1 + v2
            comb_scr[half] = jnp.where(lane == i1, v1 / tot, 0.0) + jnp.where(lane == i2, v2 / tot, 0.0)
            sel = jnp.logical_or(lane == i1, lane == i2)
            ri = lax.broadcasted_iota(jnp.int32, (tm, tm), 0)
            ci = lax.broadcasted_iota(jnp.int32, (tm, tm), 1)
            before = jnp.where(ci < ri, 1.0, 0.0).astype(BF16)
            rank = jnp.dot(before, jnp.where(sel, 1.0, 0.0).astype(BF16), preferred_element_type=F32)
            slot = jnp.where(sel, rank, -1.0)
            slot_scr[half] = slot
            slotT_scr[half] = slot.T

        onlane = lane == e
        cw = jnp.sum(jnp.where(onlane, comb_scr[half], 0.0), axis=-1, keepdims=True)
        slot_c = jnp.max(jnp.where(onlane, slot_scr[half], -1.0), axis=-1, keepdims=True)
        slot_r = slotT_scr[half, pl.ds(e, 1), :]
        count = (jnp.max(slot_r) + 1.0).astype(jnp.int32)

        def run_block(base, size):
            row_i = lax.broadcasted_iota(jnp.int32, (size, 1), 0).astype(F32)
            col_i = lax.broadcasted_iota(jnp.int32, (1, size), 1).astype(F32)
            gather = jnp.where(slot_r == base + row_i, 1.0, 0.0).astype(BF16)
            xs = jnp.dot(gather, hb_scr[half], preferred_element_type=F32).astype(BF16)
            y = _swiglu_partial(xs, wg_ref[0, 0], wu_ref[0, 0], wd_ref[0, 0])
            scatter = jnp.where(slot_c == base + col_i, 1.0, 0.0).astype(BF16)
            acc_scr[half] += cw * jnp.dot(scatter, y.astype(BF16), preferred_element_type=F32)

        n_loop = jnp.maximum((count - MOE_BLOCKS[-1] + MOE_LOOP_BLOCK - 1) // MOE_LOOP_BLOCK, 0)

        def body(i, carry):
            run_block((i * MOE_LOOP_BLOCK).astype(F32), MOE_LOOP_BLOCK)
            return carry
        lax.fori_loop(0, n_loop, body, 0)
        done = n_loop * MOE_LOOP_BLOCK
        rest = count - done
        lo = 0
        for size in MOE_BLOCKS:
            pl.when(jnp.logical_and(rest > lo, rest <= size))(
                functools.partial(run_block, done.astype(F32), size))
            lo = size

        @pl.when(e == ne - 1)
        def _():
            g_l, g_c = _mod_rows(mod_ref, b, ctx_row, 5)
            o_ref[rows, :] = x_ref[rows, :] + jnp.where(is_ctx, g_c, g_l) * acc_scr[half]


def _moe(xa, mod, layer, g2, wr, br, wg, wu, wd, moe_layer, *, seq):
    bsz, t_all, _ = xa.shape
    tm = TOK_TILE
    nt = t_all // tm
    group = MOE_GROUP
    assert (bsz * nt) % group == 0
    ne, fe = wg.shape[1], wg.shape[3]
    kern = functools.partial(_moe_kernel, seq=seq, tm=tm, nt=nt, ctx_row=bsz, ne=ne, group=group)
    out = pl.pallas_call(
        kern, grid=(bsz * nt // group, ne),
        in_specs=[pl.BlockSpec((group * tm, D_MODEL), lambda s, e: (s, 0)),
                  pl.BlockSpec((1, MOD_ROWS, 6 * D_MODEL), lambda s, e: (layer, 0, 0)),
                  _const_spec((1, D_MODEL)), _const_spec(wr.shape), _const_spec(br.shape),
                  pl.BlockSpec((1, 1, D_MODEL, fe), lambda s, e: (moe_layer, e, 0, 0)),
                  pl.BlockSpec((1, 1, D_MODEL, fe), lambda s, e: (moe_layer, e, 0, 0)),
                  pl.BlockSpec((1, 1, fe, D_MODEL), lambda s, e: (moe_layer, e, 0, 0))],
        out_specs=pl.BlockSpec((group * tm, D_MODEL), lambda s, e: (s, 0)),
        out_shape=jax.ShapeDtypeStruct((bsz * t_all, D_MODEL), F32),
        scratch_shapes=[pltpu.VMEM((group, tm, D_MODEL), BF16), pltpu.VMEM((group, tm, D_MODEL), F32),
                        pltpu.VMEM((group, tm, LANE), F32), pltpu.VMEM((group, tm, LANE), F32),
                        pltpu.VMEM((group, LANE, tm), F32)],
        compiler_params=_cparams(("parallel", "arbitrary")), name="moe_swiglu",
    )(xa.reshape(bsz * t_all, D_MODEL), mod, g2, wr, br, wg, wu, wd)
    return out.reshape(xa.shape)


def _final_kernel(x_ref, g_ref, o_ref):
    x = x_ref[0]
    o_ref[0] = x * lax.rsqrt(jnp.mean(x * x, axis=-1, keepdims=True) + EPS) * g_ref[...]


def _final_norm(xa, g, *, seq):
    bsz = xa.shape[0]
    tf = 512
    return pl.pallas_call(
        _final_kernel, grid=(bsz, seq // tf),
        in_specs=[pl.BlockSpec((1, tf, D_MODEL), lambda b, j: (b, j, 0)), _const_spec((1, D_MODEL))],
        out_specs=pl.BlockSpec((1, tf, D_MODEL), lambda b, j: (b, j, 0)),
        out_shape=jax.ShapeDtypeStruct((bsz, seq, D_MODEL), F32),
        compiler_params=_cparams(("parallel", "parallel")), name="final_norm",
    )(xa, g.reshape(1, D_MODEL))


def _rope_tables(seq, t_all):
    pos = np.arange(seq)
    per_axis = DIFF_QK_DIM // 2
    inv = (ROPE_BASE ** (-np.arange(0, per_axis, 2, dtype=np.float32) / per_axis)).astype(np.float32)
    rowp = (pos // GRID_W).astype(np.float32)
    colp = (pos % GRID_W).astype(np.float32)
    ang = np.stack([rowp[:, None] * inv, colp[:, None] * inv], axis=1).astype(np.float64)
    cos = np.concatenate([np.cos(ang), np.ones((t_all - seq, 2, 16))], axis=0)
    sin = np.concatenate([np.sin(ang), np.zeros((t_all - seq, 2, 16))], axis=0)
    qscale = (DIFF_QK_DIM ** -0.5) * LOG2E
    ropeT = np.concatenate([cos.reshape(t_all, 32).T, sin.reshape(t_all, 32).T], axis=0) * qscale
    cos64 = np.concatenate([cos[:, 0], cos[:, 0], cos[:, 1], cos[:, 1]], axis=-1)
    sin64 = np.concatenate([-sin[:, 0], sin[:, 0], -sin[:, 1], sin[:, 1]], axis=-1)
    ropeR = np.concatenate([cos64, cos64, sin64, sin64], axis=-1)
    return jnp.asarray(ropeT, F32), jnp.asarray(ropeR, F32)


def _prep_inproj_weights(w_in_l, b_gates_l):
    offs = np.cumsum((0,) + IN_SPLITS)
    col = lambda i: w_in_l[:, offs[i]:offs[i + 1]]
    mq, mk, mv, mo, gt, aq, ak, av, cv = (col(i) for i in range(9))
    nh = N_MLSTM_HEADS
    wTm = jnp.concatenate([mq, mv, mo], axis=1).T.astype(BF16)
    wTg = gt.T.astype(BF16)
    bgT = b_gates_l.reshape(4 * nh, 1).astype(F32)
    wkm = (mk * (MLSTM_HEAD_DIM ** -0.5)).astype(BF16)
    wTaq = aq.T.astype(BF16)
    wak = ak.astype(BF16)
    wTav = av.T.astype(BF16)
    wcv = cv.astype(BF16)
    return (wTm, wTg, bgT, wkm, wTaq, wak, wTav, wcv)


def kernel(x, c, ctx, c_ctx, w_mod, b_mod, g_norm1, w_in, b_gates, g_mlstm, lambda_q1, lambda_k1,
           lambda_q2, lambda_k2, g_subln, w_dw, b_dw, g_conv_ln, b_conv_ln, w_out, g_norm2,
           w_ffn_gate, w_ffn_up, w_ffn_down, w_router, b_router, w_exp_gate, w_exp_up, w_exp_down,
           g_final):
    bsz, seq, _ = x.shape
    nctx = ctx.shape[1]
    t_all = seq + nctx
    depth = w_mod.shape[0]
    assert nctx == CTX_LEN == MLSTM_CHUNK and bsz + 1 <= MOD_ROWS
    assert t_all % TOK_TILE == 0 and seq % Q_TILE == 0 and seq % CONV_TILE == 0 and seq % GRID_W == 0

    xa = jnp.concatenate([x, ctx], axis=1)
    cond = jnp.concatenate([c, c_ctx[None, :], jnp.zeros((MOD_ROWS - bsz - 1, D_MODEL), F32)], axis=0)
    mod = _mod_table(cond, w_mod, b_mod)
    ropeT, ropeR = _rope_tables(seq, t_all)
    nkc = t_all // TOK_TILE
    weg, weu, wed = w_exp_gate.astype(BF16), w_exp_up.astype(BF16), w_exp_down.astype(BF16)
    wfg, wfu, wfd = w_ffn_gate.astype(BF16), w_ffn_up.astype(BF16), w_ffn_down.astype(BF16)

    for l in range(depth):
        wts = _prep_inproj_weights(w_in[l], b_gates[l])
        (qmT, km, vmT, omT, gT, qaT, ka, vaT, u) = _inproj(
            xa, mod, l, g_norm1[l].reshape(1, D_MODEL), ropeT, ropeR, wts, seq=seq)

        hTf, hTb = _mlstm(qmT, km, vmT, gT, seq=seq)

        lam_init = 0.8 - 0.6 * math.exp(-0.3 * l)
        lamv = jnp.zeros((8, LANE), F32).at[0:4, 0:DIFF_QK_DIM].set(
            jnp.stack([lambda_q1[l], lambda_k1[l], lambda_q2[l], lambda_k2[l]]).astype(F32))
        lamv = lamv.at[4, :].set(lam_init)
        gs_col = g_subln[l].reshape(DIFF_V_DIM, 1).astype(F32)
        d = _attention(lamv, gs_col, qaT, ka, vaT, None, t_all, q_tile=Q_TILE, q_blk0=0,
                       n_q=seq // Q_TILE, k_rows=t_all, k_blk0=0, v_chunks=nkc, v_chunk0=0,
                       v_cols=TOK_TILE, v_blk0=0)
        d = _attention(lamv, gs_col, qaT, ka, vaT, d, t_all, q_tile=nctx, q_blk0=seq // nctx, n_q=1,
                       k_rows=nctx, k_blk0=seq // nctx, v_chunks=1, v_chunk0=nkc - 1,
                       v_cols=nctx, v_blk0=TOK_TILE // nctx - 1)

        cx = _conv(u, w_dw[l], b_dw[l], g_conv_ln[l], b_conv_ln[l], seq=seq)

        wo = w_out[l].astype(BF16)
        xa = _mixout(xa, mod, l, hTf, hTb, omT, g_mlstm[l].reshape(MLSTM_WIDTH, 1).astype(F32), d, cx,
                     wo[0:MLSTM_WIDTH], wo[MLSTM_WIDTH:MLSTM_WIDTH + DIFF_WIDTH],
                     wo[MLSTM_WIDTH + DIFF_WIDTH:], seq=seq)

        jj = l // 2
        g2 = g_norm2[l].reshape(1, D_MODEL)
        if l % 2 == 0:
            xa = _ffn(xa, mod, l, g2, wfg, wfu, wfd, jj, seq=seq)
        else:
            wr = jnp.concatenate([w_router[jj], jnp.zeros((D_MODEL, LANE - N_EXPERTS), F32)], axis=1)
            wr_hi = wr.astype(BF16)
            wr = jnp.stack([wr_hi, (wr - wr_hi.astype(F32)).astype(BF16)])
            br = jnp.concatenate([b_router[jj], jnp.zeros((LANE - N_EXPERTS,), F32)]).reshape(1, LANE)
            xa = _moe(xa, mod, l, g2, wr, br, weg, weu, wed, jj, seq=seq)

    return _final_norm(xa, g_final, seq=seq)
```
